```python
import math
import jax, jax.numpy as jnp
from jax import lax
import numpy as np


D_MODEL = 1024
BATCH = 16
SEQ = 2048
DEPTH = 2

N_Q_HEADS = 8
N_KV_HEADS = 2
Q_PER_KV = N_Q_HEADS // N_KV_HEADS
HEAD_DIM = D_MODEL // 16
WINDOW = 128
BLOCK = 128
BRANCH_WIDTH = D_MODEL // 2
KV_WIDTH = N_KV_HEADS * HEAD_DIM
LRU_BLOCKS = 8
LRU_BLOCK_DIM = BRANCH_WIDTH // LRU_BLOCKS
CONV_WIDTH = 4
CONV_PAD = (2, 1)
LRU_C = 8.0
SSM_GROUP = 16
SSM_GROUPS = BRANCH_WIDTH // SSM_GROUP
SSM_STATE = 64
N_BRANCH = 3
IN_COLS = BRANCH_WIDTH + 2 * KV_WIDTH + 3 * BRANCH_WIDTH + N_BRANCH * D_MODEL
N_GROUPS = 4
EXPERTS_PER_GROUP = 4
TOP_K = 2
D_EXPERT = D_MODEL // 4
NORM_EPS = 1e-6
MASK_VALUE = -1e30

kernel_name = 'hybrid_gated_swa_rglru_s5_hmoe'


def rms_norm(x, g):
    xf = x.astype(jnp.float32)
    y = xf * lax.rsqrt(jnp.mean(xf * xf, axis=-1, keepdims=True) + NORM_EPS)
    return (y * g.astype(jnp.float32)).astype(x.dtype)


def _linear_combine(e1, e2):
    a1, b1 = e1
    a2, b2 = e2
    return a1 * a2, a2 * b1 + b2


def _complex_combine(e1, e2):
    ar1, ai1, br1, bi1 = e1
    ar2, ai2, br2, bi2 = e2
    return (ar1 * ar2 - ai1 * ai2,
            ar1 * ai2 + ai1 * ar2,
            ar2 * br1 - ai2 * bi1 + br2,
            ar2 * bi1 + ai2 * br1 + bi2)


def alibi_slopes():
    return jnp.asarray(2.0 ** (-8.0 * (np.arange(N_Q_HEADS) + 1) / N_Q_HEADS), dtype=jnp.float32)


def window_attention(q, k, v, sink):
    b, l = q.shape[0], q.shape[1]
    nb = l // BLOCK
    qb = q.reshape(b, nb, BLOCK, N_KV_HEADS, Q_PER_KV, HEAD_DIM)

    def band(t):
        tp = jnp.pad(t, ((0, 0), (BLOCK, BLOCK), (0, 0), (0, 0)))
        tp = tp.reshape(b, nb + 2, BLOCK, N_KV_HEADS, HEAD_DIM)
        return jnp.concatenate([tp[:, :-2], tp[:, 1:-1], tp[:, 2:]], axis=2)

    kb, vb = band(k), band(v)
    s = jnp.einsum('bnqhgd,bnshd->bnhgqs', qb, kb,
                   preferred_element_type=jnp.float32) * (HEAD_DIM ** -0.5)
    qi = jnp.arange(BLOCK)
    kj = jnp.arange(3 * BLOCK)
    rel = qi[:, None] - kj[None, :] + BLOCK
    key_pos = jnp.arange(nb)[:, None] * BLOCK - BLOCK + kj[None, :]
    valid = (jnp.abs(rel) <= WINDOW)[None] & ((key_pos >= 0) & (key_pos < l))[:, None, :]
    dist = jnp.abs(rel).astype(jnp.float32)
    bias = (-alibi_slopes()[:, None, None] * dist).reshape(N_KV_HEADS, Q_PER_KV, BLOCK, 3 * BLOCK)
    s = jnp.where(valid[None, :, None, None], s + bias, MASK_VALUE)
    sk = sink.astype(jnp.float32).reshape(N_KV_HEADS, Q_PER_KV)[:, :, None, None]
    m = jnp.maximum(jnp.max(s, axis=-1, keepdims=True), sk)
    p = jnp.exp(s - m)
    probs = p / (jnp.sum(p, axis=-1, keepdims=True) + jnp.exp(sk - m))
    o = jnp.einsum('bnhgqs,bnshd->bnqhgd', probs, vb.astype(jnp.float32))
    return o.reshape(b, l, N_Q_HEADS * HEAD_DIM).astype(q.dtype)


def centred_dwconv(x, w, bias):
    y = lax.conv_general_dilated(x, w[:, None, :].astype(x.dtype), window_strides=(1,),
                                 padding=[CONV_PAD], dimension_numbers=('NWC', 'WIO', 'NWC'),
                                 feature_group_count=x.shape[-1])
    return y + bias.astype(x.dtype)


def rg_lru_dir(xc, w_r, b_r, w_i, b_i, lam, reverse):
    b, l, _ = xc.shape
    xb = xc.reshape(b, l, LRU_BLOCKS, LRU_BLOCK_DIM)
    r = jax.nn.sigmoid(jnp.einsum('blhi,hij->blhj', xb, w_r.astype(jnp.float32)).reshape(b, l, BRANCH_WIDTH) + b_r)
    i = jax.nn.sigmoid(jnp.einsum('blhi,hij->blhj', xb, w_i.astype(jnp.float32)).reshape(b, l, BRANCH_WIDTH) + b_i)
    log_a = -LRU_C * r * jax.nn.softplus(-lam)
    a = jnp.exp(log_a)
    u = jnp.sqrt(-jnp.expm1(2.0 * log_a)) * (i * xc)
    _, h = lax.associative_scan(_linear_combine, (a, u), axis=1, reverse=reverse)
    return h


def s5_dir(u, a_re, a_im, log_step, b_re, b_im, reverse):
    bsz, l = u.shape[0], u.shape[1]
    step = jnp.exp(log_step)[:, None]
    mag = jnp.exp(a_re * step)
    abar_r = mag * jnp.cos(a_im * step)
    abar_i = mag * jnp.sin(a_im * step)
    den = a_re * a_re + a_im * a_im
    nr = abar_r - 1.0
    sr = (nr * a_re + abar_i * a_im) / den
    si = (abar_i * a_re - nr * a_im) / den
    bbar_r = sr[..., None] * b_re - si[..., None] * b_im
    bbar_i = sr[..., None] * b_im + si[..., None] * b_re
    bu_r = jnp.einsum('blgc,gnc->blgn', u, bbar_r)
    bu_i = jnp.einsum('blgc,gnc->blgn', u, bbar_i)
    ar = jnp.broadcast_to(abar_r[None, None], (bsz, l, SSM_GROUPS, SSM_STATE))
    ai = jnp.broadcast_to(abar_i[None, None], (bsz, l, SSM_GROUPS, SSM_STATE))
    _, _, hr, hi = lax.associative_scan(_complex_combine, (ar, ai, bu_r, bu_i), axis=1, reverse=reverse)
    return hr, hi


def hybrid_mixer(h, w_in, attn_sink, conv_w, conv_b, w_r, b_r, w_i, b_i, lam,
                 a_re, a_im, log_step, b_re, b_im, c_re, c_im, d_skip, w_glu, b_glu,
                 w_branch, w_out):
    b, l, _ = h.shape
    f32 = jnp.float32
    z = h @ w_in
    cuts = [BRANCH_WIDTH, BRANCH_WIDTH + KV_WIDTH, BRANCH_WIDTH + 2 * KV_WIDTH,
            2 * BRANCH_WIDTH + 2 * KV_WIDTH, 3 * BRANCH_WIDTH + 2 * KV_WIDTH,
            4 * BRANCH_WIDTH + 2 * KV_WIDTH]
    q, k, v, xr, gr, u, gates = jnp.split(z, cuts, axis=-1)

    o_attn = window_attention(q.reshape(b, l, N_Q_HEADS, HEAD_DIM),
                              k.reshape(b, l, N_KV_HEADS, HEAD_DIM),
                              v.reshape(b, l, N_KV_HEADS, HEAD_DIM), attn_sink)

    xc = centred_dwconv(xr, conv_w, conv_b).astype(f32)
    h_f = rg_lru_dir(xc, w_r[0], b_r[0].astype(f32), w_i[0], b_i[0].astype(f32), lam[0].astype(f32), False)
    h_b = rg_lru_dir(xc, w_r[1], b_r[1].astype(f32), w_i[1], b_i[1].astype(f32), lam[1].astype(f32), True)
    o_lru = (jax.nn.gelu(gr.astype(f32)) * (h_f + h_b)).astype(h.dtype)

    uf = u.astype(f32).reshape(b, l, SSM_GROUPS, SSM_GROUP)
    hr_f, hi_f = s5_dir(uf, a_re[0].astype(f32), a_im[0].astype(f32), log_step[0].astype(f32),
                        b_re[0].astype(f32), b_im[0].astype(f32), False)
    hr_b, hi_b = s5_dir(uf, a_re[1].astype(f32), a_im[1].astype(f32), log_step[1].astype(f32),
                        b_re[1].astype(f32), b_im[1].astype(f32), True)
    hr = hr_f + hr_b
    hi = hi_f + hi_b
    y = (jnp.einsum('blgn,gcn->blgc', hr, c_re.astype(f32))
         - jnp.einsum('blgn,gcn->blgc', hi, c_im.astype(f32))
         + d_skip.astype(f32).reshape(SSM_GROUPS, SSM_GROUP) * uf)
    y = y.reshape(b, l, BRANCH_WIDTH)
    zg = jax.nn.gelu(y) @ w_glu.astype(f32) + b_glu.astype(f32)
    o_ssm = (zg[..., :BRANCH_WIDTH] * jax.nn.sigmoid(zg[..., BRANCH_WIDTH:])).astype(h.dtype)

    g = jax.nn.sigmoid(gates.reshape(b, l, N_BRANCH, D_MODEL))
    branches = jnp.stack([o_attn, o_lru, o_ssm], axis=2)
    proj = jnp.einsum('blkw,kwd->blkd', branches, w_branch)
    mixed = jnp.sum(proj * g, axis=2)
    return mixed @ w_out


def hier_moe(x, w_group, b_group, w_expert, b_expert, w1, w3, w2):
    b, l, d = x.shape
    t = b * l
    xt = x.reshape(t, d)
    g_logits = (xt @ w_group).astype(jnp.float32) + b_group.astype(jnp.float32)
    g_prob = jax.nn.softmax(g_logits, axis=-1)
    _, g_idx = lax.top_k(g_logits, 1)
    g_w = jnp.take_along_axis(g_prob, g_idx, axis=-1)
    e_logits = ((xt @ w_expert).astype(jnp.float32) + b_expert.astype(jnp.float32)).reshape(t, N_GROUPS, EXPERTS_PER_GROUP)
    g_onehot = jax.nn.one_hot(g_idx[:, 0], N_GROUPS, dtype=jnp.float32)
    e_sel = jnp.sum(e_logits * g_onehot[:, :, None], axis=1)
    top_v, top_i = lax.top_k(e_sel, TOP_K)
    top_w = jax.nn.softmax(top_v, axis=-1) * g_w
    w_e = jnp.sum(jax.nn.one_hot(top_i, EXPERTS_PER_GROUP, dtype=jnp.float32) * top_w[..., None], axis=1)
    combine = w_e[:, None, :] * g_onehot[:, :, None]
    y = jnp.zeros((t, d), jnp.float32)
    for gi in range(N_GROUPS):
        hg = jax.nn.silu(jnp.einsum('td,edf->tef', xt, w1[gi])) * jnp.einsum('td,edf->tef', xt, w3[gi])
        hg = hg.astype(jnp.float32) * combine[:, gi, :, None]
        y = y + jnp.einsum('tef,efd->td', hg, w2[gi].astype(jnp.float32))
    return y.reshape(b, l, d).astype(x.dtype)


def setup_inputs(seed: int = 0) -> dict:
    key = jax.random.key(seed)
    ks = iter(jax.random.split(key, 64))
    f32 = jnp.float32

    def nrm(shape, scale):
        return scale * jax.random.normal(next(ks), shape, f32)

    def unif(shape, lo, hi):
        return jax.random.uniform(next(ks), shape, f32, lo, hi)

    lam_u = unif((DEPTH, 2, BRANCH_WIDTH), 0.9, 0.999) ** (1.0 / LRU_C)
    lru_lambda = jnp.log(lam_u) - jnp.log1p(-lam_u)
    a_im_base = jnp.pi * jnp.arange(SSM_STATE, dtype=f32)
    return {
        'x': nrm((BATCH, SEQ, D_MODEL), 1.0),
        'norm1_g': 1.0 + nrm((DEPTH, D_MODEL), 0.01),
        'w_in': nrm((DEPTH, D_MODEL, IN_COLS), D_MODEL ** -0.5),
        'attn_sink': nrm((DEPTH, N_Q_HEADS), 0.5),
        'lru_conv_w': nrm((DEPTH, CONV_WIDTH, BRANCH_WIDTH), CONV_WIDTH ** -0.5),
        'lru_conv_b': nrm((DEPTH, BRANCH_WIDTH), 0.01),
        'lru_w_r': nrm((DEPTH, 2, LRU_BLOCKS, LRU_BLOCK_DIM, LRU_BLOCK_DIM), LRU_BLOCK_DIM ** -0.5),
        'lru_b_r': nrm((DEPTH, 2, BRANCH_WIDTH), 0.01),
        'lru_w_i': nrm((DEPTH, 2, LRU_BLOCKS, LRU_BLOCK_DIM, LRU_BLOCK_DIM), LRU_BLOCK_DIM ** -0.5),
        'lru_b_i': nrm((DEPTH, 2, BRANCH_WIDTH), 0.01),
        'lru_lambda': lru_lambda,
        'ssm_a_re': -0.5 + nrm((DEPTH, 2, SSM_GROUPS, SSM_STATE), 0.01),
        'ssm_a_im': a_im_base + nrm((DEPTH, 2, SSM_GROUPS, SSM_STATE), 0.01),
        'ssm_log_step': unif((DEPTH, 2, SSM_GROUPS), math.log(1e-3), math.log(1e-1)),
        'ssm_b_re': nrm((DEPTH, 2, SSM_GROUPS, SSM_STATE, SSM_GROUP), (2 * SSM_GROUP) ** -0.5),
        'ssm_b_im': nrm((DEPTH, 2, SSM_GROUPS, SSM_STATE, SSM_GROUP), (2 * SSM_GROUP) ** -0.5),
        'ssm_c_re': nrm((DEPTH, SSM_GROUPS, SSM_GROUP, SSM_STATE), (2 * SSM_STATE) ** -0.5),
        'ssm_c_im': nrm((DEPTH, SSM_GROUPS, SSM_GROUP, SSM_STATE), (2 * SSM_STATE) ** -0.5),
        'ssm_d': nrm((DEPTH, BRANCH_WIDTH), 1.0),
        'ssm_w_glu': nrm((DEPTH, BRANCH_WIDTH, 2 * BRANCH_WIDTH), BRANCH_WIDTH ** -0.5),
        'ssm_b_glu': nrm((DEPTH, 2 * BRANCH_WIDTH), 0.01),
        'w_branch': nrm((DEPTH, N_BRANCH, BRANCH_WIDTH, D_MODEL), BRANCH_WIDTH ** -0.5),
        'w_out': nrm((DEPTH, D_MODEL, D_MODEL), D_MODEL ** -0.5),
        'norm2_g': 1.0 + nrm((DEPTH, D_MODEL), 0.01),
        'moe_w_group': nrm((DEPTH, D_MODEL, N_GROUPS), D_MODEL ** -0.5),
        'moe_b_group': nrm((DEPTH, N_GROUPS), 0.01),
        'moe_w_expert': nrm((DEPTH, D_MODEL, N_GROUPS * EXPERTS_PER_GROUP), D_MODEL ** -0.5),
        'moe_b_expert': nrm((DEPTH, N_GROUPS * EXPERTS_PER_GROUP), 0.01),
        'moe_w1': nrm((DEPTH, N_GROUPS, EXPERTS_PER_GROUP, D_MODEL, D_EXPERT), D_MODEL ** -0.5),
        'moe_w3': nrm((DEPTH, N_GROUPS, EXPERTS_PER_GROUP, D_MODEL, D_EXPERT), D_MODEL ** -0.5),
        'moe_w2': nrm((DEPTH, N_GROUPS, EXPERTS_PER_GROUP, D_EXPERT, D_MODEL), D_EXPERT ** -0.5),
        'final_norm_g': 1.0 + nrm((D_MODEL,), 0.01),
    }


def reference(x, norm1_g, w_in, attn_sink, lru_conv_w, lru_conv_b, lru_w_r, lru_b_r,
              lru_w_i, lru_b_i, lru_lambda, ssm_a_re, ssm_a_im, ssm_log_step, ssm_b_re,
              ssm_b_im, ssm_c_re, ssm_c_im, ssm_d, ssm_w_glu, ssm_b_glu, w_branch, w_out,
              norm2_g, moe_w_group, moe_b_group, moe_w_expert, moe_b_expert, moe_w1,
              moe_w3, moe_w2, final_norm_g):
    for i in range(DEPTH):
        hn = rms_norm(x, norm1_g[i])
        x = x + hybrid_mixer(hn, w_in[i], attn_sink[i], lru_conv_w[i], lru_conv_b[i],
                             lru_w_r[i], lru_b_r[i], lru_w_i[i], lru_b_i[i], lru_lambda[i],
                             ssm_a_re[i], ssm_a_im[i], ssm_log_step[i], ssm_b_re[i], ssm_b_im[i],
                             ssm_c_re[i], ssm_c_im[i], ssm_d[i], ssm_w_glu[i], ssm_b_glu[i],
                             w_branch[i], w_out[i]).astype(x.dtype)
        hn = rms_norm(x, norm2_g[i])
        x = x + hier_moe(hn, moe_w_group[i], moe_b_group[i], moe_w_expert[i], moe_b_expert[i],
                         moe_w1[i], moe_w3[i], moe_w2[i])
    return rms_norm(x, final_norm_g)
```

```python
import functools
import math

import jax
import jax.numpy as jnp
import numpy as np
from jax import lax
from jax.experimental import pallas as pl
from jax.experimental.pallas import tpu as pltpu

F32 = jnp.float32
BF16 = jnp.bfloat16

D_MODEL = 1024
N_Q_HEADS = 8
N_KV_HEADS = 2
HEAD_DIM = 64
WINDOW = 128
BLOCK = 128
BRANCH_WIDTH = 512
KV_WIDTH = N_KV_HEADS * HEAD_DIM
LRU_BLOCKS = 8
LRU_BLOCK_DIM = BRANCH_WIDTH // LRU_BLOCKS
LRU_C = 8.0
SSM_GROUP = 16
SSM_GROUPS = BRANCH_WIDTH // SSM_GROUP
SSM_STATE = 64
N_BRANCH = 3
N_GROUPS = 4
EXPERTS_PER_GROUP = 4
N_EXPERTS = N_GROUPS * EXPERTS_PER_GROUP
D_EXPERT = 256
NORM_EPS = 1e-6
MASK_VALUE = -1e30

QKV_WIDTH = BRANCH_WIDTH + 2 * KV_WIDTH
MAIN_COLS = QKV_WIDTH + 3 * BRANCH_WIDTH
SSM_CHUNK = 16
SSM_PAIRS = SSM_GROUPS // 2
PAIR_WIDTH = 2 * SSM_CHUNK * SSM_GROUP
LANES = 128
ROUTE_COLS = LANES
VMEM_LIMIT = 56 * 1024 * 1024


def _rms(x, g):
    ms = jnp.mean(x * x, axis=-1, keepdims=True)
    return x * lax.rsqrt(ms + NORM_EPS) * g


def _dot(a, b):
    return jnp.dot(a, b, preferred_element_type=F32)


def _dot_nt(a, b):
    return lax.dot_general(a, b, (((1,), (1,)), ((), ())), preferred_element_type=F32)


def _const_spec(shape):
    nd = len(shape)
    return pl.BlockSpec(shape, lambda *_: (0,) * nd)


def _inproj_kernel(x_ref, g_ref, w_ref, qkv_ref, xr_ref, gr_ref, u_ref):
    hn = _rms(x_ref[...], g_ref[...]).astype(BF16)
    bw = BRANCH_WIDTH
    qkv_ref[...] = _dot(hn, w_ref[:, 0:QKV_WIDTH]).astype(BF16)
    xr_ref[...] = _dot(hn, w_ref[:, QKV_WIDTH:QKV_WIDTH + bw]).astype(BF16)
    gr_ref[...] = _dot(hn, w_ref[:, QKV_WIDTH + bw:QKV_WIDTH + 2 * bw]).astype(BF16)
    u_ref[...] = _dot(hn, w_ref[:, QKV_WIDTH + 2 * bw:MAIN_COLS]).astype(BF16)


def _inproj(x2, g, w_main, tm):
    t = x2.shape[0]
    row = lambda w: pl.BlockSpec((tm, w), lambda i: (i, 0))
    return pl.pallas_call(
        _inproj_kernel,
        grid=(t // tm,),
        in_specs=[row(D_MODEL), _const_spec((1, D_MODEL)), _const_spec((D_MODEL, MAIN_COLS))],
        out_specs=[row(QKV_WIDTH), row(BRANCH_WIDTH), row(BRANCH_WIDTH), row(BRANCH_WIDTH)],
        out_shape=[jax.ShapeDtypeStruct((t, QKV_WIDTH), BF16)]
        + [jax.ShapeDtypeStruct((t, BRANCH_WIDTH), BF16)] * 3,
        compiler_params=pltpu.CompilerParams(
            dimension_semantics=("parallel",), vmem_limit_bytes=VMEM_LIMIT),
        name="inproj",
    )(x2, g, w_main)


def _attn_kernel(sink_ref, q_ref, kvp_ref, kvc_ref, kvn_ref, o_ref, *, nb):
    n = pl.program_id(1)
    kv = jnp.concatenate([kvp_ref[...], kvc_ref[...], kvn_ref[...]], axis=0).astype(F32)
    k = kv[:, :KV_WIDTH]
    v = kv[:, KV_WIDTH:]
    lane = lax.broadcasted_iota(jnp.int32, (1, KV_WIDTH), 1)
    lo = lane < HEAD_DIM

    qi = lax.broadcasted_iota(jnp.int32, (BLOCK, 3 * BLOCK), 0)
    kj = lax.broadcasted_iota(jnp.int32, (BLOCK, 3 * BLOCK), 1)
    dist = jnp.abs(qi - kj + BLOCK)
    valid = dist <= WINDOW
    valid = valid & ((kj >= BLOCK) | (n > 0)) & ((kj < 2 * BLOCK) | (n < nb - 1))
    distf = dist.astype(F32)

    k2, vlo, vhi = [], [], []
    for h in range(N_KV_HEADS):
        own = lo if h == 0 else jnp.logical_not(lo)
        km = jnp.where(own, k, 0.0)
        vm = jnp.where(own, v, 0.0)
        k2.append((km + pltpu.roll(km, HEAD_DIM, 1)).astype(BF16))
        vr = pltpu.roll(vm, HEAD_DIM, 1)
        vlo.append((vm if h == 0 else vr).astype(BF16))
        vhi.append((vr if h == 0 else vm).astype(BF16))

    scale = HEAD_DIM ** -0.5
    for pair in range(N_Q_HEADS // 2):
        h = pair // (Q_PER_KV_PAIRS)
        qp = q_ref[:, pair * KV_WIDTH:(pair + 1) * KV_WIDTH].astype(F32) * scale
        acc = None
        for half in range(2):
            head = 2 * pair + half
            own = lo if half == 0 else jnp.logical_not(lo)
            qh = jnp.where(own, qp, 0.0).astype(BF16)
            s = _dot_nt(qh, k2[h])
            slope = 2.0 ** (-8.0 * (head + 1) / N_Q_HEADS)
            s = jnp.where(valid, s - slope * distf, MASK_VALUE)
            sink = sink_ref[head]
            m = jnp.maximum(jnp.max(s, axis=-1, keepdims=True), sink)
            p = jnp.exp(s - m)
            denom = jnp.sum(p, axis=-1, keepdims=True) + jnp.exp(sink - m)
            r = _dot(p.astype(BF16), vlo[h] if half == 0 else vhi[h])
            r = r / denom
            acc = r if acc is None else acc + r
        o_ref[:, pair * KV_WIDTH:(pair + 1) * KV_WIDTH] = acc.astype(BF16)


Q_PER_KV_PAIRS = (N_Q_HEADS // N_KV_HEADS) // 2


def _attention(qkv3, sink):
    b, l, _ = qkv3.shape
    nb = l // BLOCK
    kv_col = BRANCH_WIDTH // (2 * KV_WIDTH)
    kv_spec = lambda f: pl.BlockSpec((None, BLOCK, 2 * KV_WIDTH), lambda bi, n: (bi, f(n), kv_col))
    return pl.pallas_call(
        functools.partial(_attn_kernel, nb=nb),
        grid=(b, nb),
        in_specs=[
            pl.BlockSpec(memory_space=pltpu.SMEM),
            pl.BlockSpec((None, BLOCK, BRANCH_WIDTH), lambda bi, n: (bi, n, 0)),
            kv_spec(lambda n: jnp.maximum(n - 1, 0)),
            kv_spec(lambda n: n),
            kv_spec(lambda n: jnp.minimum(n + 1, nb - 1)),
        ],
        out_specs=pl.BlockSpec((None, BLOCK, BRANCH_WIDTH), lambda bi, n: (bi, n, 0)),
        out_shape=jax.ShapeDtypeStruct((b, l, BRANCH_WIDTH), BF16),
        compiler_params=pltpu.CompilerParams(
            dimension_semantics=("parallel", "parallel"), vmem_limit_bytes=VMEM_LIMIT),
        name="window_attn",
    )(sink, qkv3, qkv3, qkv3, qkv3)


def _lru_kernel(*refs, tc, nch, reverse):
    if reverse:
        x_ref, xp_ref, xn_ref, cw_ref, cb_ref, wg_ref, bg_ref, sp_ref, hf_ref, o_ref, a_s, u_s, h_s = refs
    else:
        x_ref, xp_ref, xn_ref, cw_ref, cb_ref, wg_ref, bg_ref, sp_ref, o_ref, a_s, u_s, h_s = refs
        hf_ref = None
    c = pl.program_id(0)
    blk = nch - 1 - c if reverse else c
    bsz = x_ref.shape[1]

    @pl.when(c == 0)
    def _():
        h_s[...] = jnp.zeros_like(h_s)

    xm = x_ref[...].astype(F32)
    xp = xp_ref[...].astype(F32) * (blk > 0).astype(F32)
    xn = xn_ref[...].astype(F32) * (blk < nch - 1).astype(F32)
    xe = jnp.concatenate([xp, xm, xn], axis=0)
    cw = cw_ref[...]
    xc = cb_ref[...][None]
    for tap in range(4):
        xc = xc + cw[tap:tap + 1][None] * xe[tap:tap + tc]
    xc2 = xc.reshape(tc * bsz, BRANCH_WIDTH)
    gates = jax.nn.sigmoid(_dot(xc2.astype(BF16), wg_ref[...]) + bg_ref[...])
    r = gates[:, :BRANCH_WIDTH]
    i = gates[:, BRANCH_WIDTH:]
    log_a = -LRU_C * r * sp_ref[...]
    a = jnp.exp(log_a)
    u = jnp.sqrt(1.0 - a * a) * (i * xc2)
    a_s[...] = a.reshape(tc, bsz, BRANCH_WIDTH)
    u_s[...] = u.reshape(tc, bsz, BRANCH_WIDTH)

    def step(k, h):
        t = tc - 1 - k if reverse else k
        h = a_s[t] * h + u_s[t]
        if reverse:
            o_ref[t] = (h + hf_ref[t]).astype(o_ref.dtype)
        else:
            o_ref[t] = h
        return h

    h_s[...] = lax.fori_loop(0, tc, step, h_s[...], unroll=8)


def _lru_dir(xr_tm, hf, conv_w, conv_b, wg, bg, sp, tc, reverse):
    l, b, w = xr_tm.shape
    nch = l // tc
    order = (lambda c: nch - 1 - c) if reverse else (lambda c: c)
    main = pl.BlockSpec((tc, b, w), lambda c: (order(c), 0, 0))
    prev = pl.BlockSpec((2, b, w), lambda c: (jnp.maximum(order(c) * (tc // 2) - 1, 0), 0, 0))
    nxt = pl.BlockSpec((1, b, w), lambda c: (jnp.minimum((order(c) + 1) * tc, l - 1), 0, 0))
    in_specs = [main, prev, nxt, _const_spec((4, w)), _const_spec((1, w)),
                _const_spec((w, 2 * w)), _const_spec((1, 2 * w)), _const_spec((1, w))]
    args = [xr_tm, xr_tm, xr_tm, conv_w, conv_b, wg, bg, sp]
    if reverse:
        in_specs.append(main)
        args.append(hf)
    return pl.pallas_call(
        functools.partial(_lru_kernel, tc=tc, nch=nch, reverse=reverse),
        grid=(nch,),
        in_specs=in_specs,
        out_specs=main,
        out_shape=jax.ShapeDtypeStruct((l, b, w), BF16 if reverse else F32),
        scratch_shapes=[pltpu.VMEM((tc, b, w), F32), pltpu.VMEM((tc, b, w), F32),
                        pltpu.VMEM((b, w), F32)],
        compiler_params=pltpu.CompilerParams(
            dimension_semantics=("arbitrary",), vmem_limit_bytes=VMEM_LIMIT),
        name="lru_bwd" if reverse else "lru_fwd",
    )(*args)


def _block_diag(wb):
    nbk, bd, _ = wb.shape
    eye = jnp.eye(nbk, dtype=wb.dtype)
    return jnp.einsum('hij,hk->hikj', wb, eye).reshape(nbk * bd, nbk * bd)


def _s5_kernel(u_ref, tk_ref, wsin_ref, mout_ref, lamq_ref, d_ref, y_ref, s_s, h_s, *, nch):
    bsz = s_s.shape[1]
    half = PAIR_WIDTH // 2
    x = u_ref[...]
    s_s[...] = _dot(x, wsin_ref[...]).reshape(nch, bsz, PAIR_WIDTH)
    lam = lamq_ref[...]
    lfr, lfi, lbr, lbi = [jnp.broadcast_to(lam[j:j + 1], (bsz, LANES)) for j in range(4)]

    def step(i, carry):
        hfr, hfi, hbr, hbi = carry
        jb = nch - 1 - i
        h_s[i, :, 0:LANES] = hfr
        h_s[i, :, LANES:2 * LANES] = hfi
        h_s[jb, :, 2 * LANES:3 * LANES] = hbr
        h_s[jb, :, 3 * LANES:4 * LANES] = hbi
        sfr = s_s[i, :, 0:LANES]
        sfi = s_s[i, :, LANES:2 * LANES]
        sbr = s_s[jb, :, 2 * LANES:3 * LANES]
        sbi = s_s[jb, :, 3 * LANES:4 * LANES]
        return (lfr * hfr - lfi * hfi + sfr, lfr * hfi + lfi * hfr + sfi,
                lbr * hbr - lbi * hbi + sbr, lbr * hbi + lbi * hbr + sbi)

    z = jnp.zeros((bsz, LANES), F32)
    lax.fori_loop(0, nch, step, (z, z, z, z), unroll=4)

    hcat = h_s[...].reshape(nch * bsz, PAIR_WIDTH).astype(BF16)
    y_inter = _dot(hcat, mout_ref[...])
    y0 = _dot(x[:, :half], tk_ref[0])
    y1 = _dot(x[:, half:], tk_ref[1])
    y = jnp.concatenate([y0, y1], axis=1) + y_inter + d_ref[...] * x.astype(F32)
    y_ref[...] = y.astype(y_ref.dtype)


def _s5(u_flat, tk, wsin, mout, lamq, dflat, nch, bsz):
    rows = nch * bsz
    pw = PAIR_WIDTH
    return pl.pallas_call(
        functools.partial(_s5_kernel, nch=nch),
        grid=(SSM_PAIRS,),
        in_specs=[
            pl.BlockSpec((None, rows, pw), lambda p: (p, 0, 0)),
            pl.BlockSpec((None, 2, pw // 2, pw // 2), lambda p: (p, 0, 0, 0)),
            pl.BlockSpec((None, pw, pw), lambda p: (p, 0, 0)),
            pl.BlockSpec((None, pw, pw), lambda p: (p, 0, 0)),
            pl.BlockSpec((None, 4, LANES), lambda p: (p, 0, 0)),
            pl.BlockSpec((None, 1, pw), lambda p: (p, 0, 0)),
        ],
        out_specs=pl.BlockSpec((None, rows, pw), lambda p: (p, 0, 0)),
        out_shape=jax.ShapeDtypeStruct((SSM_PAIRS, rows, pw), BF16),
        scratch_shapes=[pltpu.VMEM((nch, bsz, pw), F32), pltpu.VMEM((nch, bsz, pw), F32)],
        compiler_params=pltpu.CompilerParams(
            dimension_semantics=("parallel",), vmem_limit_bytes=VMEM_LIMIT),
        name="s5_chunked",
    )(u_flat, tk, wsin, mout, lamq, dflat)


def _s5_params(a_re, a_im, log_step, b_re, b_im, c_re, c_im, d_skip):
    q = SSM_CHUNK
    g, n, c = SSM_GROUPS, SSM_STATE, SSM_GROUP
    step = jnp.exp(log_step)[..., None]
    dr = a_re * step
    di = a_im * step
    mag = jnp.exp(dr)
    abar_r = mag * jnp.cos(di)
    abar_i = mag * jnp.sin(di)
    den = a_re * a_re + a_im * a_im
    nr = abar_r - 1.0
    sr = (nr * a_re + abar_i * a_im) / den
    si = (abar_i * a_re - nr * a_im) / den
    bb_r = sr[..., None] * b_re - si[..., None] * b_im
    bb_i = sr[..., None] * b_im + si[..., None] * b_re
    kk = jnp.arange(q + 1, dtype=F32)
    pmag = jnp.exp(dr[..., None] * kk)
    pr = pmag * jnp.cos(di[..., None] * kk)
    pi = pmag * jnp.sin(di[..., None] * kk)
    cr = c_re[None, :, :, :, None]
    ci = c_im[None, :, :, :, None]
    cpr = cr * pr[:, :, None] - ci * pi[:, :, None]
    cpi = cr * pi[:, :, None] + ci * pr[:, :, None]
    kern = (jnp.einsum('dgcnk,dgne->dgkce', cpr, bb_r)
            - jnp.einsum('dgcnk,dgne->dgkce', cpi, bb_i))[:, :, :q]
    s_idx = jnp.arange(q)[:, None]
    r_idx = jnp.arange(q)[None, :]
    lag = r_idx - s_idx
    tk_f = jnp.where((lag >= 0)[None, :, :, None, None],
                     kern[0][:, jnp.clip(lag, 0, q - 1)], 0.0)
    tk_b = jnp.where((lag <= 0)[None, :, :, None, None],
                     kern[1][:, jnp.clip(-lag, 0, q - 1)], 0.0)
    tk = (tk_f + tk_b).transpose(0, 1, 4, 2, 3).reshape(g, q * c, q * c)
    tk = tk.reshape(SSM_PAIRS, 2, q * c, q * c)

    def state_in(d, powers):
        wr = pr[d][:, :, powers][..., None] * bb_r[d][:, :, None, :] \
            - pi[d][:, :, powers][..., None] * bb_i[d][:, :, None, :]
        wi = pr[d][:, :, powers][..., None] * bb_i[d][:, :, None, :] \
            + pi[d][:, :, powers][..., None] * bb_r[d][:, :, None, :]
        lay = lambda w_: w_.transpose(0, 2, 3, 1).reshape(g, q * c, n)
        return lay(wr), lay(wi)

    fr, fi = state_in(0, q - 1 - jnp.arange(q))
    br, bi = state_in(1, jnp.arange(q))

    def pair_rows(w_):
        w_ = w_.reshape(SSM_PAIRS, 2, q * c, n)
        z = jnp.zeros_like(w_[:, 0])
        top = jnp.concatenate([w_[:, 0], z], axis=-1)
        bot = jnp.concatenate([z, w_[:, 1]], axis=-1)
        return jnp.concatenate([top, bot], axis=1)

    wsin = jnp.concatenate([pair_rows(fr), pair_rows(fi), pair_rows(br), pair_rows(bi)], axis=-1)

    def read_out(d, powers):
        sel_r = cpr[d][..., powers]
        sel_i = cpi[d][..., powers]
        lay = lambda w_: w_.transpose(0, 2, 3, 1).reshape(g, n, q * c)
        return lay(sel_r), lay(-sel_i)

    ofr, ofi = read_out(0, jnp.arange(q) + 1)
    obr, obi = read_out(1, q - jnp.arange(q))

    def pair_cols(w_):
        w_ = w_.reshape(SSM_PAIRS, 2, n, q * c)
        z = jnp.zeros_like(w_[:, 0])
        top = jnp.concatenate([w_[:, 0], z], axis=-1)
        bot = jnp.concatenate([z, w_[:, 1]], axis=-1)
        return jnp.concatenate([top, bot], axis=1)

    mout = jnp.concatenate([pair_cols(ofr), pair_cols(ofi), pair_cols(obr), pair_cols(obi)], axis=1)
    lamq = jnp.stack([pr[0][..., q], pi[0][..., q], pr[1][..., q], pi[1][..., q]], axis=1)
    lamq = lamq.reshape(SSM_PAIRS, 2, 4, n).transpose(0, 2, 1, 3).reshape(SSM_PAIRS, 4, 2 * n)
    dflat = jnp.broadcast_to(d_skip.reshape(g, 1, c), (g, q, c)).reshape(SSM_PAIRS, 1, PAIR_WIDTH)
    return tk.astype(BF16), wsin.astype(BF16), mout.astype(BF16), lamq, dflat


def _merge_kernel(x_ref, g_ref, oa_ref, hs_ref, gr_ref, ys_ref, wgate_ref, wb_ref, wglu_ref,
                  bglu_ref, wout_ref, o_ref):
    x = x_ref[...]
    hn = _rms(x, g_ref[...]).astype(BF16)
    bw = BRANCH_WIDTH
    o_lru = (jax.nn.gelu(gr_ref[...].astype(F32)) * hs_ref[...].astype(F32)).astype(BF16)
    zg = _dot(jax.nn.gelu(ys_ref[...].astype(F32)).astype(BF16), wglu_ref[...]) + bglu_ref[...]
    o_ssm = (zg[:, :bw] * jax.nn.sigmoid(zg[:, bw:])).astype(BF16)
    mixed = None
    for kb, ob in enumerate((oa_ref[...], o_lru, o_ssm)):
        gate = jax.nn.sigmoid(_dot(hn, wgate_ref[:, kb * D_MODEL:(kb + 1) * D_MODEL]))
        term = _dot(ob, wb_ref[kb]) * gate
        mixed = term if mixed is None else mixed + term
    o_ref[...] = x + _dot(mixed.astype(BF16), wout_ref[...])


def _merge(x2, g, o_attn, h_sum, gr, y_ssm, w_gate, w_branch, w_glu, b_glu, w_out, tm):
    t = x2.shape[0]
    row = lambda w: pl.BlockSpec((tm, w), lambda i: (i, 0))
    return pl.pallas_call(
        _merge_kernel,
        grid=(t // tm,),
        in_specs=[row(D_MODEL), _const_spec((1, D_MODEL)), row(BRANCH_WIDTH), row(BRANCH_WIDTH),
                  row(BRANCH_WIDTH), row(BRANCH_WIDTH),
                  _const_spec((D_MODEL, N_BRANCH * D_MODEL)),
                  _const_spec((N_BRANCH, BRANCH_WIDTH, D_MODEL)),
                  _const_spec((BRANCH_WIDTH, 2 * BRANCH_WIDTH)),
                  _const_spec((1, 2 * BRANCH_WIDTH)),
                  _const_spec((D_MODEL, D_MODEL))],
        out_specs=row(D_MODEL),
        out_shape=jax.ShapeDtypeStruct((t, D_MODEL), F32),
        compiler_params=pltpu.CompilerParams(
            dimension_semantics=("parallel",), vmem_limit_bytes=VMEM_LIMIT),
        name="merge",
    )(x2, g, o_attn, h_sum, gr, y_ssm, w_gate, w_branch, w_glu, b_glu, w_out)


def _route(logits):
    lane = lax.broadcasted_iota(jnp.int32, logits.shape, 1)
    neg = -jnp.inf
    big = jnp.int32(ROUTE_COLS)
    first = lambda mask: jnp.min(jnp.where(mask, lane, big), axis=-1, keepdims=True)
    gmask = lane < N_GROUPS
    gmax = jnp.max(jnp.where(gmask, logits, neg), axis=-1, keepdims=True)
    gidx = first(gmask & (logits == gmax))
    gsum = jnp.sum(jnp.where(gmask, jnp.exp(logits - gmax), 0.0), axis=-1, keepdims=True)
    g_w = 1.0 / gsum
    e_lo = N_GROUPS + EXPERTS_PER_GROUP * gidx
    emask = (lane >= e_lo) & (lane < e_lo + EXPERTS_PER_GROUP)
    v1 = jnp.max(jnp.where(emask, logits, neg), axis=-1, keepdims=True)
    i1 = first(emask & (logits == v1))
    emask2 = emask & (lane != i1)
    v2 = jnp.max(jnp.where(emask2, logits, neg), axis=-1, keepdims=True)
    i2 = first(emask2 & (logits == v2))
    t = jnp.exp(v2 - v1)
    w1 = 1.0 / (1.0 + t)
    w2 = t * w1
    return jnp.where(lane == i1, w1 * g_w, jnp.where(lane == i2, w2 * g_w, 0.0))


def _moe_kernel(x_ref, g_ref, wr_ref, br_ref, w13_ref, w2_ref, fg_ref, o_ref, hn_s, cw_s, acc_s,
                *, final_norm):
    e = pl.program_id(1)

    @pl.when(e == 0)
    def _():
        hn = _rms(x_ref[...], g_ref[...])
        hn_s[...] = hn.astype(BF16)
        logits = jnp.dot(hn, wr_ref[...], preferred_element_type=F32,
                         precision=lax.Precision.HIGHEST) + br_ref[...]
        cw_s[...] = _route(logits)
        acc_s[...] = jnp.zeros_like(acc_s)

    h = _dot(hn_s[...], w13_ref[...])
    lane = lax.broadcasted_iota(jnp.int32, cw_s.shape, 1)
    wcol = jnp.sum(jnp.where(lane == e + N_GROUPS, cw_s[...], 0.0), axis=-1, keepdims=True)
    hg = jax.nn.silu(h[:, :D_EXPERT]) * h[:, D_EXPERT:] * wcol
    acc_s[...] += _dot(hg.astype(BF16), w2_ref[...])

    @pl.when(e == N_EXPERTS - 1)
    def _():
        y = x_ref[...] + acc_s[...]
        if final_norm:
            y = _rms(y, fg_ref[...])
        o_ref[...] = y


def _moe(x2, g, w_route, b_route, w13, w2, final_g, tm, final_norm):
    t = x2.shape[0]
    return pl.pallas_call(
        functools.partial(_moe_kernel, final_norm=final_norm),
        grid=(t // tm, N_EXPERTS),
        in_specs=[pl.BlockSpec((tm, D_MODEL), lambda i, e: (i, 0)),
                  _const_spec((1, D_MODEL)),
                  _const_spec((D_MODEL, ROUTE_COLS)),
                  _const_spec((1, ROUTE_COLS)),
                  pl.BlockSpec((None, D_MODEL, 2 * D_EXPERT), lambda i, e: (e, 0, 0)),
                  pl.BlockSpec((None, D_EXPERT, D_MODEL), lambda i, e: (e, 0, 0)),
                  _const_spec((1, D_MODEL))],
        out_specs=pl.BlockSpec((tm, D_MODEL), lambda i, e: (i, 0)),
        out_shape=jax.ShapeDtypeStruct((t, D_MODEL), F32),
        scratch_shapes=[pltpu.VMEM((tm, D_MODEL), BF16), pltpu.VMEM((tm, ROUTE_COLS), F32),
                        pltpu.VMEM((tm, D_MODEL), F32)],
        compiler_params=pltpu.CompilerParams(
            dimension_semantics=("parallel", "arbitrary"), vmem_limit_bytes=VMEM_LIMIT),
        name="moe_final" if final_norm else "moe",
    )(x2, g, w_route, b_route, w13, w2, final_g)


def _pick_tile(n, want):
    t = min(n, want)
    while n % t:
        t //= 2
    return t


def _layer(x2, bsz, seq, p, final_g, final_norm):
    t = bsz * seq
    bw = BRANCH_WIDTH
    q = SSM_CHUNK
    nch = seq // q
    tm = _pick_tile(t, 512)

    w_in = p['w_in']
    qkv, xr, gr, u = _inproj(x2, p['norm1_g'][None], w_in[:, :MAIN_COLS].astype(BF16), tm)

    o_attn = _attention(qkv.reshape(bsz, seq, QKV_WIDTH), p['attn_sink']).reshape(t, bw)

    xr_tm = xr.reshape(bsz, seq, bw).transpose(1, 0, 2)
    sp = jax.nn.softplus(-p['lru_lambda'])
    tc = _pick_tile(seq, 64)
    h = None
    for d in range(2):
        wg = jnp.concatenate([_block_diag(p['lru_w_r'][d]), _block_diag(p['lru_w_i'][d])], axis=1)
        bg = jnp.concatenate([p['lru_b_r'][d], p['lru_b_i'][d]])[None]
        h = _lru_dir(xr_tm, h, p['lru_conv_w'], p['lru_conv_b'][None], wg.astype(BF16), bg,
                     sp[d][None], tc, reverse=(d == 1))
    h_sum = h.transpose(1, 0, 2).reshape(t, bw)

    tk, wsin, mout, lamq, dflat = _s5_params(
        p['ssm_a_re'], p['ssm_a_im'], p['ssm_log_step'], p['ssm_b_re'], p['ssm_b_im'],
        p['ssm_c_re'], p['ssm_c_im'], p['ssm_d'])
    u_flat = (u.reshape(bsz, nch, q, SSM_PAIRS, 2, SSM_GROUP)
              .transpose(3, 1, 0, 4, 2, 5).reshape(SSM_PAIRS, nch * bsz, PAIR_WIDTH))
    y_flat = _s5(u_flat, tk, wsin, mout, lamq, dflat, nch, bsz)
    y_ssm = (y_flat.reshape(SSM_PAIRS, nch, bsz, 2, q, SSM_GROUP)
             .transpose(2, 1, 4, 0, 3, 5).reshape(t, bw))

    x2 = _merge(x2, p['norm1_g'][None], o_attn, h_sum, gr, y_ssm,
                w_in[:, MAIN_COLS:].astype(BF16), p['w_branch'].astype(BF16),
                p['ssm_w_glu'].astype(BF16), p['ssm_b_glu'][None], p['w_out'].astype(BF16), tm)

    pad = ROUTE_COLS - N_GROUPS - N_EXPERTS
    w_route = jnp.concatenate(
        [p['moe_w_group'], p['moe_w_expert'], jnp.zeros((D_MODEL, pad), F32)], axis=1)
    b_route = jnp.concatenate(
        [p['moe_b_group'], p['moe_b_expert'], jnp.zeros((pad,), F32)])[None]
    w13 = jnp.concatenate([p['moe_w1'], p['moe_w3']], axis=-1).reshape(
        N_EXPERTS, D_MODEL, 2 * D_EXPERT).astype(BF16)
    w2 = p['moe_w2'].reshape(N_EXPERTS, D_EXPERT, D_MODEL).astype(BF16)
    return _moe(x2, p['norm2_g'][None], w_route, b_route, w13, w2, final_g[None],
                _pick_tile(t, 1024), final_norm)


_LAYER_KEYS = ('norm1_g', 'w_in', 'attn_sink', 'lru_conv_w', 'lru_conv_b', 'lru_w_r', 'lru_b_r',
               'lru_w_i', 'lru_b_i', 'lru_lambda', 'ssm_a_re', 'ssm_a_im', 'ssm_log_step',
               'ssm_b_re', 'ssm_b_im', 'ssm_c_re', 'ssm_c_im', 'ssm_d', 'ssm_w_glu', 'ssm_b_glu',
               'w_branch', 'w_out', 'norm2_g', 'moe_w_group', 'moe_b_group', 'moe_w_expert',
               'moe_b_expert', 'moe_w1', 'moe_w3', 'moe_w2')


def kernel(x, norm1_g, w_in, attn_sink, lru_conv_w, lru_conv_b, lru_w_r, lru_b_r, lru_w_i, lru_b_i,
           lru_lambda, ssm_a_re, ssm_a_im, ssm_log_step, ssm_b_re, ssm_b_im, ssm_c_re, ssm_c_im,
           ssm_d, ssm_w_glu, ssm_b_glu, w_branch, w_out, norm2_g, moe_w_group, moe_b_group,
           moe_w_expert, moe_b_expert, moe_w1, moe_w3, moe_w2, final_norm_g):
    stacked = dict(zip(_LAYER_KEYS, (
        norm1_g, w_in, attn_sink, lru_conv_w, lru_conv_b, lru_w_r, lru_b_r, lru_w_i, lru_b_i,
        lru_lambda, ssm_a_re, ssm_a_im, ssm_log_step, ssm_b_re, ssm_b_im, ssm_c_re, ssm_c_im,
        ssm_d, ssm_w_glu, ssm_b_glu, w_branch, w_out, norm2_g, moe_w_group, moe_b_group,
        moe_w_expert, moe_b_expert, moe_w1, moe_w3, moe_w2)))
    bsz, seq, _ = x.shape
    depth = norm1_g.shape[0]
    x2 = x.reshape(bsz * seq, D_MODEL)
    for i in range(depth):
        p = {k: v[i] for k, v in stacked.items()}
        x2 = _layer(x2, bsz, seq, p, final_norm_g, final_norm=(i == depth - 1))
    return x2.reshape(bsz, seq, D_MODEL)
```

```python
import functools

import jax
import jax.numpy as jnp
from jax import lax
from jax.experimental import pallas as pl
from jax.experimental.pallas import tpu as pltpu

F32 = jnp.float32
BF16 = jnp.bfloat16

D_MODEL = 1024
N_Q_HEADS = 8
N_KV_HEADS = 2
HEAD_DIM = 64
WINDOW = 128
BLOCK = 128
BRANCH_WIDTH = 512
KV_WIDTH = N_KV_HEADS * HEAD_DIM
LRU_BLOCKS = 8
LRU_C = 8.0
SSM_GROUP = 16
SSM_GROUPS = BRANCH_WIDTH // SSM_GROUP
SSM_STATE = 64
N_BRANCH = 3
N_GROUPS = 4
EXPERTS_PER_GROUP = 4
N_EXPERTS = N_GROUPS * EXPERTS_PER_GROUP
D_EXPERT = 256
NORM_EPS = 1e-6
MASK_VALUE = -1e30

LANES = 128
SUBLANES = 8
QKV_WIDTH = BRANCH_WIDTH + 2 * KV_WIDTH
MAIN_COLS = QKV_WIDTH + 3 * BRANCH_WIDTH
Q_PER_KV_PAIRS = (N_Q_HEADS // N_KV_HEADS) // 2
SSM_CHUNK = 8
SSM_OCTETS = BRANCH_WIDTH // LANES
OCT_GROUPS = LANES // SSM_GROUP
OCT_IN = SSM_CHUNK * LANES
OCT_STATE = OCT_GROUPS * SSM_STATE
N_COMP = 4
SCAN_STEPS = (1, 2, 4)
ROUTE_COLS = LANES
CONV_HALO = 16
VMEM_LIMIT = 56 * 1024 * 1024


def _rms(x, g):
    ms = jnp.mean(x * x, axis=-1, keepdims=True)
    return x * lax.rsqrt(ms + NORM_EPS) * g


def _dot(a, b):
    return jnp.dot(a, b, preferred_element_type=F32)


def _dot_nt(a, b):
    return lax.dot_general(a, b, (((1,), (1,)), ((), ())), preferred_element_type=F32)


def _const_spec(shape):
    nd = len(shape)
    return pl.BlockSpec(shape, lambda *_: (0,) * nd)


def _row_in_tile(rows):
    return lax.broadcasted_iota(jnp.int32, (rows, 1), 0) & (SUBLANES - 1)


def _inproj_kernel(x_ref, g_ref, w_ref, qkv_ref, xr_ref, gr_ref, u_ref):
    hn = _rms(x_ref[...], g_ref[...]).astype(BF16)
    bw = BRANCH_WIDTH
    qkv_ref[...] = _dot(hn, w_ref[:, 0:QKV_WIDTH]).astype(BF16)
    xr_ref[...] = _dot(hn, w_ref[:, QKV_WIDTH:QKV_WIDTH + bw]).astype(BF16)
    gr_ref[...] = _dot(hn, w_ref[:, QKV_WIDTH + bw:QKV_WIDTH + 2 * bw]).astype(BF16)
    u_ref[...] = _dot(hn, w_ref[:, QKV_WIDTH + 2 * bw:MAIN_COLS])


def _inproj(x2, g, w_main, tm):
    t = x2.shape[0]
    row = lambda w: pl.BlockSpec((tm, w), lambda i: (i, 0))
    return pl.pallas_call(
        _inproj_kernel,
        grid=(t // tm,),
        in_specs=[row(D_MODEL), _const_spec((1, D_MODEL)), _const_spec((D_MODEL, MAIN_COLS))],
        out_specs=[row(QKV_WIDTH), row(BRANCH_WIDTH), row(BRANCH_WIDTH), row(BRANCH_WIDTH)],
        out_shape=[jax.ShapeDtypeStruct((t, QKV_WIDTH), BF16)]
        + [jax.ShapeDtypeStruct((t, BRANCH_WIDTH), BF16)] * 2
        + [jax.ShapeDtypeStruct((t, BRANCH_WIDTH), F32)],
        compiler_params=pltpu.CompilerParams(
            dimension_semantics=("parallel",), vmem_limit_bytes=VMEM_LIMIT),
        name="inproj",
    )(x2, g, w_main)


def _attn_kernel(sink_ref, q_ref, kvp_ref, kvc_ref, kvn_ref, o_ref, *, nb):
    n = pl.program_id(1)
    kv = jnp.concatenate([kvp_ref[...], kvc_ref[...], kvn_ref[...]], axis=0).astype(F32)
    k = kv[:, :KV_WIDTH]
    v = kv[:, KV_WIDTH:]
    lane = lax.broadcasted_iota(jnp.int32, (1, KV_WIDTH), 1)
    lo = lane < HEAD_DIM

    qi = lax.broadcasted_iota(jnp.int32, (BLOCK, 3 * BLOCK), 0)
    kj = lax.broadcasted_iota(jnp.int32, (BLOCK, 3 * BLOCK), 1)
    dist = jnp.abs(qi - kj + BLOCK)
    valid = dist <= WINDOW
    valid = valid & ((kj >= BLOCK) | (n > 0)) & ((kj < 2 * BLOCK) | (n < nb - 1))
    distf = dist.astype(F32)

    k2, vlo, vhi = [], [], []
    for h in range(N_KV_HEADS):
        own = lo if h == 0 else jnp.logical_not(lo)
        km = jnp.where(own, k, 0.0)
        vm = jnp.where(own, v, 0.0)
        k2.append((km + pltpu.roll(km, HEAD_DIM, 1)).astype(BF16))
        vr = pltpu.roll(vm, HEAD_DIM, 1)
        vlo.append((vm if h == 0 else vr).astype(BF16))
        vhi.append((vr if h == 0 else vm).astype(BF16))

    scale = HEAD_DIM ** -0.5
    for pair in range(N_Q_HEADS // 2):
        h = pair // Q_PER_KV_PAIRS
        qp = q_ref[:, pair * KV_WIDTH:(pair + 1) * KV_WIDTH].astype(F32) * scale
        acc = None
        for half in range(2):
            head = 2 * pair + half
            own = lo if half == 0 else jnp.logical_not(lo)
            qh = jnp.where(own, qp, 0.0).astype(BF16)
            s = _dot_nt(qh, k2[h])
            slope = 2.0 ** (-8.0 * (head + 1) / N_Q_HEADS)
            s = jnp.where(valid, s - slope * distf, MASK_VALUE)
            sink = sink_ref[head]
            m = jnp.maximum(jnp.max(s, axis=-1, keepdims=True), sink)
            p = jnp.exp(s - m)
            denom = jnp.sum(p, axis=-1, keepdims=True) + jnp.exp(sink - m)
            r = _dot(p.astype(BF16), vlo[h] if half == 0 else vhi[h])
            r = r / denom
            acc = r if acc is None else acc + r
        o_ref[:, pair * KV_WIDTH:(pair + 1) * KV_WIDTH] = acc.astype(BF16)


def _attention(qkv3, sink):
    b, l, _ = qkv3.shape
    nb = l // BLOCK
    kv_col = BRANCH_WIDTH // (2 * KV_WIDTH)
    kv_spec = lambda f: pl.BlockSpec((None, BLOCK, 2 * KV_WIDTH), lambda bi, n: (bi, f(n), kv_col))
    return pl.pallas_call(
        functools.partial(_attn_kernel, nb=nb),
        grid=(b, nb),
        in_specs=[
            pl.BlockSpec(memory_space=pltpu.SMEM),
            pl.BlockSpec((None, BLOCK, BRANCH_WIDTH), lambda bi, n: (bi, n, 0)),
            kv_spec(lambda n: jnp.maximum(n - 1, 0)),
            kv_spec(lambda n: n),
            kv_spec(lambda n: jnp.minimum(n + 1, nb - 1)),
        ],
        out_specs=pl.BlockSpec((None, BLOCK, BRANCH_WIDTH), lambda bi, n: (bi, n, 0)),
        out_shape=jax.ShapeDtypeStruct((b, l, BRANCH_WIDTH), BF16),
        compiler_params=pltpu.CompilerParams(
            dimension_semantics=("parallel", "parallel"), vmem_limit_bytes=VMEM_LIMIT),
        name="window_attn",
    )(sink, qkv3, qkv3, qkv3, qkv3)


def _tile_scan(a, u, reverse):
    rows = a.shape[0]
    rit = _row_in_tile(rows)
    for d in SCAN_STEPS:
        if reverse:
            take = rit < SUBLANES - d
            shift = rows - d
        else:
            take = rit >= d
            shift = d
        a_sh = jnp.where(take, pltpu.roll(a, shift, 0), 1.0)
        u_sh = jnp.where(take, pltpu.roll(u, shift, 0), 0.0)
        u = u + a * u_sh
        a = a * a_sh
    return a, u


def _lru_kernel(x_ref, cw_ref, cb_ref, wg_ref, bg_ref, sp_ref, o_ref, xe_s, a_s, u_s, hf_s,
                *, seq, rc):
    phase = pl.program_id(1)
    bw = BRANCH_WIDTH
    n_tiles = seq // SUBLANES

    zeros = jnp.zeros((CONV_HALO, bw), F32)
    xe_s[0:CONV_HALO, :] = zeros
    xe_s[CONV_HALO + seq:CONV_HALO + seq + CONV_HALO, :] = zeros
    xe_s[CONV_HALO:CONV_HALO + seq, :] = x_ref[...].astype(F32)
    cw = cw_ref[...]

    def build(reverse):
        for c in range(seq // rc):
            base = CONV_HALO + c * rc
            xc = cb_ref[...]
            for tap in range(4):
                xc = xc + cw[tap:tap + 1] * xe_s[base + tap - 2:base + tap - 2 + rc, :]
            gates = jax.nn.sigmoid(_dot(xc.astype(BF16), wg_ref[...]) + bg_ref[...])
            log_a = -LRU_C * gates[:, :bw] * sp_ref[...]
            a = jnp.exp(log_a)
            u = jnp.sqrt(1.0 - a * a) * (gates[:, bw:] * xc)
            a, u = _tile_scan(a, u, reverse)
            a_s[c * rc:(c + 1) * rc, :] = a
            u_s[c * rc:(c + 1) * rc, :] = u

    @pl.when(phase == 0)
    def _():
        build(False)

        def step(k, carry):
            r0 = pl.multiple_of(k * SUBLANES, SUBLANES)
            h = u_s[pl.ds(r0, SUBLANES), :] + a_s[pl.ds(r0, SUBLANES), :] * carry
            hf_s[pl.ds(r0, SUBLANES), :] = h
            return jnp.broadcast_to(h[SUBLANES - 1:SUBLANES, :], (SUBLANES, bw))

        lax.fori_loop(0, n_tiles, step, jnp.zeros((SUBLANES, bw), F32), unroll=4)

    @pl.when(phase == 1)
    def _():
        build(True)

        def step(k, carry):
            r0 = pl.multiple_of((n_tiles - 1 - k) * SUBLANES, SUBLANES)
            h = u_s[pl.ds(r0, SUBLANES), :] + a_s[pl.ds(r0, SUBLANES), :] * carry
            o_ref[pl.ds(r0, SUBLANES), :] = h + hf_s[pl.ds(r0, SUBLANES), :]
            return jnp.broadcast_to(h[0:1, :], (SUBLANES, bw))

        lax.fori_loop(0, n_tiles, step, jnp.zeros((SUBLANES, bw), F32), unroll=4)


def _lru(xr3, conv_w, conv_b, wg, bg, sp):
    b, l, w = xr3.shape
    seq_spec = pl.BlockSpec((None, l, w), lambda bi, ph: (bi, 0, 0))
    per_dir = lambda shape: pl.BlockSpec((None,) + shape, lambda bi, ph: (ph,) + (0,) * len(shape))
    return pl.pallas_call(
        functools.partial(_lru_kernel, seq=l, rc=_pick_tile(l, 512)),
        grid=(b, 2),
        in_specs=[seq_spec, _const_spec((4, w)), _const_spec((1, w)),
                  per_dir((w, 2 * w)), per_dir((1, 2 * w)), per_dir((1, w))],
        out_specs=seq_spec,
        out_shape=jax.ShapeDtypeStruct((b, l, w), F32),
        scratch_shapes=[pltpu.VMEM((l + 2 * CONV_HALO, w), F32), pltpu.VMEM((l, w), F32),
                        pltpu.VMEM((l, w), F32), pltpu.VMEM((l, w), F32)],
        compiler_params=pltpu.CompilerParams(
            dimension_semantics=("parallel", "arbitrary"), vmem_limit_bytes=VMEM_LIMIT),
        name="rg_lru",
    )(xr3, conv_w, conv_b, wg, bg, sp)


def _block_diag(wb):
    nbk, bd, _ = wb.shape
    eye = jnp.eye(nbk, dtype=wb.dtype)
    return jnp.einsum('hij,hk->hikj', wb, eye).reshape(nbk * bd, nbk * bd)


def _s5_kernel(u_ref, tk_ref, wsin_ref, mout_ref, lamd_ref, lamp_ref, d_ref, y_ref,
               x_s, g_s, *, nch, nbat):
    q = SSM_CHUNK
    rows = nbat * nch
    st = OCT_STATE
    n_tiles = nch // SUBLANES

    for s in range(q):
        x_s[:, s * LANES:(s + 1) * LANES] = u_ref[pl.ds(s, rows, stride=q), :].astype(BF16)
    x = x_s[...]

    sloc = _dot(x, wsin_ref[...])
    chunk = lax.broadcasted_iota(jnp.int32, (rows, 1), 0) & (nch - 1)
    rit = _row_in_tile(rows)
    comps = []
    for comp in range(N_COMP):
        reverse = comp >= 2
        piece = sloc[:, comp * st:(comp + 1) * st]
        if reverse:
            piece = jnp.where(chunk == nch - 1, 0.0, pltpu.roll(piece, rows - 1, 0))
        else:
            piece = jnp.where(chunk == 0, 0.0, pltpu.roll(piece, 1, 0))
        comps.append(piece)

    for di, d in enumerate(SCAN_STEPS):
        for base in (0, 2):
            reverse = base == 2
            lr = lamd_ref[di * N_COMP + base:di * N_COMP + base + 1, :]
            li = lamd_ref[di * N_COMP + base + 1:di * N_COMP + base + 2, :]
            if reverse:
                take = rit < SUBLANES - d
                shift = rows - d
            else:
                take = rit >= d
                shift = d
            gr, gi = comps[base], comps[base + 1]
            sr = jnp.where(take, pltpu.roll(gr, shift, 0), 0.0)
            si = jnp.where(take, pltpu.roll(gi, shift, 0), 0.0)
            comps[base] = gr + lr * sr - li * si
            comps[base + 1] = gi + lr * si + li * sr
    for comp in range(N_COMP):
        g_s[:, comp * st:(comp + 1) * st] = comps[comp]

    pf_r, pf_i = lamp_ref[0:SUBLANES, :], lamp_ref[SUBLANES:2 * SUBLANES, :]
    pb_r, pb_i = lamp_ref[2 * SUBLANES:3 * SUBLANES, :], lamp_ref[3 * SUBLANES:4 * SUBLANES, :]

    def step(k, carry):
        out = []
        for bi in range(nbat):
            cfr, cfi, cbr, cbi = carry[4 * bi:4 * bi + 4]
            rf = pl.multiple_of(bi * nch + k * SUBLANES, SUBLANES)
            rb = pl.multiple_of(bi * nch + (n_tiles - 1 - k) * SUBLANES, SUBLANES)
            gfr = g_s[pl.ds(rf, SUBLANES), 0:st] + pf_r * cfr - pf_i * cfi
            gfi = g_s[pl.ds(rf, SUBLANES), st:2 * st] + pf_r * cfi + pf_i * cfr
            gbr = g_s[pl.ds(rb, SUBLANES), 2 * st:3 * st] + pb_r * cbr - pb_i * cbi
            gbi = g_s[pl.ds(rb, SUBLANES), 3 * st:4 * st] + pb_r * cbi + pb_i * cbr
            g_s[pl.ds(rf, SUBLANES), 0:st] = gfr
            g_s[pl.ds(rf, SUBLANES), st:2 * st] = gfi
            g_s[pl.ds(rb, SUBLANES), 2 * st:3 * st] = gbr
            g_s[pl.ds(rb, SUBLANES), 3 * st:4 * st] = gbi
            last = lambda v: jnp.broadcast_to(v[SUBLANES - 1:SUBLANES, :], (SUBLANES, st))
            first = lambda v: jnp.broadcast_to(v[0:1, :], (SUBLANES, st))
            out += [last(gfr), last(gfi), first(gbr), first(gbi)]
        return tuple(out)

    z = jnp.zeros((SUBLANES, st), F32)
    lax.fori_loop(0, n_tiles, step, (z,) * (4 * nbat))

    y = _dot(x, tk_ref[...]) + _dot(g_s[...].astype(BF16), mout_ref[...])
    for s in range(q):
        skip = d_ref[...] * u_ref[pl.ds(s, rows, stride=q), :]
        y_ref[pl.ds(s, rows, stride=q), :] = y[:, s * LANES:(s + 1) * LANES] + skip


def _s5(u2, tk, wsin, mout, lamd, lamp, dskip, bsz, seq, nbat):
    q = SSM_CHUNK
    nch = seq // q
    rows = nbat * nch
    tok = nbat * seq
    oct_spec = lambda shape: pl.BlockSpec((None,) + shape, lambda o, i: (o,) + (0,) * len(shape))
    return pl.pallas_call(
        functools.partial(_s5_kernel, nch=nch, nbat=nbat),
        grid=(SSM_OCTETS, bsz // nbat),
        in_specs=[
            pl.BlockSpec((tok, LANES), lambda o, i: (i, o)),
            oct_spec((OCT_IN, OCT_IN)),
            oct_spec((OCT_IN, N_COMP * OCT_STATE)),
            oct_spec((N_COMP * OCT_STATE, OCT_IN)),
            oct_spec((len(SCAN_STEPS) * N_COMP, OCT_STATE)),
            oct_spec((N_COMP * SUBLANES, OCT_STATE)),
            oct_spec((1, LANES)),
        ],
        out_specs=pl.BlockSpec((tok, LANES), lambda o, i: (i, o)),
        out_shape=jax.ShapeDtypeStruct(u2.shape, F32),
        scratch_shapes=[pltpu.VMEM((rows, OCT_IN), BF16),
                        pltpu.VMEM((rows, N_COMP * OCT_STATE), F32)],
        compiler_params=pltpu.CompilerParams(
            dimension_semantics=("parallel", "parallel"), vmem_limit_bytes=VMEM_LIMIT),
        name="s5_chunked",
    )(u2, tk, wsin, mout, lamd, lamp, dskip)


def _s5_params(a_re, a_im, log_step, b_re, b_im, c_re, c_im, d_skip):
    q = SSM_CHUNK
    g, n, c = SSM_GROUPS, SSM_STATE, SSM_GROUP
    no, og = SSM_OCTETS, OCT_GROUPS
    eye = jnp.eye(og, dtype=F32)
    step = jnp.exp(log_step)[..., None]
    dr = a_re * step
    di = a_im * step
    mag = jnp.exp(dr)
    abar_r = mag * jnp.cos(di)
    abar_i = mag * jnp.sin(di)
    den = a_re * a_re + a_im * a_im
    nr = abar_r - 1.0
    sr = (nr * a_re + abar_i * a_im) / den
    si = (abar_i * a_re - nr * a_im) / den
    bb_r = sr[..., None] * b_re - si[..., None] * b_im
    bb_i = sr[..., None] * b_im + si[..., None] * b_re

    def powers(kk):
        kk = jnp.asarray(kk, F32)
        pmag = jnp.exp(dr[..., None] * kk)
        return pmag * jnp.cos(di[..., None] * kk), pmag * jnp.sin(di[..., None] * kk)

    pr, pi = powers(jnp.arange(q + 1))
    cr = c_re[None, :, :, :, None]
    ci = c_im[None, :, :, :, None]
    cpr = cr * pr[:, :, None] - ci * pi[:, :, None]
    cpi = cr * pi[:, :, None] + ci * pr[:, :, None]
    kern = (jnp.einsum('dgcnk,dgne->dgkce', cpr, bb_r)
            - jnp.einsum('dgcnk,dgne->dgkce', cpi, bb_i))[:, :, :q]
    lag = jnp.arange(q)[None, :] - jnp.arange(q)[:, None]
    tk_f = jnp.where((lag >= 0)[None, :, :, None, None],
                     kern[0][:, jnp.clip(lag, 0, q - 1)], 0.0)
    tk_b = jnp.where((lag <= 0)[None, :, :, None, None],
                     kern[1][:, jnp.clip(-lag, 0, q - 1)], 0.0)
    tk = (tk_f + tk_b).reshape(no, og, q, q, c, c)
    tk = jnp.einsum('ogsrce,gh->osgerhc', tk, eye).reshape(no, OCT_IN, OCT_IN)

    def state_in(d, pw):
        wr = pr[d][:, :, pw][..., None] * bb_r[d][:, :, None, :] \
            - pi[d][:, :, pw][..., None] * bb_i[d][:, :, None, :]
        wi = pr[d][:, :, pw][..., None] * bb_i[d][:, :, None, :] \
            + pi[d][:, :, pw][..., None] * bb_r[d][:, :, None, :]
        lay = lambda w_: jnp.einsum('ognse,gh->osgehn', w_.reshape(no, og, n, q, c), eye
                                    ).reshape(no, OCT_IN, OCT_STATE)
        return lay(wr), lay(wi)

    wsin = jnp.concatenate(state_in(0, q - 1 - jnp.arange(q)) + state_in(1, jnp.arange(q)), axis=-1)

    def read_out(d, pw):
        lay = lambda w_: jnp.einsum('ogcnr,gh->ognrhc', w_.reshape(no, og, c, n, q), eye
                                    ).reshape(no, OCT_STATE, OCT_IN)
        return lay(cpr[d][..., pw]), lay(-cpi[d][..., pw])

    mout = jnp.concatenate(read_out(0, jnp.arange(q) + 1) + read_out(1, q - jnp.arange(q)), axis=1)

    def scan_rows(kk):
        sr_, si_ = powers(q * jnp.asarray(kk))
        rows = jnp.stack([sr_[0], si_[0], sr_[1], si_[1]], axis=0)
        return rows.reshape(N_COMP, no, og, n, len(kk))

    lamd = scan_rows(SCAN_STEPS).transpose(1, 4, 0, 2, 3).reshape(
        no, len(SCAN_STEPS) * N_COMP, OCT_STATE)
    fwd = scan_rows(list(range(1, SUBLANES + 1)))[:2]
    bwd = scan_rows(list(range(SUBLANES, 0, -1)))[2:]
    lamp = jnp.concatenate([fwd, bwd], axis=0).transpose(1, 0, 4, 2, 3).reshape(
        no, N_COMP * SUBLANES, OCT_STATE)
    dskip = d_skip.reshape(no, 1, LANES)
    return tk.astype(BF16), wsin.astype(BF16), mout.astype(BF16), lamd, lamp, dskip


def _merge_kernel(x_ref, g_ref, oa_ref, hs_ref, gr_ref, ys_ref, wgate_ref, wb_ref, wglu_ref,
                  bglu_ref, wout_ref, o_ref):
    x = x_ref[...]
    hn = _rms(x, g_ref[...]).astype(BF16)
    bw = BRANCH_WIDTH
    o_lru = (jax.nn.gelu(gr_ref[...].astype(F32)) * hs_ref[...]).astype(BF16)
    zg = _dot(jax.nn.gelu(ys_ref[...]).astype(BF16), wglu_ref[...]) + bglu_ref[...]
    o_ssm = (zg[:, :bw] * jax.nn.sigmoid(zg[:, bw:])).astype(BF16)
    mixed = None
    for kb, ob in enumerate((oa_ref[...], o_lru, o_ssm)):
        gate = jax.nn.sigmoid(_dot(hn, wgate_ref[:, kb * D_MODEL:(kb + 1) * D_MODEL]))
        term = _dot(ob, wb_ref[kb]) * gate
        mixed = term if mixed is None else mixed + term
    o_ref[...] = x + _dot(mixed.astype(BF16), wout_ref[...])


def _merge(x2, g, o_attn, h_sum, gr, y_ssm, w_gate, w_branch, w_glu, b_glu, w_out, tm):
    t = x2.shape[0]
    row = lambda w: pl.BlockSpec((tm, w), lambda i: (i, 0))
    return pl.pallas_call(
        _merge_kernel,
        grid=(t // tm,),
        in_specs=[row(D_MODEL), _const_spec((1, D_MODEL)), row(BRANCH_WIDTH), row(BRANCH_WIDTH),
                  row(BRANCH_WIDTH), row(BRANCH_WIDTH),
                  _const_spec((D_MODEL, N_BRANCH * D_MODEL)),
                  _const_spec((N_BRANCH, BRANCH_WIDTH, D_MODEL)),
                  _const_spec((BRANCH_WIDTH, 2 * BRANCH_WIDTH)),
                  _const_spec((1, 2 * BRANCH_WIDTH)),
                  _const_spec((D_MODEL, D_MODEL))],
        out_specs=row(D_MODEL),
        out_shape=jax.ShapeDtypeStruct((t, D_MODEL), F32),
        compiler_params=pltpu.CompilerParams(
            dimension_semantics=("parallel",), vmem_limit_bytes=VMEM_LIMIT),
        name="merge",
    )(x2, g, o_attn, h_sum, gr, y_ssm, w_gate, w_branch, w_glu, b_glu, w_out)


def _route(logits):
    lane = lax.broadcasted_iota(jnp.int32, logits.shape, 1).astype(F32)
    neg = -jnp.inf
    first = lambda mask: jnp.min(jnp.where(mask, lane, float(ROUTE_COLS)), axis=-1, keepdims=True)
    gmask = lane < N_GROUPS
    gmax = jnp.max(jnp.where(gmask, logits, neg), axis=-1, keepdims=True)
    gidx = first(gmask & (logits == gmax))
    gsum = jnp.sum(jnp.where(gmask, jnp.exp(logits - gmax), 0.0), axis=-1, keepdims=True)
    g_w = 1.0 / gsum
    e_lo = N_GROUPS + EXPERTS_PER_GROUP * gidx
    emask = (lane >= e_lo) & (lane < e_lo + EXPERTS_PER_GROUP)
    v1 = jnp.max(jnp.where(emask, logits, neg), axis=-1, keepdims=True)
    i1 = first(emask & (logits == v1))
    emask2 = emask & (lane != i1)
    v2 = jnp.max(jnp.where(emask2, logits, neg), axis=-1, keepdims=True)
    i2 = first(emask2 & (logits == v2))
    t = jnp.exp(v2 - v1)
    w1 = 1.0 / (1.0 + t)
    w2 = t * w1
    return jnp.where(lane == i1, w1 * g_w, jnp.where(lane == i2, w2 * g_w, 0.0))


def _moe_kernel(x_ref, g_ref, wr_ref, br_ref, w13_ref, w2_ref, fg_ref, o_ref, hn_s, cw_s, acc_s,
                *, final_norm):
    e = pl.program_id(1)

    @pl.when(e == 0)
    def _():
        hn = _rms(x_ref[...], g_ref[...])
        hn_s[...] = hn.astype(BF16)
        logits = jnp.dot(hn, wr_ref[...], preferred_element_type=F32,
                         precision=lax.Precision.HIGHEST) + br_ref[...]
        cw_s[...] = _route(logits)
        acc_s[...] = jnp.zeros_like(acc_s)

    h = _dot(hn_s[...], w13_ref[...])
    lane = lax.broadcasted_iota(jnp.int32, cw_s.shape, 1)
    wcol = jnp.sum(jnp.where(lane == e + N_GROUPS, cw_s[...], 0.0), axis=-1, keepdims=True)
    hg = jax.nn.silu(h[:, :D_EXPERT]) * h[:, D_EXPERT:] * wcol
    acc_s[...] += _dot(hg.astype(BF16), w2_ref[...])

    @pl.when(e == N_EXPERTS - 1)
    def _():
        y = x_ref[...] + acc_s[...]
        if final_norm:
            y = _rms(y, fg_ref[...])
        o_ref[...] = y


def _moe(x2, g, w_route, b_route, w13, w2, final_g, tm, final_norm):
    t = x2.shape[0]
    return pl.pallas_call(
        functools.partial(_moe_kernel, final_norm=final_norm),
        grid=(t // tm, N_EXPERTS),
        in_specs=[pl.BlockSpec((tm, D_MODEL), lambda i, e: (i, 0)),
                  _const_spec((1, D_MODEL)),
                  _const_spec((D_MODEL, ROUTE_COLS)),
                  _const_spec((1, ROUTE_COLS)),
                  pl.BlockSpec((None, D_MODEL, 2 * D_EXPERT), lambda i, e: (e, 0, 0)),
                  pl.BlockSpec((None, D_EXPERT, D_MODEL), lambda i, e: (e, 0, 0)),
                  _const_spec((1, D_MODEL))],
        out_specs=pl.BlockSpec((tm, D_MODEL), lambda i, e: (i, 0)),
        out_shape=jax.ShapeDtypeStruct((t, D_MODEL), F32),
        scratch_shapes=[pltpu.VMEM((tm, D_MODEL), BF16), pltpu.VMEM((tm, ROUTE_COLS), F32),
                        pltpu.VMEM((tm, D_MODEL), F32)],
        compiler_params=pltpu.CompilerParams(
            dimension_semantics=("parallel", "arbitrary"), vmem_limit_bytes=VMEM_LIMIT),
        name="moe_final" if final_norm else "moe",
    )(x2, g, w_route, b_route, w13, w2, final_g)


def _pick_tile(n, want):
    t = min(n, want)
    while n % t:
        t //= 2
    return t


def _layer(x2, bsz, seq, p, final_g, final_norm):
    t = bsz * seq
    bw = BRANCH_WIDTH
    tm = _pick_tile(t, 512)

    w_in = p['w_in']
    qkv, xr, gr, u = _inproj(x2, p['norm1_g'][None], w_in[:, :MAIN_COLS].astype(BF16), tm)

    o_attn = _attention(qkv.reshape(bsz, seq, QKV_WIDTH), p['attn_sink']).reshape(t, bw)

    wg = jnp.stack([jnp.concatenate([_block_diag(p['lru_w_r'][d]), _block_diag(p['lru_w_i'][d])],
                                    axis=1) for d in range(2)]).astype(BF16)
    bg = jnp.concatenate([p['lru_b_r'], p['lru_b_i']], axis=-1)[:, None]
    sp = jax.nn.softplus(-p['lru_lambda'])[:, None]
    h_sum = _lru(xr.reshape(bsz, seq, bw), p['lru_conv_w'], p['lru_conv_b'][None], wg, bg, sp
                 ).reshape(t, bw)

    s5p = _s5_params(p['ssm_a_re'], p['ssm_a_im'], p['ssm_log_step'], p['ssm_b_re'], p['ssm_b_im'],
                     p['ssm_c_re'], p['ssm_c_im'], p['ssm_d'])
    y_ssm = _s5(u, *s5p, bsz, seq, _pick_tile(bsz, 2))

    x2 = _merge(x2, p['norm1_g'][None], o_attn, h_sum, gr, y_ssm,
                w_in[:, MAIN_COLS:].astype(BF16), p['w_branch'].astype(BF16),
                p['ssm_w_glu'].astype(BF16), p['ssm_b_glu'][None], p['w_out'].astype(BF16), tm)

    pad = ROUTE_COLS - N_GROUPS - N_EXPERTS
    w_route = jnp.concatenate(
        [p['moe_w_group'], p['moe_w_expert'], jnp.zeros((D_MODEL, pad), F32)], axis=1)
    b_route = jnp.concatenate(
        [p['moe_b_group'], p['moe_b_expert'], jnp.zeros((pad,), F32)])[None]
    w13 = jnp.concatenate([p['moe_w1'], p['moe_w3']], axis=-1).reshape(
        N_EXPERTS, D_MODEL, 2 * D_EXPERT).astype(BF16)
    w2 = p['moe_w2'].reshape(N_EXPERTS, D_EXPERT, D_MODEL).astype(BF16)
    return _moe(x2, p['norm2_g'][None], w_route, b_route, w13, w2, final_g[None],
                _pick_tile(t, 1024), final_norm)


_LAYER_KEYS = ('norm1_g', 'w_in', 'attn_sink', 'lru_conv_w', 'lru_conv_b', 'lru_w_r', 'lru_b_r',
               'lru_w_i', 'lru_b_i', 'lru_lambda', 'ssm_a_re', 'ssm_a_im', 'ssm_log_step',
               'ssm_b_re', 'ssm_b_im', 'ssm_c_re', 'ssm_c_im', 'ssm_d', 'ssm_w_glu', 'ssm_b_glu',
               'w_branch', 'w_out', 'norm2_g', 'moe_w_group', 'moe_b_group', 'moe_w_expert',
               'moe_b_expert', 'moe_w1', 'moe_w3', 'moe_w2')


def kernel(x, norm1_g, w_in, attn_sink, lru_conv_w, lru_conv_b, lru_w_r, lru_b_r, lru_w_i, lru_b_i,
           lru_lambda, ssm_a_re, ssm_a_im, ssm_log_step, ssm_b_re, ssm_b_im, ssm_c_re, ssm_c_im,
           ssm_d, ssm_w_glu, ssm_b_glu, w_branch, w_out, norm2_g, moe_w_group, moe_b_group,
           moe_w_expert, moe_b_expert, moe_w1, moe_w3, moe_w2, final_norm_g):
    stacked = dict(zip(_LAYER_KEYS, (
        norm1_g, w_in, attn_sink, lru_conv_w, lru_conv_b, lru_w_r, lru_b_r, lru_w_i, lru_b_i,
        lru_lambda, ssm_a_re, ssm_a_im, ssm_log_step, ssm_b_re, ssm_b_im, ssm_c_re, ssm_c_im,
        ssm_d, ssm_w_glu, ssm_b_glu, w_branch, w_out, norm2_g, moe_w_group, moe_b_group,
        moe_w_expert, moe_b_expert, moe_w1, moe_w3, moe_w2)))
    bsz, seq, _ = x.shape
    depth = norm1_g.shape[0]
    x2 = x.reshape(bsz * seq, D_MODEL)
    for i in range(depth):
        p = {k: v[i] for k, v in stacked.items()}
        x2 = _layer(x2, bsz, seq, p, final_norm_g, final_norm=(i == depth - 1))
    return x2.reshape(bsz, seq, D_MODEL)
```

```python
import functools

import jax
import jax.numpy as jnp
from jax import lax
from jax.experimental import pallas as pl
from jax.experimental.pallas import tpu as pltpu

F32 = jnp.float32
BF16 = jnp.bfloat16

D_MODEL = 1024
N_Q_HEADS = 8
N_KV_HEADS = 2
HEAD_DIM = 64
WINDOW = 128
BLOCK = 128
BRANCH_WIDTH = 512
KV_WIDTH = N_KV_HEADS * HEAD_DIM
LRU_BLOCKS = 8
LRU_C = 8.0
SSM_GROUP = 16
SSM_GROUPS = BRANCH_WIDTH // SSM_GROUP
SSM_STATE = 64
N_BRANCH = 3
N_GROUPS = 4
EXPERTS_PER_GROUP = 4
N_EXPERTS = N_GROUPS * EXPERTS_PER_GROUP
D_EXPERT = 256
GROUP_HIDDEN = EXPERTS_PER_GROUP * D_EXPERT
NORM_EPS = 1e-6
MASK_VALUE = -1e30

LANES = 128
SUBLANES = 8
QKV_WIDTH = BRANCH_WIDTH + 2 * KV_WIDTH
MAIN_COLS = QKV_WIDTH + 3 * BRANCH_WIDTH
Q_PER_KV_PAIRS = (N_Q_HEADS // N_KV_HEADS) // 2
SSM_CHUNK = 8
SSM_OCTETS = BRANCH_WIDTH // LANES
OCT_GROUPS = LANES // SSM_GROUP
OCT_IN = SSM_CHUNK * LANES
OCT_STATE = OCT_GROUPS * SSM_STATE
N_COMP = 4
SCAN_STEPS = (1, 2, 4)
ROUTE_COLS = LANES
CONV_HALO = 16
VMEM_LIMIT = 56 * 1024 * 1024


def _rms(x, g):
    ms = jnp.mean(x * x, axis=-1, keepdims=True)
    return x * lax.rsqrt(ms + NORM_EPS) * g


def _sigmoid(x):
    return 0.5 * jnp.tanh(0.5 * x) + 0.5


def _dot(a, b):
    return jnp.dot(a, b, preferred_element_type=F32)


def _dot_nt(a, b):
    return lax.dot_general(a, b, (((1,), (1,)), ((), ())), preferred_element_type=F32)


def _const_spec(shape):
    nd = len(shape)
    return pl.BlockSpec(shape, lambda *_: (0,) * nd)


def _layer_spec(layer, shape, **kw):
    nd = len(shape)
    return pl.BlockSpec((None,) + tuple(shape), lambda *_: (layer,) + (0,) * nd, **kw)


def _row_in_tile(rows):
    return lax.broadcasted_iota(jnp.int32, (rows, 1), 0) & (SUBLANES - 1)


def _pick_tile(n, want):
    t = min(n, want)
    while n % t:
        t //= 2
    return t


def _inproj_kernel(x_ref, g_ref, w_ref, qkv_ref, xr_ref, gr_ref, u_ref):
    hn = _rms(x_ref[...], g_ref[...]).astype(BF16)
    bw = BRANCH_WIDTH
    qkv_ref[...] = _dot(hn, w_ref[:, 0:QKV_WIDTH]).astype(BF16)
    xr_ref[...] = _dot(hn, w_ref[:, QKV_WIDTH:QKV_WIDTH + bw]).astype(BF16)
    gr_ref[...] = _dot(hn, w_ref[:, QKV_WIDTH + bw:QKV_WIDTH + 2 * bw]).astype(BF16)
    u_ref[...] = _dot(hn, w_ref[:, QKV_WIDTH + 2 * bw:MAIN_COLS])


def _inproj(layer, x2, g, w_main, tm):
    t = x2.shape[0]
    row = lambda w: pl.BlockSpec((tm, w), lambda i: (i, 0))
    return pl.pallas_call(
        _inproj_kernel,
        grid=(t // tm,),
        in_specs=[row(D_MODEL), _layer_spec(layer, (1, D_MODEL)),
                  _layer_spec(layer, (D_MODEL, MAIN_COLS))],
        out_specs=[row(QKV_WIDTH), row(BRANCH_WIDTH), row(BRANCH_WIDTH), row(BRANCH_WIDTH)],
        out_shape=[jax.ShapeDtypeStruct((t, QKV_WIDTH), BF16)]
        + [jax.ShapeDtypeStruct((t, BRANCH_WIDTH), BF16)] * 2
        + [jax.ShapeDtypeStruct((t, BRANCH_WIDTH), F32)],
        compiler_params=pltpu.CompilerParams(
            dimension_semantics=("parallel",), vmem_limit_bytes=VMEM_LIMIT),
        name="inproj",
    )(x2, g, w_main)


def _attn_kernel(sink_ref, q_ref, kvp_ref, kvc_ref, kvn_ref, o_ref, *, nb, layer):
    n = pl.program_id(1)
    kv = jnp.concatenate([kvp_ref[...], kvc_ref[...], kvn_ref[...]], axis=0).astype(F32)
    k = kv[:, :KV_WIDTH]
    v = kv[:, KV_WIDTH:]
    lane = lax.broadcasted_iota(jnp.int32, (1, KV_WIDTH), 1)
    lo = lane < HEAD_DIM

    qi = lax.broadcasted_iota(jnp.int32, (BLOCK, 3 * BLOCK), 0)
    kj = lax.broadcasted_iota(jnp.int32, (BLOCK, 3 * BLOCK), 1)
    dist = jnp.abs(qi - kj + BLOCK)
    valid = dist <= WINDOW
    valid = valid & ((kj >= BLOCK) | (n > 0)) & ((kj < 2 * BLOCK) | (n < nb - 1))
    distf = dist.astype(F32)

    k2, vlo, vhi = [], [], []
    for h in range(N_KV_HEADS):
        own = lo if h == 0 else jnp.logical_not(lo)
        km = jnp.where(own, k, 0.0)
        vm = jnp.where(own, v, 0.0)
        k2.append((km + pltpu.roll(km, HEAD_DIM, 1)).astype(BF16))
        vr = pltpu.roll(vm, HEAD_DIM, 1)
        vlo.append((vm if h == 0 else vr).astype(BF16))
        vhi.append((vr if h == 0 else vm).astype(BF16))

    scale = HEAD_DIM ** -0.5
    for pair in range(N_Q_HEADS // 2):
        h = pair // Q_PER_KV_PAIRS
        qp = q_ref[:, pair * KV_WIDTH:(pair + 1) * KV_WIDTH].astype(F32) * scale
        acc = None
        for half in range(2):
            head = 2 * pair + half
            own = lo if half == 0 else jnp.logical_not(lo)
            qh = jnp.where(own, qp, 0.0).astype(BF16)
            s = _dot_nt(qh, k2[h])
            slope = 2.0 ** (-8.0 * (head + 1) / N_Q_HEADS)
            s = jnp.where(valid, s - slope * distf, MASK_VALUE)
            sink = sink_ref[layer, head]
            m = jnp.maximum(jnp.max(s, axis=-1, keepdims=True), sink)
            p = jnp.exp(s - m)
            denom = jnp.sum(p, axis=-1, keepdims=True) + jnp.exp(sink - m)
            r = _dot(p.astype(BF16), vlo[h] if half == 0 else vhi[h])
            r = r / denom
            acc = r if acc is None else acc + r
        o_ref[:, pair * KV_WIDTH:(pair + 1) * KV_WIDTH] = acc.astype(BF16)


def _attention(layer, qkv3, sink):
    b, l, _ = qkv3.shape
    nb = l // BLOCK
    kv_col = BRANCH_WIDTH // (2 * KV_WIDTH)
    kv_spec = lambda f: pl.BlockSpec((None, BLOCK, 2 * KV_WIDTH), lambda bi, n: (bi, f(n), kv_col))
    return pl.pallas_call(
        functools.partial(_attn_kernel, nb=nb, layer=layer),
        grid=(b, nb),
        in_specs=[
            pl.BlockSpec(memory_space=pltpu.SMEM),
            pl.BlockSpec((None, BLOCK, BRANCH_WIDTH), lambda bi, n: (bi, n, 0)),
            kv_spec(lambda n: jnp.maximum(n - 1, 0)),
            kv_spec(lambda n: n),
            kv_spec(lambda n: jnp.minimum(n + 1, nb - 1)),
        ],
        out_specs=pl.BlockSpec((None, BLOCK, BRANCH_WIDTH), lambda bi, n: (bi, n, 0)),
        out_shape=jax.ShapeDtypeStruct((b, l, BRANCH_WIDTH), BF16),
        compiler_params=pltpu.CompilerParams(
            dimension_semantics=("parallel", "parallel"), vmem_limit_bytes=VMEM_LIMIT),
        name="window_attn",
    )(sink, qkv3, qkv3, qkv3, qkv3)


def _tile_scan(a, u, reverse):
    rows = a.shape[0]
    rit = _row_in_tile(rows)
    for d in SCAN_STEPS:
        if reverse:
            take = rit < SUBLANES - d
            shift = rows - d
        else:
            take = rit >= d
            shift = d
        a_sh = jnp.where(take, pltpu.roll(a, shift, 0), 1.0)
        u_sh = jnp.where(take, pltpu.roll(u, shift, 0), 0.0)
        u = u + a * u_sh
        a = a * a_sh
    return a, u


def _lru_kernel(x_ref, cw_ref, cb_ref, wg_ref, bg_ref, sp_ref, o_ref, xe_s, a_s, u_s, hf_s,
                *, seq, rc):
    phase = pl.program_id(1)
    bw = BRANCH_WIDTH
    n_tiles = seq // SUBLANES

    zeros = jnp.zeros((CONV_HALO, bw), F32)
    xe_s[0:CONV_HALO, :] = zeros
    xe_s[CONV_HALO + seq:CONV_HALO + seq + CONV_HALO, :] = zeros
    xe_s[CONV_HALO:CONV_HALO + seq, :] = x_ref[...].astype(F32)
    cw = cw_ref[...]

    def build(reverse):
        for c in range(seq // rc):
            base = CONV_HALO + c * rc
            xc = cb_ref[...]
            for tap in range(4):
                xc = xc + cw[tap:tap + 1] * xe_s[base + tap - 2:base + tap - 2 + rc, :]
            gates = _sigmoid(_dot(xc.astype(BF16), wg_ref[...]) + bg_ref[...])
            log_a = -LRU_C * gates[:, :bw] * sp_ref[...]
            a = jnp.exp(log_a)
            u = jnp.sqrt(1.0 - a * a) * (gates[:, bw:] * xc)
            a, u = _tile_scan(a, u, reverse)
            a_s[c * rc:(c + 1) * rc, :] = a
            u_s[c * rc:(c + 1) * rc, :] = u

    @pl.when(phase == 0)
    def _():
        build(False)

        def step(k, carry):
            r0 = pl.multiple_of(k * SUBLANES, SUBLANES)
            h = u_s[pl.ds(r0, SUBLANES), :] + a_s[pl.ds(r0, SUBLANES), :] * carry
            hf_s[pl.ds(r0, SUBLANES), :] = h
            return jnp.broadcast_to(h[SUBLANES - 1:SUBLANES, :], (SUBLANES, bw))

        lax.fori_loop(0, n_tiles, step, jnp.zeros((SUBLANES, bw), F32), unroll=4)

    @pl.when(phase == 1)
    def _():
        build(True)

        def step(k, carry):
            r0 = pl.multiple_of((n_tiles - 1 - k) * SUBLANES, SUBLANES)
            h = u_s[pl.ds(r0, SUBLANES), :] + a_s[pl.ds(r0, SUBLANES), :] * carry
            o_ref[pl.ds(r0, SUBLANES), :] = h + hf_s[pl.ds(r0, SUBLANES), :]
            return jnp.broadcast_to(h[0:1, :], (SUBLANES, bw))

        lax.fori_loop(0, n_tiles, step, jnp.zeros((SUBLANES, bw), F32), unroll=4)


def _lru(layer, xr3, conv_w, conv_b, wg, bg, sp):
    b, l, w = xr3.shape
    seq_spec = pl.BlockSpec((None, l, w), lambda bi, ph: (bi, 0, 0))
    per_dir = lambda shape: pl.BlockSpec(
        (None, None) + shape, lambda bi, ph: (layer, ph) + (0,) * len(shape))
    return pl.pallas_call(
        functools.partial(_lru_kernel, seq=l, rc=_pick_tile(l, 512)),
        grid=(b, 2),
        in_specs=[seq_spec, _layer_spec(layer, (4, w)), _layer_spec(layer, (1, w)),
                  per_dir((w, 2 * w)), per_dir((1, 2 * w)), per_dir((1, w))],
        out_specs=seq_spec,
        out_shape=jax.ShapeDtypeStruct((b, l, w), F32),
        scratch_shapes=[pltpu.VMEM((l + 2 * CONV_HALO, w), F32), pltpu.VMEM((l, w), F32),
                        pltpu.VMEM((l, w), F32), pltpu.VMEM((l, w), F32)],
        compiler_params=pltpu.CompilerParams(
            dimension_semantics=("parallel", "arbitrary"), vmem_limit_bytes=VMEM_LIMIT),
        name="rg_lru",
    )(xr3, conv_w, conv_b, wg, bg, sp)


def _block_diag(wb):
    nbk, bd = wb.shape[-3], wb.shape[-2]
    eye = jnp.eye(nbk, dtype=wb.dtype)
    out = jnp.einsum('...hij,hk->...hikj', wb, eye)
    return out.reshape(wb.shape[:-3] + (nbk * bd, nbk * bd))


def _expand_groups(compact, out_cols, hi_shift_out, hi_shift_in, lo_mask, row_shift, col_shift):
    cc = compact.shape[1]
    i = lax.broadcasted_iota(jnp.int32, (cc, out_cols), 0)
    j = lax.broadcasted_iota(jnp.int32, (cc, out_cols), 1)
    place = ((j >> hi_shift_out) == (i >> hi_shift_in)) & ((j & lo_mask) == (i & lo_mask))
    spread = _dot(compact, place.astype(BF16))
    rg = (lax.broadcasted_iota(jnp.int32, (compact.shape[0], 1), 0) >> row_shift) & (OCT_GROUPS - 1)
    cg = (lax.broadcasted_iota(jnp.int32, (1, out_cols), 1) >> col_shift) & (OCT_GROUPS - 1)
    return jnp.where(rg == cg, spread, 0.0).astype(BF16)


def _s5_kernel(u_ref, tkc_ref, wsinc_ref, moutc_ref, lamd_ref, lamp_ref, d_ref, y_ref,
               x_s, g_s, tk_s, wsin_s, mout_s, *, nch, nbat):
    q = SSM_CHUNK
    rows = nbat * nch
    st = OCT_STATE
    n_tiles = nch // SUBLANES
    c_bits = SSM_GROUP.bit_length() - 1
    n_bits = SSM_STATE.bit_length() - 1

    @pl.when(pl.program_id(1) == 0)
    def _():
        tk_s[...] = _expand_groups(tkc_ref[...], OCT_IN, 7, c_bits, SSM_GROUP - 1, c_bits, c_bits)
        wsin_s[...] = _expand_groups(wsinc_ref[...], N_COMP * st, 9, n_bits, SSM_STATE - 1,
                                     c_bits, n_bits)
        mout_s[...] = _expand_groups(moutc_ref[...], OCT_IN, 7, c_bits, SSM_GROUP - 1,
                                     n_bits, c_bits)

    for s in range(q):
        x_s[:, s * LANES:(s + 1) * LANES] = u_ref[pl.ds(s, rows, stride=q), :].astype(BF16)
    x = x_s[...]

    sloc = _dot(x, wsin_s[...])
    chunk = lax.broadcasted_iota(jnp.int32, (rows, 1), 0) & (nch - 1)
    rit = _row_in_tile(rows)
    comps = []
    for comp in range(N_COMP):
        reverse = comp >= 2
        piece = sloc[:, comp * st:(comp + 1) * st]
        if reverse:
            piece = jnp.where(chunk == nch - 1, 0.0, pltpu.roll(piece, rows - 1, 0))
        else:
            piece = jnp.where(chunk == 0, 0.0, pltpu.roll(piece, 1, 0))
        comps.append(piece)

    for di, d in enumerate(SCAN_STEPS):
        for base in (0, 2):
            reverse = base == 2
            lr = lamd_ref[di * N_COMP + base:di * N_COMP + base + 1, :]
            li = lamd_ref[di * N_COMP + base + 1:di * N_COMP + base + 2, :]
            if reverse:
                take = rit < SUBLANES - d
                shift = rows - d
            else:
                take = rit >= d
                shift = d
            gr, gi = comps[base], comps[base + 1]
            sr = jnp.where(take, pltpu.roll(gr, shift, 0), 0.0)
            si = jnp.where(take, pltpu.roll(gi, shift, 0), 0.0)
            comps[base] = gr + lr * sr - li * si
            comps[base + 1] = gi + lr * si + li * sr
    for comp in range(N_COMP):
        g_s[:, comp * st:(comp + 1) * st] = comps[comp]

    pf_r, pf_i = lamp_ref[0:SUBLANES, :], lamp_ref[SUBLANES:2 * SUBLANES, :]
    pb_r, pb_i = lamp_ref[2 * SUBLANES:3 * SUBLANES, :], lamp_ref[3 * SUBLANES:4 * SUBLANES, :]

    def step(k, carry):
        out = []
        for bi in range(nbat):
            cfr, cfi, cbr, cbi = carry[4 * bi:4 * bi + 4]
            rf = pl.multiple_of(bi * nch + k * SUBLANES, SUBLANES)
            rb = pl.multiple_of(bi * nch + (n_tiles - 1 - k) * SUBLANES, SUBLANES)
            gfr = g_s[pl.ds(rf, SUBLANES), 0:st] + pf_r * cfr - pf_i * cfi
            gfi = g_s[pl.ds(rf, SUBLANES), st:2 * st] + pf_r * cfi + pf_i * cfr
            gbr = g_s[pl.ds(rb, SUBLANES), 2 * st:3 * st] + pb_r * cbr - pb_i * cbi
            gbi = g_s[pl.ds(rb, SUBLANES), 3 * st:4 * st] + pb_r * cbi + pb_i * cbr
            g_s[pl.ds(rf, SUBLANES), 0:st] = gfr
            g_s[pl.ds(rf, SUBLANES), st:2 * st] = gfi
            g_s[pl.ds(rb, SUBLANES), 2 * st:3 * st] = gbr
            g_s[pl.ds(rb, SUBLANES), 3 * st:4 * st] = gbi
            last = lambda v: jnp.broadcast_to(v[SUBLANES - 1:SUBLANES, :], (SUBLANES, st))
            first = lambda v: jnp.broadcast_to(v[0:1, :], (SUBLANES, st))
            out += [last(gfr), last(gfi), first(gbr), first(gbi)]
        return tuple(out)

    z = jnp.zeros((SUBLANES, st), F32)
    lax.fori_loop(0, n_tiles, step, (z,) * (4 * nbat))

    y = _dot(x, tk_s[...]) + _dot(g_s[...].astype(BF16), mout_s[...])
    for s in range(q):
        skip = d_ref[...] * u_ref[pl.ds(s, rows, stride=q), :]
        y_ref[pl.ds(s, rows, stride=q), :] = y[:, s * LANES:(s + 1) * LANES] + skip


def _s5(layer, u2, tkc, wsinc, moutc, lamd, lamp, dskip, bsz, seq, nbat):
    q = SSM_CHUNK
    nch = seq // q
    assert nch & (nch - 1) == 0 and nch % SUBLANES == 0, "chunks per sequence must be a power of two"
    rows = nbat * nch
    tok = nbat * seq
    st = OCT_STATE
    oct_spec = lambda shape: pl.BlockSpec(
        (None, None) + shape, lambda o, i: (layer, o) + (0,) * len(shape))
    return pl.pallas_call(
        functools.partial(_s5_kernel, nch=nch, nbat=nbat),
        grid=(SSM_OCTETS, bsz // nbat),
        in_specs=[
            pl.BlockSpec((tok, LANES), lambda o, i: (i, o)),
            oct_spec((OCT_IN, LANES)),
            oct_spec((OCT_IN, N_COMP * SSM_STATE)),
            oct_spec((N_COMP * st, LANES)),
            oct_spec((len(SCAN_STEPS) * N_COMP, st)),
            oct_spec((N_COMP * SUBLANES, st)),
            oct_spec((1, LANES)),
        ],
        out_specs=pl.BlockSpec((tok, LANES), lambda o, i: (i, o)),
        out_shape=jax.ShapeDtypeStruct(u2.shape, F32),
        scratch_shapes=[pltpu.VMEM((rows, OCT_IN), BF16),
                        pltpu.VMEM((rows, N_COMP * st), F32),
                        pltpu.VMEM((OCT_IN, OCT_IN), BF16),
                        pltpu.VMEM((OCT_IN, N_COMP * st), BF16),
                        pltpu.VMEM((N_COMP * st, OCT_IN), BF16)],
        compiler_params=pltpu.CompilerParams(
            dimension_semantics=("parallel", "arbitrary"), vmem_limit_bytes=VMEM_LIMIT),
        name="s5_chunked",
    )(u2, tkc, wsinc, moutc, lamd, lamp, dskip)


def _s5_params(a_re, a_im, log_step, b_re, b_im, c_re, c_im, d_skip):
    q = SSM_CHUNK
    g, n, c = SSM_GROUPS, SSM_STATE, SSM_GROUP
    no, og = SSM_OCTETS, OCT_GROUPS
    step = jnp.exp(log_step)[..., None]
    dr = a_re * step
    di = a_im * step
    mag = jnp.exp(dr)
    abar_r = mag * jnp.cos(di)
    abar_i = mag * jnp.sin(di)
    den = a_re * a_re + a_im * a_im
    nr = abar_r - 1.0
    sr = (nr * a_re + abar_i * a_im) / den
    si = (abar_i * a_re - nr * a_im) / den
    bb_r = sr[..., None] * b_re - si[..., None] * b_im
    bb_i = sr[..., None] * b_im + si[..., None] * b_re

    def powers(kk):
        kk = jnp.asarray(kk, F32)
        pmag = jnp.exp(dr[..., None] * kk)
        return pmag * jnp.cos(di[..., None] * kk), pmag * jnp.sin(di[..., None] * kk)

    pr, pi = powers(jnp.arange(q + 1))
    cr = c_re[None, :, :, :, None]
    ci = c_im[None, :, :, :, None]
    cpr = cr * pr[:, :, None] - ci * pi[:, :, None]
    cpi = cr * pi[:, :, None] + ci * pr[:, :, None]
    kern = (jnp.einsum('dgcnk,dgne->dgkce', cpr, bb_r)
            - jnp.einsum('dgcnk,dgne->dgkce', cpi, bb_i))[:, :, :q]
    lag = jnp.arange(q)[None, :] - jnp.arange(q)[:, None]
    tk_f = jnp.where((lag >= 0)[None, :, :, None, None],
                     kern[0][:, jnp.clip(lag, 0, q - 1)], 0.0)
    tk_b = jnp.where((lag <= 0)[None, :, :, None, None],
                     kern[1][:, jnp.clip(-lag, 0, q - 1)], 0.0)
    tkc = (tk_f + tk_b).reshape(no, og, q, q, c, c).transpose(0, 2, 1, 5, 3, 4).reshape(
        no, OCT_IN, LANES)

    def state_in(d, pw):
        wr = pr[d][:, :, pw][..., None] * bb_r[d][:, :, None, :] \
            - pi[d][:, :, pw][..., None] * bb_i[d][:, :, None, :]
        wi = pr[d][:, :, pw][..., None] * bb_i[d][:, :, None, :] \
            + pi[d][:, :, pw][..., None] * bb_r[d][:, :, None, :]
        lay = lambda w_: w_.reshape(no, og, n, q, c).transpose(0, 3, 1, 4, 2).reshape(no, OCT_IN, n)
        return lay(wr), lay(wi)

    wsinc = jnp.concatenate(state_in(0, q - 1 - jnp.arange(q)) + state_in(1, jnp.arange(q)), axis=-1)

    def read_out(d, pw):
        lay = lambda w_: w_.reshape(no, og, c, n, q).transpose(0, 1, 3, 4, 2).reshape(
            no, OCT_STATE, LANES)
        return lay(cpr[d][..., pw]), lay(-cpi[d][..., pw])

    moutc = jnp.concatenate(read_out(0, jnp.arange(q) + 1) + read_out(1, q - jnp.arange(q)), axis=1)

    def scan_rows(kk):
        sr_, si_ = powers(q * jnp.asarray(kk))
        rows = jnp.stack([sr_[0], si_[0], sr_[1], si_[1]], axis=0)
        return rows.reshape(N_COMP, no, og, n, len(kk))

    lamd = scan_rows(SCAN_STEPS).transpose(1, 4, 0, 2, 3).reshape(
        no, len(SCAN_STEPS) * N_COMP, OCT_STATE)
    fwd = scan_rows(list(range(1, SUBLANES + 1)))[:2]
    bwd = scan_rows(list(range(SUBLANES, 0, -1)))[2:]
    lamp = jnp.concatenate([fwd, bwd], axis=0).transpose(1, 0, 4, 2, 3).reshape(
        no, N_COMP * SUBLANES, OCT_STATE)
    dskip = d_skip.reshape(no, 1, LANES)
    return tkc.astype(BF16), wsinc.astype(BF16), moutc.astype(BF16), lamd, lamp, dskip


def _merge_kernel(x_ref, g_ref, oa_ref, hs_ref, gr_ref, ys_ref, wgate_ref, wb_ref, wglu_ref,
                  bglu_ref, wout_ref, o_ref):
    x = x_ref[...]
    hn = _rms(x, g_ref[...]).astype(BF16)
    bw = BRANCH_WIDTH
    o_lru = (jax.nn.gelu(gr_ref[...].astype(F32)) * hs_ref[...]).astype(BF16)
    zg = _dot(jax.nn.gelu(ys_ref[...]).astype(BF16), wglu_ref[...]) + bglu_ref[...]
    o_ssm = (zg[:, :bw] * _sigmoid(zg[:, bw:])).astype(BF16)
    mixed = None
    for kb, ob in enumerate((oa_ref[...], o_lru, o_ssm)):
        gate = _sigmoid(_dot(hn, wgate_ref[:, kb * D_MODEL:(kb + 1) * D_MODEL]))
        term = _dot(ob, wb_ref[kb]) * gate
        mixed = term if mixed is None else mixed + term
    o_ref[...] = x + _dot(mixed.astype(BF16), wout_ref[...])


def _merge(layer, x2, g, o_attn, h_sum, gr, y_ssm, w_gate, w_branch, w_glu, b_glu, w_out, tm):
    t = x2.shape[0]
    row = lambda w: pl.BlockSpec((tm, w), lambda i: (i, 0))
    return pl.pallas_call(
        _merge_kernel,
        grid=(t // tm,),
        in_specs=[row(D_MODEL), _layer_spec(layer, (1, D_MODEL)), row(BRANCH_WIDTH),
                  row(BRANCH_WIDTH), row(BRANCH_WIDTH), row(BRANCH_WIDTH),
                  _layer_spec(layer, (D_MODEL, N_BRANCH * D_MODEL)),
                  _layer_spec(layer, (N_BRANCH, BRANCH_WIDTH, D_MODEL)),
                  _layer_spec(layer, (BRANCH_WIDTH, 2 * BRANCH_WIDTH)),
                  _layer_spec(layer, (1, 2 * BRANCH_WIDTH)),
                  _layer_spec(layer, (D_MODEL, D_MODEL))],
        out_specs=row(D_MODEL),
        out_shape=jax.ShapeDtypeStruct((t, D_MODEL), F32),
        compiler_params=pltpu.CompilerParams(
            dimension_semantics=("parallel",), vmem_limit_bytes=VMEM_LIMIT),
        name="merge",
    )(x2, g, o_attn, h_sum, gr, y_ssm, w_gate, w_branch, w_glu, b_glu, w_out)


def _route(logits):
    lane = lax.broadcasted_iota(jnp.int32, logits.shape, 1).astype(F32)
    neg = -jnp.inf
    first = lambda mask: jnp.min(jnp.where(mask, lane, float(ROUTE_COLS)), axis=-1, keepdims=True)
    gmask = lane < N_GROUPS
    gmax = jnp.max(jnp.where(gmask, logits, neg), axis=-1, keepdims=True)
    gidx = first(gmask & (logits == gmax))
    gsum = jnp.sum(jnp.where(gmask, jnp.exp(logits - gmax), 0.0), axis=-1, keepdims=True)
    g_w = 1.0 / gsum
    e_lo = N_GROUPS + EXPERTS_PER_GROUP * gidx
    emask = (lane >= e_lo) & (lane < e_lo + EXPERTS_PER_GROUP)
    v1 = jnp.max(jnp.where(emask, logits, neg), axis=-1, keepdims=True)
    i1 = first(emask & (logits == v1))
    emask2 = emask & (lane != i1)
    v2 = jnp.max(jnp.where(emask2, logits, neg), axis=-1, keepdims=True)
    i2 = first(emask2 & (logits == v2))
    t = jnp.exp(v2 - v1)
    w1 = 1.0 / (1.0 + t)
    w2 = t * w1
    return jnp.where(lane == i1, w1 * g_w, jnp.where(lane == i2, w2 * g_w, 0.0))


def _moe_kernel(x_ref, g_ref, wrh_ref, wrl_ref, br_ref, w13_ref, w2_ref, fg_ref, o_ref,
                *, final_norm):
    x = x_ref[...]
    tm = x.shape[0]
    hn = _rms(x, g_ref[...])
    hb = hn.astype(BF16)
    lo = (hn - hb.astype(F32)).astype(BF16)
    logits = (_dot(hb, wrh_ref[...]) + _dot(hb, wrl_ref[...]) + _dot(lo, wrh_ref[...])
              + br_ref[...])
    cw = _route(logits)
    lane = lax.broadcasted_iota(jnp.int32, cw.shape, 1)
    y = None
    for grp in range(N_GROUPS):
        h = _dot(hb, w13_ref[grp])
        cols = []
        for e in range(EXPERTS_PER_GROUP):
            col = N_GROUPS + grp * EXPERTS_PER_GROUP + e
            wcol = jnp.sum(jnp.where(lane == col, cw, 0.0), axis=-1, keepdims=True)
            cols.append(jnp.broadcast_to(wcol, (tm, D_EXPERT)))
        h1 = h[:, :GROUP_HIDDEN]
        hg = (h1 * _sigmoid(h1) * h[:, GROUP_HIDDEN:] * jnp.concatenate(cols, axis=1)).astype(BF16)
        term = _dot(hg, w2_ref[grp])
        y = term if y is None else y + term
    y = x + y
    if final_norm:
        y = _rms(y, fg_ref[...])
    o_ref[...] = y


def _moe(layer, x2, g, wr_hi, wr_lo, b_route, w13, w2, final_g, tm, final_norm):
    t = x2.shape[0]
    row = pl.BlockSpec((tm, D_MODEL), lambda i: (i, 0))
    once = pl.Buffered(1)
    return pl.pallas_call(
        functools.partial(_moe_kernel, final_norm=final_norm),
        grid=(t // tm,),
        in_specs=[row,
                  _layer_spec(layer, (1, D_MODEL)),
                  _layer_spec(layer, (D_MODEL, ROUTE_COLS)),
                  _layer_spec(layer, (D_MODEL, ROUTE_COLS)),
                  _layer_spec(layer, (1, ROUTE_COLS)),
                  _layer_spec(layer, (N_GROUPS, D_MODEL, 2 * GROUP_HIDDEN), pipeline_mode=once),
                  _layer_spec(layer, (N_GROUPS, GROUP_HIDDEN, D_MODEL), pipeline_mode=once),
                  _const_spec((1, D_MODEL))],
        out_specs=row,
        out_shape=jax.ShapeDtypeStruct((t, D_MODEL), F32),
        compiler_params=pltpu.CompilerParams(
            dimension_semantics=("parallel",), vmem_limit_bytes=VMEM_LIMIT),
        name="moe_final" if final_norm else "moe",
    )(x2, g, wr_hi, wr_lo, b_route, w13, w2, final_g)


def _prepare(p):
    w_in = p['w_in'].astype(BF16)
    depth = w_in.shape[0]
    pad = ROUTE_COLS - N_GROUPS - N_EXPERTS
    w_route = jnp.concatenate(
        [p['moe_w_group'], p['moe_w_expert'], jnp.zeros((depth, D_MODEL, pad), F32)], axis=-1)
    wr_hi = w_route.astype(BF16)
    grp = lambda w: w.transpose(0, 1, 3, 2, 4).reshape(depth, N_GROUPS, D_MODEL, GROUP_HIDDEN)
    tkc, wsinc, moutc, lamd, lamp, dskip = jax.vmap(_s5_params)(
        p['ssm_a_re'], p['ssm_a_im'], p['ssm_log_step'], p['ssm_b_re'], p['ssm_b_im'],
        p['ssm_c_re'], p['ssm_c_im'], p['ssm_d'])
    return dict(
        norm1_g=p['norm1_g'][:, None],
        w_main=w_in[:, :, :MAIN_COLS],
        w_gate=w_in[:, :, MAIN_COLS:],
        attn_sink=p['attn_sink'],
        conv_w=p['lru_conv_w'],
        conv_b=p['lru_conv_b'][:, None],
        lru_wg=jnp.concatenate([_block_diag(p['lru_w_r']), _block_diag(p['lru_w_i'])],
                               axis=-1).astype(BF16),
        lru_bg=jnp.concatenate([p['lru_b_r'], p['lru_b_i']], axis=-1)[:, :, None],
        lru_sp=jax.nn.softplus(-p['lru_lambda'])[:, :, None],
        tkc=tkc, wsinc=wsinc, moutc=moutc, lamd=lamd, lamp=lamp, dskip=dskip,
        w_branch=p['w_branch'].astype(BF16),
        w_glu=p['ssm_w_glu'].astype(BF16),
        b_glu=p['ssm_b_glu'][:, None],
        w_out=p['w_out'].astype(BF16),
        norm2_g=p['norm2_g'][:, None],
        wr_hi=wr_hi,
        wr_lo=(w_route - wr_hi.astype(F32)).astype(BF16),
        b_route=jnp.concatenate(
            [p['moe_b_group'], p['moe_b_expert'], jnp.zeros((depth, pad), F32)], axis=-1)[:, None],
        w13=jnp.concatenate([grp(p['moe_w1']), grp(p['moe_w3'])], axis=-1).astype(BF16),
        w2=p['moe_w2'].reshape(depth, N_GROUPS, GROUP_HIDDEN, D_MODEL).astype(BF16),
    )


def _layer(layer, x2, bsz, seq, w, final_g, final_norm):
    t = bsz * seq
    bw = BRANCH_WIDTH
    tm = _pick_tile(t, 512)
    qkv, xr, gr, u = _inproj(layer, x2, w['norm1_g'], w['w_main'], tm)
    o_attn = _attention(layer, qkv.reshape(bsz, seq, QKV_WIDTH), w['attn_sink']).reshape(t, bw)
    h_sum = _lru(layer, xr.reshape(bsz, seq, bw), w['conv_w'], w['conv_b'], w['lru_wg'],
                 w['lru_bg'], w['lru_sp']).reshape(t, bw)
    y_ssm = _s5(layer, u, w['tkc'], w['wsinc'], w['moutc'], w['lamd'], w['lamp'], w['dskip'],
                bsz, seq, _pick_tile(bsz, 2))
    x2 = _merge(layer, x2, w['norm1_g'], o_attn, h_sum, gr, y_ssm, w['w_gate'], w['w_branch'],
                w['w_glu'], w['b_glu'], w['w_out'], tm)
    return _moe(layer, x2, w['norm2_g'], w['wr_hi'], w['wr_lo'], w['b_route'], w['w13'], w['w2'],
                final_g, tm, final_norm)


_PARAM_KEYS = ('norm1_g', 'w_in', 'attn_sink', 'lru_conv_w', 'lru_conv_b', 'lru_w_r', 'lru_b_r',
               'lru_w_i', 'lru_b_i', 'lru_lambda', 'ssm_a_re', 'ssm_a_im', 'ssm_log_step',
               'ssm_b_re', 'ssm_b_im', 'ssm_c_re', 'ssm_c_im', 'ssm_d', 'ssm_w_glu', 'ssm_b_glu',
               'w_branch', 'w_out', 'norm2_g', 'moe_w_group', 'moe_b_group', 'moe_w_expert',
               'moe_b_expert', 'moe_w1', 'moe_w3', 'moe_w2')


def kernel(x, norm1_g, w_in, attn_sink, lru_conv_w, lru_conv_b, lru_w_r, lru_b_r, lru_w_i, lru_b_i,
           lru_lambda, ssm_a_re, ssm_a_im, ssm_log_step, ssm_b_re, ssm_b_im, ssm_c_re, ssm_c_im,
           ssm_d, ssm_w_glu, ssm_b_glu, w_branch, w_out, norm2_g, moe_w_group, moe_b_group,
           moe_w_expert, moe_b_expert, moe_w1, moe_w3, moe_w2, final_norm_g):
    params = dict(zip(_PARAM_KEYS, (
        norm1_g, w_in, attn_sink, lru_conv_w, lru_conv_b, lru_w_r, lru_b_r, lru_w_i, lru_b_i,
        lru_lambda, ssm_a_re, ssm_a_im, ssm_log_step, ssm_b_re, ssm_b_im, ssm_c_re, ssm_c_im,
        ssm_d, ssm_w_glu, ssm_b_glu, w_branch, w_out, norm2_g, moe_w_group, moe_b_group,
        moe_w_expert, moe_b_expert, moe_w1, moe_w3, moe_w2)))
    bsz, seq, _ = x.shape
    depth = norm1_g.shape[0]
    w = _prepare(params)
    x2 = x.reshape(bsz * seq, D_MODEL)
    for layer in range(depth):
        x2 = _layer(layer, x2, bsz, seq, w, final_norm_g[None], final_norm=(layer == depth - 1))
    return x2.reshape(bsz, seq, D_MODEL)
```

```python
import functools

import jax
import jax.numpy as jnp
import numpy as np
from jax import lax
from jax.experimental import pallas as pl
from jax.experimental.pallas import tpu as pltpu

F32 = jnp.float32
BF16 = jnp.bfloat16

D_MODEL = 1024
N_Q_HEADS = 8
N_KV_HEADS = 2
HEAD_DIM = 64
WINDOW = 128
BLOCK = 128
BRANCH_WIDTH = 512
KV_WIDTH = N_KV_HEADS * HEAD_DIM
LRU_BLOCKS = 8
LRU_C = 8.0
SSM_GROUP = 16
SSM_GROUPS = BRANCH_WIDTH // SSM_GROUP
SSM_STATE = 64
N_BRANCH = 3
N_GROUPS = 4
EXPERTS_PER_GROUP = 4
N_EXPERTS = N_GROUPS * EXPERTS_PER_GROUP
D_EXPERT = 256
GROUP_HIDDEN = EXPERTS_PER_GROUP * D_EXPERT
NORM_EPS = 1e-6
MASK_VALUE = -1e30

LANES = 128
SUBLANES = 8
QKV_WIDTH = BRANCH_WIDTH + 2 * KV_WIDTH
MAIN_COLS = QKV_WIDTH + 3 * BRANCH_WIDTH
HEADS_PER_KV = N_Q_HEADS // N_KV_HEADS
SSM_CHUNK = 16
PAIR_CH = 2 * SSM_GROUP
SSM_PAIRS = SSM_GROUPS // 2
TILE_PAIRS = LANES // PAIR_CH
PAIR_IN = SSM_CHUNK * PAIR_CH
PAIR_STATE = 2 * SSM_STATE
SSM_TILE = 512
N_COMP = 4
SCAN_STEPS = (1, 2, 4)
ROUTE_COLS = LANES
CONV_HALO = 16
VMEM_LIMIT = 56 * 1024 * 1024


def _rms(x, g):
    ms = jnp.mean(x * x, axis=-1, keepdims=True)
    return x * lax.rsqrt(ms + NORM_EPS) * g


def _sigmoid(x):
    return 0.5 * jnp.tanh(0.5 * x) + 0.5


def _dot(a, b):
    return jnp.dot(a, b, preferred_element_type=F32)


def _dot_nt(a, b):
    return lax.dot_general(a, b, (((1,), (1,)), ((), ())), preferred_element_type=F32)


def _const_spec(shape):
    nd = len(shape)
    return pl.BlockSpec(shape, lambda *_: (0,) * nd)


def _layer_spec(layer, shape, **kw):
    nd = len(shape)
    return pl.BlockSpec((None,) + tuple(shape), lambda *_: (layer,) + (0,) * nd, **kw)


def _row_in_tile(rows):
    return lax.broadcasted_iota(jnp.int32, (rows, 1), 0) & (SUBLANES - 1)


def _pick_tile(n, want):
    t = min(n, want)
    while n % t:
        t //= 2
    return t


def _chunk_perm():
    t = np.arange(SSM_TILE)
    dst = (t % SSM_CHUNK) * (SSM_TILE // SSM_CHUNK) + t // SSM_CHUNK
    perm = np.zeros((SSM_TILE, SSM_TILE), np.float32)
    perm[dst, t] = 1.0
    return perm


def _inproj_kernel(x_ref, g_ref, w_ref, perm_ref, qkv_ref, xr_ref, gr_ref, u_ref):
    hn = _rms(x_ref[...], g_ref[...]).astype(BF16)
    bw = BRANCH_WIDTH
    qkv_ref[...] = _dot(hn, w_ref[:, 0:QKV_WIDTH]).astype(BF16)
    xr_ref[...] = _dot(hn, w_ref[:, QKV_WIDTH:QKV_WIDTH + bw]).astype(BF16)
    gr_ref[...] = _dot(hn, w_ref[:, QKV_WIDTH + bw:QKV_WIDTH + 2 * bw]).astype(BF16)
    u = _dot(hn, w_ref[:, QKV_WIDTH + 2 * bw:MAIN_COLS]).astype(BF16)
    u_ref[...] = _dot(perm_ref[...], u).astype(BF16)


def _inproj(layer, x2, g, w_main, perm):
    t = x2.shape[0]
    tm = SSM_TILE
    row = lambda w: pl.BlockSpec((tm, w), lambda i: (i, 0))
    return pl.pallas_call(
        _inproj_kernel,
        grid=(t // tm,),
        in_specs=[row(D_MODEL), _layer_spec(layer, (1, D_MODEL)),
                  _layer_spec(layer, (D_MODEL, MAIN_COLS)), _const_spec((tm, tm))],
        out_specs=[row(QKV_WIDTH), row(BRANCH_WIDTH), row(BRANCH_WIDTH), row(BRANCH_WIDTH)],
        out_shape=[jax.ShapeDtypeStruct((t, QKV_WIDTH), BF16)]
        + [jax.ShapeDtypeStruct((t, BRANCH_WIDTH), BF16)] * 3,
        compiler_params=pltpu.CompilerParams(
            dimension_semantics=("parallel",), vmem_limit_bytes=VMEM_LIMIT),
        name="inproj",
    )(x2, g, w_main, perm)


def _attn_bias():
    qi = np.arange(BLOCK)[:, None]
    kj = np.arange(3 * BLOCK)[None, :]
    dist = np.abs(qi - kj + BLOCK)
    out = np.empty((N_KV_HEADS, HEADS_PER_KV * BLOCK, 3 * BLOCK), np.float32)
    for head in range(N_Q_HEADS):
        h, g = divmod(head, HEADS_PER_KV)
        slope = 2.0 ** (-8.0 * (head + 1) / N_Q_HEADS)
        out[h, g * BLOCK:(g + 1) * BLOCK] = np.where(dist <= WINDOW, -slope * dist, 1.0)
    return out


def _attn_kernel(sink_ref, bias_ref, q_ref, kvp_ref, kvc_ref, kvn_ref, o_ref, *, nb, layer):
    n = pl.program_id(1)
    kv = jnp.concatenate([kvp_ref[...], kvc_ref[...], kvn_ref[...]], axis=0).astype(F32)
    k = kv[:, :KV_WIDTH]
    v = kv[:, KV_WIDTH:]
    lane = lax.broadcasted_iota(jnp.int32, (1, KV_WIDTH), 1)
    lo = lane < HEAD_DIM

    rows = HEADS_PER_KV * BLOCK
    kj = lax.broadcasted_iota(jnp.int32, (1, 3 * BLOCK), 1)
    in_seq = ((kj >= BLOCK) | (n > 0)) & ((kj < 2 * BLOCK) | (n < nb - 1))
    head_in_kv = lax.broadcasted_iota(jnp.int32, (rows, 1), 0) // BLOCK

    scale = HEAD_DIM ** -0.5
    for h in range(N_KV_HEADS):
        own = lo if h == 0 else jnp.logical_not(lo)
        km = jnp.where(own, k, 0.0)
        vm = jnp.where(own, v, 0.0)
        k2 = ((km + pltpu.roll(km, HEAD_DIM, 1)) * scale).astype(BF16)
        v2 = (vm + pltpu.roll(vm, HEAD_DIM, 1)).astype(BF16)
        parts = []
        sink = jnp.zeros((rows, 1), F32)
        for g in range(HEADS_PER_KV):
            head = h * HEADS_PER_KV + g
            pair = head // 2
            qp = q_ref[:, pair * KV_WIDTH:(pair + 1) * KV_WIDTH]
            keep = lo if head % 2 == 0 else jnp.logical_not(lo)
            parts.append(jnp.where(keep, qp, jnp.zeros_like(qp)))
            sink = jnp.where(head_in_kv == g, sink_ref[layer, head], sink)
        s = _dot_nt(jnp.concatenate(parts, axis=0), k2)
        bias = bias_ref[h]
        s = jnp.where((bias <= 0.0) & in_seq, s + bias, MASK_VALUE)
        m = jnp.maximum(jnp.max(s, axis=-1, keepdims=True), sink)
        p = jnp.exp(s - m)
        denom = jnp.sum(p, axis=-1, keepdims=True) + jnp.exp(sink - m)
        r = _dot(p.astype(BF16), v2) / denom
        for j in range(HEADS_PER_KV // 2):
            even = r[(2 * j) * BLOCK:(2 * j + 1) * BLOCK]
            odd = r[(2 * j + 1) * BLOCK:(2 * j + 2) * BLOCK]
            pair = h * (HEADS_PER_KV // 2) + j
            o_ref[:, pair * KV_WIDTH:(pair + 1) * KV_WIDTH] = jnp.where(lo, even, odd).astype(BF16)


def _attention(layer, qkv3, sink):
    b, l, _ = qkv3.shape
    nb = l // BLOCK
    kv_col = BRANCH_WIDTH // (2 * KV_WIDTH)
    kv_spec = lambda f: pl.BlockSpec((None, BLOCK, 2 * KV_WIDTH), lambda bi, n: (bi, f(n), kv_col))
    return pl.pallas_call(
        functools.partial(_attn_kernel, nb=nb, layer=layer),
        grid=(b, nb),
        in_specs=[
            pl.BlockSpec(memory_space=pltpu.SMEM),
            _const_spec((N_KV_HEADS, HEADS_PER_KV * BLOCK, 3 * BLOCK)),
            pl.BlockSpec((None, BLOCK, BRANCH_WIDTH), lambda bi, n: (bi, n, 0)),
            kv_spec(lambda n: jnp.maximum(n - 1, 0)),
            kv_spec(lambda n: n),
            kv_spec(lambda n: jnp.minimum(n + 1, nb - 1)),
        ],
        out_specs=pl.BlockSpec((None, BLOCK, BRANCH_WIDTH), lambda bi, n: (bi, n, 0)),
        out_shape=jax.ShapeDtypeStruct((b, l, BRANCH_WIDTH), BF16),
        compiler_params=pltpu.CompilerParams(
            dimension_semantics=("parallel", "parallel"), vmem_limit_bytes=VMEM_LIMIT),
        name="window_attn",
    )(sink, jnp.asarray(_attn_bias()), qkv3, qkv3, qkv3, qkv3)


def _tile_scan(a, u, reverse):
    rows = a.shape[0]
    rit = _row_in_tile(rows)
    for d in SCAN_STEPS:
        if reverse:
            take = rit < SUBLANES - d
            shift = rows - d
        else:
            take = rit >= d
            shift = d
        a_sh = jnp.where(take, pltpu.roll(a, shift, 0), 1.0)
        u_sh = jnp.where(take, pltpu.roll(u, shift, 0), 0.0)
        u = u + a * u_sh
        a = a * a_sh
    return a, u


def _lru_kernel(x_ref, cw_ref, cb_ref, wg_ref, bg_ref, sp_ref, o_ref, xe_s, a_s, u_s, hf_s,
                *, seq, rc):
    phase = pl.program_id(1)
    bw = BRANCH_WIDTH
    n_tiles = seq // SUBLANES

    zeros = jnp.zeros((CONV_HALO, bw), F32)
    xe_s[0:CONV_HALO, :] = zeros
    xe_s[CONV_HALO + seq:CONV_HALO + seq + CONV_HALO, :] = zeros
    xe_s[CONV_HALO:CONV_HALO + seq, :] = x_ref[...].astype(F32)
    cw = cw_ref[...]

    def build(reverse):
        for c in range(seq // rc):
            base = CONV_HALO + c * rc
            xc = cb_ref[...]
            for tap in range(4):
                xc = xc + cw[tap:tap + 1] * xe_s[base + tap - 2:base + tap - 2 + rc, :]
            gates = _sigmoid(_dot(xc.astype(BF16), wg_ref[...]) + bg_ref[...])
            log_a = -LRU_C * gates[:, :bw] * sp_ref[...]
            a = jnp.exp(log_a)
            u = jnp.sqrt(1.0 - a * a) * (gates[:, bw:] * xc)
            a, u = _tile_scan(a, u, reverse)
            a_s[c * rc:(c + 1) * rc, :] = a
            u_s[c * rc:(c + 1) * rc, :] = u

    @pl.when(phase == 0)
    def _():
        build(False)

        def step(k, carry):
            r0 = pl.multiple_of(k * SUBLANES, SUBLANES)
            h = u_s[pl.ds(r0, SUBLANES), :] + a_s[pl.ds(r0, SUBLANES), :] * carry
            hf_s[pl.ds(r0, SUBLANES), :] = h
            return jnp.broadcast_to(h[SUBLANES - 1:SUBLANES, :], (SUBLANES, bw))

        lax.fori_loop(0, n_tiles, step, jnp.zeros((SUBLANES, bw), F32), unroll=4)

    @pl.when(phase == 1)
    def _():
        build(True)

        def step(k, carry):
            r0 = pl.multiple_of((n_tiles - 1 - k) * SUBLANES, SUBLANES)
            h = u_s[pl.ds(r0, SUBLANES), :] + a_s[pl.ds(r0, SUBLANES), :] * carry
            o_ref[pl.ds(r0, SUBLANES), :] = h + hf_s[pl.ds(r0, SUBLANES), :]
            return jnp.broadcast_to(h[0:1, :], (SUBLANES, bw))

        lax.fori_loop(0, n_tiles, step, jnp.zeros((SUBLANES, bw), F32), unroll=4)


def _lru(layer, xr3, conv_w, conv_b, wg, bg, sp):
    b, l, w = xr3.shape
    seq_spec = pl.BlockSpec((None, l, w), lambda bi, ph: (bi, 0, 0))
    per_dir = lambda shape: pl.BlockSpec(
        (None, None) + shape, lambda bi, ph: (layer, ph) + (0,) * len(shape))
    return pl.pallas_call(
        functools.partial(_lru_kernel, seq=l, rc=_pick_tile(l, 512)),
        grid=(b, 2),
        in_specs=[seq_spec, _layer_spec(layer, (4, w)), _layer_spec(layer, (1, w)),
                  per_dir((w, 2 * w)), per_dir((1, 2 * w)), per_dir((1, w))],
        out_specs=seq_spec,
        out_shape=jax.ShapeDtypeStruct((b, l, w), F32),
        scratch_shapes=[pltpu.VMEM((l + 2 * CONV_HALO, w), F32), pltpu.VMEM((l, w), F32),
                        pltpu.VMEM((l, w), F32), pltpu.VMEM((l, w), F32)],
        compiler_params=pltpu.CompilerParams(
            dimension_semantics=("parallel", "arbitrary"), vmem_limit_bytes=VMEM_LIMIT),
        name="rg_lru",
    )(xr3, conv_w, conv_b, wg, bg, sp)


def _block_diag(wb):
    nbk, bd = wb.shape[-3], wb.shape[-2]
    eye = jnp.eye(nbk, dtype=wb.dtype)
    out = jnp.einsum('...hij,hk->...hikj', wb, eye)
    return out.reshape(wb.shape[:-3] + (nbk * bd, nbk * bd))


def _expand_pair(compact, out_cols, hi_shift_out, hi_shift_in, lo_mask, row_shift, col_shift):
    cc = compact.shape[1]
    i = lax.broadcasted_iota(jnp.int32, (cc, out_cols), 0)
    j = lax.broadcasted_iota(jnp.int32, (cc, out_cols), 1)
    place = ((j >> hi_shift_out) == (i >> hi_shift_in)) & ((j & lo_mask) == (i & lo_mask))
    spread = _dot(compact, place.astype(BF16))
    rg = (lax.broadcasted_iota(jnp.int32, (compact.shape[0], 1), 0) >> row_shift) & 1
    cg = (lax.broadcasted_iota(jnp.int32, (1, out_cols), 1) >> col_shift) & 1
    return jnp.where(rg == cg, spread, 0.0).astype(BF16)


def _lane_piece_mask(piece):
    lane = lax.broadcasted_iota(jnp.int32, (1, LANES), 1)
    return (lane // PAIR_CH) == piece


def _s5_kernel(u_ref, tkc_ref, wsinc_ref, moutc_ref, lamd_ref, lamp_ref, d_ref, y_ref,
               x_s, y_s, g_s, tk_s, wsin_s, mout_s, *, nch, nbat):
    q = SSM_CHUNK
    rows = nbat * nch
    st = PAIR_STATE
    c_bits = SSM_GROUP.bit_length() - 1
    n_bits = SSM_STATE.bit_length() - 1
    spt = LANES // PAIR_CH

    @pl.when(pl.program_id(1) == 0)
    def _():
        for pp in range(TILE_PAIRS):
            tk_s[pp] = _expand_pair(tkc_ref[pp], PAIR_IN, c_bits + 1, c_bits, SSM_GROUP - 1,
                                    c_bits, c_bits)
            wsin_s[pp] = _expand_pair(wsinc_ref[pp], N_COMP * st, n_bits + 1, n_bits,
                                      SSM_STATE - 1, c_bits, n_bits)
            mout_s[pp] = _expand_pair(moutc_ref[pp], PAIR_IN, c_bits + 1, c_bits, SSM_GROUP - 1,
                                      n_bits, c_bits)

    step_rows = lambda s: u_ref[:, s].astype(F32).reshape(rows, LANES)
    masks = [_lane_piece_mask(j) for j in range(spt)]
    for kt in range(q // spt):
        rolled = []
        for j in range(spt):
            xs = step_rows(kt * spt + j)
            rolled.append([xs if sh == 0 else pltpu.roll(xs, sh * PAIR_CH, 1) for sh in range(spt)])
        for pp in range(TILE_PAIRS):
            tile = None
            for j in range(spt):
                src = rolled[j][(j - pp) % spt]
                tile = src if tile is None else jnp.where(masks[j], src, tile)
            x_s[pp, :, kt * LANES:(kt + 1) * LANES] = tile.astype(BF16)

    for pp in range(TILE_PAIRS):
        _s5_pair(x_s.at[pp], y_s.at[pp], g_s, tk_s.at[pp], wsin_s.at[pp], mout_s.at[pp],
                 lamd_ref.at[pp], lamp_ref.at[pp], nch=nch, nbat=nbat)

    for kt in range(q // spt):
        rolled = []
        for pp in range(TILE_PAIRS):
            yt = y_s[pp, :, kt * LANES:(kt + 1) * LANES]
            rolled.append([yt if sh == 0 else pltpu.roll(yt, sh * PAIR_CH, 1) for sh in range(spt)])
        for j in range(spt):
            tile = None
            for pp in range(TILE_PAIRS):
                src = rolled[pp][(pp - j) % spt]
                tile = src if tile is None else jnp.where(masks[pp], src, tile)
            s = kt * spt + j
            out = tile + d_ref[...] * step_rows(s)
            y_ref[:, s] = out.reshape(y_ref.shape[0], y_ref.shape[2], LANES).astype(y_ref.dtype)


def _s5_pair(x_ref, y_ref, g_s, tk_ref, wsin_ref, mout_ref, lamd_ref, lamp_ref, *, nch, nbat):
    rows = nbat * nch
    st = PAIR_STATE
    n_tiles = nch // SUBLANES
    x = x_ref[...]

    sloc = _dot(x, wsin_ref[...])
    chunk = lax.broadcasted_iota(jnp.int32, (rows, 1), 0) & (nch - 1)
    rit = _row_in_tile(rows)
    comps = []
    for comp in range(N_COMP):
        reverse = comp >= 2
        piece = sloc[:, comp * st:(comp + 1) * st]
        if reverse:
            piece = jnp.where(chunk == nch - 1, 0.0, pltpu.roll(piece, rows - 1, 0))
        else:
            piece = jnp.where(chunk == 0, 0.0, pltpu.roll(piece, 1, 0))
        comps.append(piece)

    for di, d in enumerate(SCAN_STEPS):
        for base in (0, 2):
            reverse = base == 2
            lr = lamd_ref[di * N_COMP + base:di * N_COMP + base + 1, :]
            li = lamd_ref[di * N_COMP + base + 1:di * N_COMP + base + 2, :]
            if reverse:
                take = rit < SUBLANES - d
                shift = rows - d
            else:
                take = rit >= d
                shift = d
            gr, gi = comps[base], comps[base + 1]
            sr = jnp.where(take, pltpu.roll(gr, shift, 0), 0.0)
            si = jnp.where(take, pltpu.roll(gi, shift, 0), 0.0)
            comps[base] = gr + lr * sr - li * si
            comps[base + 1] = gi + lr * si + li * sr
    for comp in range(N_COMP):
        g_s[:, comp * st:(comp + 1) * st] = comps[comp]

    pf_r, pf_i = lamp_ref[0:SUBLANES, :], lamp_ref[SUBLANES:2 * SUBLANES, :]
    pb_r, pb_i = lamp_ref[2 * SUBLANES:3 * SUBLANES, :], lamp_ref[3 * SUBLANES:4 * SUBLANES, :]

    def step(k, carry):
        out = []
        for bi in range(nbat):
            cfr, cfi, cbr, cbi = carry[4 * bi:4 * bi + 4]
            rf = pl.multiple_of(bi * nch + k * SUBLANES, SUBLANES)
            rb = pl.multiple_of(bi * nch + (n_tiles - 1 - k) * SUBLANES, SUBLANES)
            gfr = g_s[pl.ds(rf, SUBLANES), 0:st] + pf_r * cfr - pf_i * cfi
            gfi = g_s[pl.ds(rf, SUBLANES), st:2 * st] + pf_r * cfi + pf_i * cfr
            gbr = g_s[pl.ds(rb, SUBLANES), 2 * st:3 * st] + pb_r * cbr - pb_i * cbi
            gbi = g_s[pl.ds(rb, SUBLANES), 3 * st:4 * st] + pb_r * cbi + pb_i * cbr
            g_s[pl.ds(rf, SUBLANES), 0:st] = gfr
            g_s[pl.ds(rf, SUBLANES), st:2 * st] = gfi
            g_s[pl.ds(rb, SUBLANES), 2 * st:3 * st] = gbr
            g_s[pl.ds(rb, SUBLANES), 3 * st:4 * st] = gbi
            last = lambda v: jnp.broadcast_to(v[SUBLANES - 1:SUBLANES, :], (SUBLANES, st))
            first = lambda v: jnp.broadcast_to(v[0:1, :], (SUBLANES, st))
            out += [last(gfr), last(gfi), first(gbr), first(gbi)]
        return tuple(out)

    z = jnp.zeros((SUBLANES, st), F32)
    lax.fori_loop(0, n_tiles, step, (z,) * (4 * nbat))

    y_ref[...] = _dot(x, tk_ref[...]) + _dot(g_s[...].astype(BF16), mout_ref[...])


def _s5(layer, u2, tkc, wsinc, moutc, lamd, lamp, dskip, bsz, seq, nbat):
    q = SSM_CHUNK
    nch = seq // q
    assert nch & (nch - 1) == 0 and nch % SUBLANES == 0, "chunks per sequence must be a power of two"
    rows = nbat * nch
    assert (nbat * seq) % SSM_TILE == 0
    nt = nbat * seq // SSM_TILE
    cpt = SSM_TILE // q
    st = PAIR_STATE
    tp = TILE_PAIRS
    tile_spec = lambda shape: pl.BlockSpec(
        (None, None) + shape, lambda o, i: (layer, o) + (0,) * len(shape))
    io_spec = pl.BlockSpec((nt, q, cpt, LANES), lambda o, i: (i, 0, 0, o))
    u4 = u2.reshape(u2.shape[0] // SSM_TILE, q, cpt, BRANCH_WIDTH)
    return pl.pallas_call(
        functools.partial(_s5_kernel, nch=nch, nbat=nbat),
        grid=(BRANCH_WIDTH // LANES, bsz // nbat),
        in_specs=[
            io_spec,
            tile_spec((tp, PAIR_IN, q * SSM_GROUP)),
            tile_spec((tp, PAIR_IN, N_COMP * SSM_STATE)),
            tile_spec((tp, N_COMP * st, q * SSM_GROUP)),
            tile_spec((tp, len(SCAN_STEPS) * N_COMP, st)),
            tile_spec((tp, N_COMP * SUBLANES, st)),
            tile_spec((1, LANES)),
        ],
        out_specs=io_spec,
        out_shape=jax.ShapeDtypeStruct(u4.shape, BF16),
        scratch_shapes=[pltpu.VMEM((tp, rows, PAIR_IN), BF16),
                        pltpu.VMEM((tp, rows, PAIR_IN), F32),
                        pltpu.VMEM((rows, N_COMP * st), F32),
                        pltpu.VMEM((tp, PAIR_IN, PAIR_IN), BF16),
                        pltpu.VMEM((tp, PAIR_IN, N_COMP * st), BF16),
                        pltpu.VMEM((tp, N_COMP * st, PAIR_IN), BF16)],
        compiler_params=pltpu.CompilerParams(
            dimension_semantics=("parallel", "arbitrary"), vmem_limit_bytes=VMEM_LIMIT),
        name="s5_chunked",
    )(u4, tkc, wsinc, moutc, lamd, lamp, dskip).reshape(u2.shape)


def _s5_params(a_re, a_im, log_step, b_re, b_im, c_re, c_im, d_skip):
    q = SSM_CHUNK
    g, n, c = SSM_GROUPS, SSM_STATE, SSM_GROUP
    no, og = SSM_PAIRS, 2
    step = jnp.exp(log_step)[..., None]
    dr = a_re * step
    di = a_im * step
    mag = jnp.exp(dr)
    abar_r = mag * jnp.cos(di)
    abar_i = mag * jnp.sin(di)
    den = a_re * a_re + a_im * a_im
    nr = abar_r - 1.0
    sr = (nr * a_re + abar_i * a_im) / den
    si = (abar_i * a_re - nr * a_im) / den
    bb_r = sr[..., None] * b_re - si[..., None] * b_im
    bb_i = sr[..., None] * b_im + si[..., None] * b_re

    def powers(kk):
        kk = jnp.asarray(kk, F32)
        pmag = jnp.exp(dr[..., None] * kk)
        return pmag * jnp.cos(di[..., None] * kk), pmag * jnp.sin(di[..., None] * kk)

    pr, pi = powers(jnp.arange(q + 1))
    cr = c_re[None, :, :, :, None]
    ci = c_im[None, :, :, :, None]
    cpr = cr * pr[:, :, None] - ci * pi[:, :, None]
    cpi = cr * pi[:, :, None] + ci * pr[:, :, None]
    kern = (jnp.einsum('dgcnk,dgne->dgkce', cpr, bb_r)
            - jnp.einsum('dgcnk,dgne->dgkce', cpi, bb_i))[:, :, :q]
    lag = jnp.arange(q)[None, :] - jnp.arange(q)[:, None]
    tk_f = jnp.where((lag >= 0)[None, :, :, None, None],
                     kern[0][:, jnp.clip(lag, 0, q - 1)], 0.0)
    tk_b = jnp.where((lag <= 0)[None, :, :, None, None],
                     kern[1][:, jnp.clip(-lag, 0, q - 1)], 0.0)
    tkc = (tk_f + tk_b).reshape(no, og, q, q, c, c).transpose(0, 2, 1, 5, 3, 4).reshape(
        no, PAIR_IN, q * c)

    def state_in(d, pw):
        wr = pr[d][:, :, pw][..., None] * bb_r[d][:, :, None, :] \
            - pi[d][:, :, pw][..., None] * bb_i[d][:, :, None, :]
        wi = pr[d][:, :, pw][..., None] * bb_i[d][:, :, None, :] \
            + pi[d][:, :, pw][..., None] * bb_r[d][:, :, None, :]
        lay = lambda w_: w_.reshape(no, og, n, q, c).transpose(0, 3, 1, 4, 2).reshape(no, PAIR_IN, n)
        return lay(wr), lay(wi)

    wsinc = jnp.concatenate(state_in(0, q - 1 - jnp.arange(q)) + state_in(1, jnp.arange(q)), axis=-1)

    def read_out(d, pw):
        lay = lambda w_: w_.reshape(no, og, c, n, q).transpose(0, 1, 3, 4, 2).reshape(
            no, PAIR_STATE, q * c)
        return lay(cpr[d][..., pw]), lay(-cpi[d][..., pw])

    moutc = jnp.concatenate(read_out(0, jnp.arange(q) + 1) + read_out(1, q - jnp.arange(q)), axis=1)

    def scan_rows(kk):
        sr_, si_ = powers(q * jnp.asarray(kk))
        rows = jnp.stack([sr_[0], si_[0], sr_[1], si_[1]], axis=0)
        return rows.reshape(N_COMP, no, og, n, len(kk))

    lamd = scan_rows(SCAN_STEPS).transpose(1, 4, 0, 2, 3).reshape(
        no, len(SCAN_STEPS) * N_COMP, PAIR_STATE)
    fwd = scan_rows(list(range(1, SUBLANES + 1)))[:2]
    bwd = scan_rows(list(range(SUBLANES, 0, -1)))[2:]
    lamp = jnp.concatenate([fwd, bwd], axis=0).transpose(1, 0, 4, 2, 3).reshape(
        no, N_COMP * SUBLANES, PAIR_STATE)
    dskip = d_skip.reshape(BRANCH_WIDTH // LANES, 1, LANES)
    by_tile = lambda w_: w_.reshape((BRANCH_WIDTH // LANES, TILE_PAIRS) + w_.shape[1:])
    return (by_tile(tkc.astype(BF16)), by_tile(wsinc.astype(BF16)), by_tile(moutc.astype(BF16)),
            by_tile(lamd), by_tile(lamp), dskip)


def _merge_kernel(x_ref, g_ref, oa_ref, hs_ref, gr_ref, ys_ref, unperm_ref, wgate_ref, wb_ref,
                  wglu_ref, bglu_ref, wout_ref, o_ref):
    x = x_ref[...]
    hn = _rms(x, g_ref[...]).astype(BF16)
    bw = BRANCH_WIDTH
    o_lru = (jax.nn.gelu(gr_ref[...].astype(F32)) * hs_ref[...]).astype(BF16)
    zg = _dot(jax.nn.gelu(ys_ref[...].astype(F32)).astype(BF16), wglu_ref[...]) + bglu_ref[...]
    o_ssm = (zg[:, :bw] * _sigmoid(zg[:, bw:])).astype(BF16)
    o_ssm = _dot(unperm_ref[...], o_ssm).astype(BF16)
    mixed = None
    for kb, ob in enumerate((oa_ref[...], o_lru, o_ssm)):
        gate = _sigmoid(_dot(hn, wgate_ref[:, kb * D_MODEL:(kb + 1) * D_MODEL]))
        term = _dot(ob, wb_ref[kb]) * gate
        mixed = term if mixed is None else mixed + term
    o_ref[...] = x + _dot(mixed.astype(BF16), wout_ref[...])


def _merge(layer, x2, g, o_attn, h_sum, gr, y_ssm, unperm, w_gate, w_branch, w_glu, b_glu, w_out):
    t = x2.shape[0]
    tm = SSM_TILE
    row = lambda w: pl.BlockSpec((tm, w), lambda i: (i, 0))
    return pl.pallas_call(
        _merge_kernel,
        grid=(t // tm,),
        in_specs=[row(D_MODEL), _layer_spec(layer, (1, D_MODEL)), row(BRANCH_WIDTH),
                  row(BRANCH_WIDTH), row(BRANCH_WIDTH), row(BRANCH_WIDTH), _const_spec((tm, tm)),
                  _layer_spec(layer, (D_MODEL, N_BRANCH * D_MODEL)),
                  _layer_spec(layer, (N_BRANCH, BRANCH_WIDTH, D_MODEL)),
                  _layer_spec(layer, (BRANCH_WIDTH, 2 * BRANCH_WIDTH)),
                  _layer_spec(layer, (1, 2 * BRANCH_WIDTH)),
                  _layer_spec(layer, (D_MODEL, D_MODEL))],
        out_specs=row(D_MODEL),
        out_shape=jax.ShapeDtypeStruct((t, D_MODEL), F32),
        compiler_params=pltpu.CompilerParams(
            dimension_semantics=("parallel",), vmem_limit_bytes=VMEM_LIMIT),
        name="merge",
    )(x2, g, o_attn, h_sum, gr, y_ssm, unperm, w_gate, w_branch, w_glu, b_glu, w_out)


def _route(logits):
    lane = lax.broadcasted_iota(jnp.int32, logits.shape, 1).astype(F32)
    neg = -jnp.inf
    first = lambda mask: jnp.min(jnp.where(mask, lane, float(ROUTE_COLS)), axis=-1, keepdims=True)
    gmask = lane < N_GROUPS
    gmax = jnp.max(jnp.where(gmask, logits, neg), axis=-1, keepdims=True)
    gidx = first(gmask & (logits == gmax))
    gsum = jnp.sum(jnp.where(gmask, jnp.exp(logits - gmax), 0.0), axis=-1, keepdims=True)
    g_w = 1.0 / gsum
    e_lo = N_GROUPS + EXPERTS_PER_GROUP * gidx
    emask = (lane >= e_lo) & (lane < e_lo + EXPERTS_PER_GROUP)
    v1 = jnp.max(jnp.where(emask, logits, neg), axis=-1, keepdims=True)
    i1 = first(emask & (logits == v1))
    emask2 = emask & (lane != i1)
    v2 = jnp.max(jnp.where(emask2, logits, neg), axis=-1, keepdims=True)
    i2 = first(emask2 & (logits == v2))
    t = jnp.exp(v2 - v1)
    w1 = 1.0 / (1.0 + t)
    w2 = t * w1
    return jnp.where(lane == i1, w1 * g_w, jnp.where(lane == i2, w2 * g_w, 0.0))


def _moe_kernel(x_ref, g_ref, wrh_ref, wrl_ref, br_ref, w13_ref, w2_ref, fg_ref, o_ref,
                *, final_norm):
    x = x_ref[...]
    tm = x.shape[0]
    hn = _rms(x, g_ref[...])
    hb = hn.astype(BF16)
    lo = (hn - hb.astype(F32)).astype(BF16)
    logits = (_dot(hb, wrh_ref[...]) + _dot(hb, wrl_ref[...]) + _dot(lo, wrh_ref[...])
              + br_ref[...])
    cw = _route(logits)
    lane = lax.broadcasted_iota(jnp.int32, cw.shape, 1)
    y = None
    for grp in range(N_GROUPS):
        h = _dot(hb, w13_ref[grp])
        cols = []
        for e in range(EXPERTS_PER_GROUP):
            col = N_GROUPS + grp * EXPERTS_PER_GROUP + e
            wcol = jnp.sum(jnp.where(lane == col, cw, 0.0), axis=-1, keepdims=True)
            cols.append(jnp.broadcast_to(wcol, (tm, D_EXPERT)))
        h1 = h[:, :GROUP_HIDDEN]
        hg = (h1 * _sigmoid(h1) * h[:, GROUP_HIDDEN:] * jnp.concatenate(cols, axis=1)).astype(BF16)
        term = _dot(hg, w2_ref[grp])
        y = term if y is None else y + term
    y = x + y
    if final_norm:
        y = _rms(y, fg_ref[...])
    o_ref[...] = y


def _moe(layer, x2, g, wr_hi, wr_lo, b_route, w13, w2, final_g, tm, final_norm):
    t = x2.shape[0]
    row = pl.BlockSpec((tm, D_MODEL), lambda i: (i, 0))
    once = pl.Buffered(1)
    return pl.pallas_call(
        functools.partial(_moe_kernel, final_norm=final_norm),
        grid=(t // tm,),
        in_specs=[row,
                  _layer_spec(layer, (1, D_MODEL)),
                  _layer_spec(layer, (D_MODEL, ROUTE_COLS)),
                  _layer_spec(layer, (D_MODEL, ROUTE_COLS)),
                  _layer_spec(layer, (1, ROUTE_COLS)),
                  _layer_spec(layer, (N_GROUPS, D_MODEL, 2 * GROUP_HIDDEN), pipeline_mode=once),
                  _layer_spec(layer, (N_GROUPS, GROUP_HIDDEN, D_MODEL), pipeline_mode=once),
                  _const_spec((1, D_MODEL))],
        out_specs=row,
        out_shape=jax.ShapeDtypeStruct((t, D_MODEL), F32),
        compiler_params=pltpu.CompilerParams(
            dimension_semantics=("parallel",), vmem_limit_bytes=VMEM_LIMIT),
        name="moe_final" if final_norm else "moe",
    )(x2, g, wr_hi, wr_lo, b_route, w13, w2, final_g)


def _prepare(p):
    w_in = p['w_in']
    depth = w_in.shape[0]
    pad = ROUTE_COLS - N_GROUPS - N_EXPERTS
    w_route = jnp.concatenate(
        [p['moe_w_group'], p['moe_w_expert'], jnp.zeros((depth, D_MODEL, pad), F32)], axis=-1)
    wr_hi = w_route.astype(BF16)
    grp = lambda w: w.transpose(0, 1, 3, 2, 4).reshape(depth, N_GROUPS, D_MODEL, GROUP_HIDDEN)
    tkc, wsinc, moutc, lamd, lamp, dskip = jax.vmap(_s5_params)(
        p['ssm_a_re'], p['ssm_a_im'], p['ssm_log_step'], p['ssm_b_re'], p['ssm_b_im'],
        p['ssm_c_re'], p['ssm_c_im'], p['ssm_d'])
    return dict(
        norm1_g=p['norm1_g'][:, None],
        w_main=w_in[:, :, :MAIN_COLS].astype(BF16),
        w_gate=w_in[:, :, MAIN_COLS:].astype(BF16),
        attn_sink=p['attn_sink'],
        conv_w=p['lru_conv_w'],
        conv_b=p['lru_conv_b'][:, None],
        lru_wg=jnp.concatenate([_block_diag(p['lru_w_r']), _block_diag(p['lru_w_i'])],
                               axis=-1).astype(BF16),
        lru_bg=jnp.concatenate([p['lru_b_r'], p['lru_b_i']], axis=-1)[:, :, None],
        lru_sp=jax.nn.softplus(-p['lru_lambda'])[:, :, None],
        tkc=tkc, wsinc=wsinc, moutc=moutc, lamd=lamd, lamp=lamp, dskip=dskip,
        w_branch=p['w_branch'].astype(BF16),
        w_glu=p['ssm_w_glu'].astype(BF16),
        b_glu=p['ssm_b_glu'][:, None],
        w_out=p['w_out'].astype(BF16),
        norm2_g=p['norm2_g'][:, None],
        wr_hi=wr_hi,
        wr_lo=(w_route - wr_hi.astype(F32)).astype(BF16),
        b_route=jnp.concatenate(
            [p['moe_b_group'], p['moe_b_expert'], jnp.zeros((depth, pad), F32)], axis=-1)[:, None],
        w13=jnp.concatenate([grp(p['moe_w1']), grp(p['moe_w3'])], axis=-1).astype(BF16),
        w2=p['moe_w2'].reshape(depth, N_GROUPS, GROUP_HIDDEN, D_MODEL).astype(BF16),
    )


def _layer(layer, x2, bsz, seq, w, final_g, final_norm):
    t = bsz * seq
    bw = BRANCH_WIDTH
    assert t % SSM_TILE == 0
    perm = _chunk_perm()
    qkv, xr, gr, u = _inproj(layer, x2, w['norm1_g'], w['w_main'], jnp.asarray(perm, BF16))
    o_attn = _attention(layer, qkv.reshape(bsz, seq, QKV_WIDTH), w['attn_sink']).reshape(t, bw)
    h_sum = _lru(layer, xr.reshape(bsz, seq, bw), w['conv_w'], w['conv_b'], w['lru_wg'],
                 w['lru_bg'], w['lru_sp']).reshape(t, bw)
    y_ssm = _s5(layer, u, w['tkc'], w['wsinc'], w['moutc'], w['lamd'], w['lamp'], w['dskip'],
                bsz, seq, _pick_tile(bsz, 4))
    x2 = _merge(layer, x2, w['norm1_g'], o_attn, h_sum, gr, y_ssm, jnp.asarray(perm.T, BF16),
                w['w_gate'], w['w_branch'], w['w_glu'], w['b_glu'], w['w_out'])
    return _moe(layer, x2, w['norm2_g'], w['wr_hi'], w['wr_lo'], w['b_route'], w['w13'], w['w2'],
                final_g, _pick_tile(t, 512), final_norm)


_PARAM_KEYS = ('norm1_g', 'w_in', 'attn_sink', 'lru_conv_w', 'lru_conv_b', 'lru_w_r', 'lru_b_r',
               'lru_w_i', 'lru_b_i', 'lru_lambda', 'ssm_a_re', 'ssm_a_im', 'ssm_log_step',
               'ssm_b_re', 'ssm_b_im', 'ssm_c_re', 'ssm_c_im', 'ssm_d', 'ssm_w_glu', 'ssm_b_glu',
               'w_branch', 'w_out', 'norm2_g', 'moe_w_group', 'moe_b_group', 'moe_w_expert',
               'moe_b_expert', 'moe_w1', 'moe_w3', 'moe_w2')


def kernel(x, norm1_g, w_in, attn_sink, lru_conv_w, lru_conv_b, lru_w_r, lru_b_r, lru_w_i, lru_b_i,
           lru_lambda, ssm_a_re, ssm_a_im, ssm_log_step, ssm_b_re, ssm_b_im, ssm_c_re, ssm_c_im,
           ssm_d, ssm_w_glu, ssm_b_glu, w_branch, w_out, norm2_g, moe_w_group, moe_b_group,
           moe_w_expert, moe_b_expert, moe_w1, moe_w3, moe_w2, final_norm_g):
    params = dict(zip(_PARAM_KEYS, (
        norm1_g, w_in, attn_sink, lru_conv_w, lru_conv_b, lru_w_r, lru_b_r, lru_w_i, lru_b_i,
        lru_lambda, ssm_a_re, ssm_a_im, ssm_log_step, ssm_b_re, ssm_b_im, ssm_c_re, ssm_c_im,
        ssm_d, ssm_w_glu, ssm_b_glu, w_branch, w_out, norm2_g, moe_w_group, moe_b_group,
        moe_w_expert, moe_b_expert, moe_w1, moe_w3, moe_w2)))
    bsz, seq, _ = x.shape
    depth = norm1_g.shape[0]
    w = _prepare(params)
    x2 = x.reshape(bsz * seq, D_MODEL)
    for layer in range(depth):
        x2 = _layer(layer, x2, bsz, seq, w, final_norm_g[None], final_norm=(layer == depth - 1))
    return x2.reshape(bsz, seq, D_MODEL)
```

```python
import functools

import jax
import jax.numpy as jnp
import numpy as np
from jax import lax
from jax.experimental import pallas as pl
from jax.experimental.pallas import tpu as pltpu

F32 = jnp.float32
BF16 = jnp.bfloat16

D_MODEL = 1024
N_Q_HEADS = 8
N_KV_HEADS = 2
HEAD_DIM = 64
WINDOW = 128
BLOCK = 128
BRANCH_WIDTH = 512
KV_WIDTH = N_KV_HEADS * HEAD_DIM
LRU_BLOCKS = 8
LRU_C = 8.0
SSM_GROUP = 16
SSM_GROUPS = BRANCH_WIDTH // SSM_GROUP
SSM_STATE = 64
N_BRANCH = 3
N_GROUPS = 4
EXPERTS_PER_GROUP = 4
N_EXPERTS = N_GROUPS * EXPERTS_PER_GROUP
D_EXPERT = 256
GROUP_HIDDEN = EXPERTS_PER_GROUP * D_EXPERT
NORM_EPS = 1e-6
MASK_VALUE = -1e30

LANES = 128
SUBLANES = 8
QKV_WIDTH = BRANCH_WIDTH + 2 * KV_WIDTH
MAIN_COLS = QKV_WIDTH + 3 * BRANCH_WIDTH
HEADS_PER_KV = N_Q_HEADS // N_KV_HEADS
SSM_CHUNK = 16
PAIR_CH = 2 * SSM_GROUP
SSM_PAIRS = SSM_GROUPS // 2
TILE_PAIRS = LANES // PAIR_CH
PAIR_IN = SSM_CHUNK * PAIR_CH
PAIR_STATE = 2 * SSM_STATE
SSM_TILE = 512
N_COMP = 4
SCAN_STEPS = (1, 2, 4)
ROUTE_COLS = LANES
CONV_HALO = 16
SQRT_CLAMP = 1e-30
VMEM_LIMIT = 56 * 1024 * 1024


def _rms(x, g):
    ms = jnp.mean(x * x, axis=-1, keepdims=True)
    return x * lax.rsqrt(ms + NORM_EPS) * g


def _sigmoid(x):
    return 0.5 * jnp.tanh(0.5 * x) + 0.5


def _dot(a, b):
    return jnp.dot(a, b, preferred_element_type=F32)


def _dot_nt(a, b):
    return lax.dot_general(a, b, (((1,), (1,)), ((), ())), preferred_element_type=F32)


def _const_spec(shape):
    nd = len(shape)
    return pl.BlockSpec(shape, lambda *_: (0,) * nd)


def _layer_spec(layer, shape, **kw):
    nd = len(shape)
    return pl.BlockSpec((None,) + tuple(shape), lambda *_: (layer,) + (0,) * nd, **kw)


def _row_in_tile(rows):
    return lax.broadcasted_iota(jnp.int32, (rows, 1), 0) & (SUBLANES - 1)


def _pick_tile(n, want):
    t = min(n, want)
    while n % t:
        t //= 2
    return t


def _chunk_perm():
    t = np.arange(SSM_TILE)
    dst = (t % SSM_CHUNK) * (SSM_TILE // SSM_CHUNK) + t // SSM_CHUNK
    perm = np.zeros((SSM_TILE, SSM_TILE), np.float32)
    perm[dst, t] = 1.0
    return perm


def _inproj_kernel(x_ref, g_ref, w_ref, perm_ref, qkv_ref, xr_ref, gr_ref, u_ref):
    hn = _rms(x_ref[...], g_ref[...]).astype(BF16)
    bw = BRANCH_WIDTH
    qkv_ref[...] = _dot(hn, w_ref[:, 0:QKV_WIDTH]).astype(BF16)
    xr_ref[...] = _dot(hn, w_ref[:, QKV_WIDTH:QKV_WIDTH + bw]).astype(BF16)
    gr_ref[...] = _dot(hn, w_ref[:, QKV_WIDTH + bw:QKV_WIDTH + 2 * bw]).astype(BF16)
    u = _dot(hn, w_ref[:, QKV_WIDTH + 2 * bw:MAIN_COLS]).astype(BF16)
    u_ref[...] = _dot(perm_ref[...], u).astype(BF16)


def _inproj(layer, x2, g, w_main, perm):
    t = x2.shape[0]
    tm = SSM_TILE
    row = lambda w: pl.BlockSpec((tm, w), lambda i: (i, 0))
    return pl.pallas_call(
        _inproj_kernel,
        grid=(t // tm,),
        in_specs=[row(D_MODEL), _layer_spec(layer, (1, D_MODEL)),
                  _layer_spec(layer, (D_MODEL, MAIN_COLS)), _const_spec((tm, tm))],
        out_specs=[row(QKV_WIDTH), row(BRANCH_WIDTH), row(BRANCH_WIDTH), row(BRANCH_WIDTH)],
        out_shape=[jax.ShapeDtypeStruct((t, QKV_WIDTH), BF16)]
        + [jax.ShapeDtypeStruct((t, BRANCH_WIDTH), BF16)] * 3,
        compiler_params=pltpu.CompilerParams(
            dimension_semantics=("parallel",), vmem_limit_bytes=VMEM_LIMIT),
        name="inproj",
    )(x2, g, w_main, perm)


def _attn_bias():
    qi = np.arange(BLOCK)[:, None]
    kj = np.arange(3 * BLOCK)[None, :]
    dist = np.abs(qi - kj + BLOCK)
    out = np.empty((N_KV_HEADS, HEADS_PER_KV * BLOCK, 3 * BLOCK), np.float32)
    for head in range(N_Q_HEADS):
        h, g = divmod(head, HEADS_PER_KV)
        slope = 2.0 ** (-8.0 * (head + 1) / N_Q_HEADS)
        out[h, g * BLOCK:(g + 1) * BLOCK] = np.where(dist <= WINDOW, -slope * dist, 1.0)
    return out


def _attn_kernel(sink_ref, bias_ref, q_ref, kvp_ref, kvc_ref, kvn_ref, o_ref, *, nb, layer):
    n = pl.program_id(1)
    kv = jnp.concatenate([kvp_ref[...], kvc_ref[...], kvn_ref[...]], axis=0).astype(F32)
    k = kv[:, :KV_WIDTH]
    v = kv[:, KV_WIDTH:]
    lane = lax.broadcasted_iota(jnp.int32, (1, KV_WIDTH), 1)
    lo = lane < HEAD_DIM

    rows = HEADS_PER_KV * BLOCK
    kj = lax.broadcasted_iota(jnp.int32, (1, 3 * BLOCK), 1)
    in_seq = ((kj >= BLOCK) | (n > 0)) & ((kj < 2 * BLOCK) | (n < nb - 1))
    head_in_kv = lax.broadcasted_iota(jnp.int32, (rows, 1), 0) // BLOCK

    scale = HEAD_DIM ** -0.5
    for h in range(N_KV_HEADS):
        own = lo if h == 0 else jnp.logical_not(lo)
        km = jnp.where(own, k, 0.0)
        vm = jnp.where(own, v, 0.0)
        k2 = ((km + pltpu.roll(km, HEAD_DIM, 1)) * scale).astype(BF16)
        v2 = (vm + pltpu.roll(vm, HEAD_DIM, 1)).astype(BF16)
        parts = []
        sink = jnp.zeros((rows, 1), F32)
        for g in range(HEADS_PER_KV):
            head = h * HEADS_PER_KV + g
            pair = head // 2
            qp = q_ref[:, pair * KV_WIDTH:(pair + 1) * KV_WIDTH]
            keep = lo if head % 2 == 0 else jnp.logical_not(lo)
            parts.append(jnp.where(keep, qp, jnp.zeros_like(qp)))
            sink = jnp.where(head_in_kv == g, sink_ref[layer, head], sink)
        s = _dot_nt(jnp.concatenate(parts, axis=0), k2)
        bias = bias_ref[h]
        s = jnp.where((bias <= 0.0) & in_seq, s + bias, MASK_VALUE)
        m = jnp.maximum(jnp.max(s, axis=-1, keepdims=True), sink)
        p = jnp.exp(s - m)
        denom = jnp.sum(p, axis=-1, keepdims=True) + jnp.exp(sink - m)
        r = _dot(p.astype(BF16), v2) / denom
        for j in range(HEADS_PER_KV // 2):
            even = r[(2 * j) * BLOCK:(2 * j + 1) * BLOCK]
            odd = r[(2 * j + 1) * BLOCK:(2 * j + 2) * BLOCK]
            pair = h * (HEADS_PER_KV // 2) + j
            o_ref[:, pair * KV_WIDTH:(pair + 1) * KV_WIDTH] = jnp.where(lo, even, odd).astype(BF16)


def _attention(layer, qkv3, sink):
    b, l, _ = qkv3.shape
    nb = l // BLOCK
    kv_col = BRANCH_WIDTH // (2 * KV_WIDTH)
    kv_spec = lambda f: pl.BlockSpec((None, BLOCK, 2 * KV_WIDTH), lambda bi, n: (bi, f(n), kv_col))
    return pl.pallas_call(
        functools.partial(_attn_kernel, nb=nb, layer=layer),
        grid=(b, nb),
        in_specs=[
            pl.BlockSpec(memory_space=pltpu.SMEM),
            _const_spec((N_KV_HEADS, HEADS_PER_KV * BLOCK, 3 * BLOCK)),
            pl.BlockSpec((None, BLOCK, BRANCH_WIDTH), lambda bi, n: (bi, n, 0)),
            kv_spec(lambda n: jnp.maximum(n - 1, 0)),
            kv_spec(lambda n: n),
            kv_spec(lambda n: jnp.minimum(n + 1, nb - 1)),
        ],
        out_specs=pl.BlockSpec((None, BLOCK, BRANCH_WIDTH), lambda bi, n: (bi, n, 0)),
        out_shape=jax.ShapeDtypeStruct((b, l, BRANCH_WIDTH), BF16),
        compiler_params=pltpu.CompilerParams(
            dimension_semantics=("parallel", "parallel"), vmem_limit_bytes=VMEM_LIMIT),
        name="window_attn",
    )(sink, jnp.asarray(_attn_bias()), qkv3, qkv3, qkv3, qkv3)


def _tile_scan(a, u, reverse):
    rows = a.shape[0]
    rit = _row_in_tile(rows)
    for d in SCAN_STEPS:
        if reverse:
            take = rit < SUBLANES - d
            shift = rows - d
        else:
            take = rit >= d
            shift = d
        a_sh = jnp.where(take, pltpu.roll(a, shift, 0), 1.0)
        u_sh = jnp.where(take, pltpu.roll(u, shift, 0), 0.0)
        u = u + a * u_sh
        a = a * a_sh
    return a, u


def _lru_kernel(x_ref, cw_ref, cb_ref, wg_ref, bg_ref, sp_ref, o_ref, xe_s, xc_s, a_s, u_s, hf_s,
                *, seq, rc):
    phase = pl.program_id(1)
    bw = BRANCH_WIDTH
    n_tiles = seq // SUBLANES

    def conv():
        zeros = jnp.zeros((CONV_HALO, bw), F32)
        xe_s[0:CONV_HALO, :] = zeros
        xe_s[CONV_HALO + seq:CONV_HALO + seq + CONV_HALO, :] = zeros
        xe_s[CONV_HALO:CONV_HALO + seq, :] = x_ref[...].astype(F32)
        cw = cw_ref[...]
        for c in range(seq // rc):
            base = CONV_HALO + c * rc
            xc = cb_ref[...]
            for tap in range(4):
                xc = xc + cw[tap:tap + 1] * xe_s[base + tap - 2:base + tap - 2 + rc, :]
            xc_s[c * rc:(c + 1) * rc, :] = xc

    def build(reverse):
        for c in range(seq // rc):
            xc = xc_s[c * rc:(c + 1) * rc, :]
            gates = _sigmoid(_dot(xc.astype(BF16), wg_ref[...]) + bg_ref[...])
            log_a = -LRU_C * gates[:, :bw] * sp_ref[...]
            a = jnp.exp(log_a)
            t = 1.0 - a * a
            u = t * lax.rsqrt(jnp.maximum(t, SQRT_CLAMP)) * (gates[:, bw:] * xc)
            a, u = _tile_scan(a, u, reverse)
            a_s[c * rc:(c + 1) * rc, :] = a
            u_s[c * rc:(c + 1) * rc, :] = u

    @pl.when(phase == 0)
    def _():
        conv()
        build(False)

        def step(k, carry):
            r0 = pl.multiple_of(k * SUBLANES, SUBLANES)
            h = u_s[pl.ds(r0, SUBLANES), :] + a_s[pl.ds(r0, SUBLANES), :] * carry
            hf_s[pl.ds(r0, SUBLANES), :] = h
            return jnp.broadcast_to(h[SUBLANES - 1:SUBLANES, :], (SUBLANES, bw))

        lax.fori_loop(0, n_tiles, step, jnp.zeros((SUBLANES, bw), F32), unroll=4)

    @pl.when(phase == 1)
    def _():
        build(True)

        def step(k, carry):
            r0 = pl.multiple_of((n_tiles - 1 - k) * SUBLANES, SUBLANES)
            h = u_s[pl.ds(r0, SUBLANES), :] + a_s[pl.ds(r0, SUBLANES), :] * carry
            o_ref[pl.ds(r0, SUBLANES), :] = h + hf_s[pl.ds(r0, SUBLANES), :]
            return jnp.broadcast_to(h[0:1, :], (SUBLANES, bw))

        lax.fori_loop(0, n_tiles, step, jnp.zeros((SUBLANES, bw), F32), unroll=4)


def _lru(layer, xr3, conv_w, conv_b, wg, bg, sp):
    b, l, w = xr3.shape
    seq_spec = pl.BlockSpec((None, l, w), lambda bi, ph: (bi, 0, 0))
    per_dir = lambda shape: pl.BlockSpec(
        (None, None) + shape, lambda bi, ph: (layer, ph) + (0,) * len(shape))
    return pl.pallas_call(
        functools.partial(_lru_kernel, seq=l, rc=_pick_tile(l, 512)),
        grid=(b, 2),
        in_specs=[seq_spec, _layer_spec(layer, (4, w)), _layer_spec(layer, (1, w)),
                  per_dir((w, 2 * w)), per_dir((1, 2 * w)), per_dir((1, w))],
        out_specs=seq_spec,
        out_shape=jax.ShapeDtypeStruct((b, l, w), F32),
        scratch_shapes=[pltpu.VMEM((l + 2 * CONV_HALO, w), F32)] + [pltpu.VMEM((l, w), F32)] * 4,
        compiler_params=pltpu.CompilerParams(
            dimension_semantics=("parallel", "arbitrary"), vmem_limit_bytes=VMEM_LIMIT),
        name="rg_lru",
    )(xr3, conv_w, conv_b, wg, bg, sp)


def _block_diag(wb):
    nbk, bd = wb.shape[-3], wb.shape[-2]
    eye = jnp.eye(nbk, dtype=wb.dtype)
    out = jnp.einsum('...hij,hk->...hikj', wb, eye)
    return out.reshape(wb.shape[:-3] + (nbk * bd, nbk * bd))


def _split3(a):
    a1 = a.astype(BF16)
    r1 = a - a1.astype(F32)
    a2 = r1.astype(BF16)
    a3 = (r1 - a2.astype(F32)).astype(BF16)
    return a1, a2, a3


def _place(a, sel):
    p1, p2, p3 = _split3(a)
    return _dot(p1, sel) + _dot(p2, sel) + _dot(p3, sel)


def _place_nt(sel, a):
    p1, p2, p3 = _split3(a)
    return _dot_nt(sel, p1) + _dot_nt(sel, p2) + _dot_nt(sel, p3)


def _sel_cols(n_cols, base, idx_of_col):
    lane = lax.broadcasted_iota(jnp.int32, (LANES, n_cols), 0)
    col = lax.broadcasted_iota(jnp.int32, (LANES, n_cols), 1)
    return (lane == base + idx_of_col(col)).astype(BF16)


def _sel_rows(n_rows, base, idx_of_row):
    row = lax.broadcasted_iota(jnp.int32, (n_rows, LANES), 0)
    lane = lax.broadcasted_iota(jnp.int32, (n_rows, LANES), 1)
    return (lane == base + idx_of_row(row)).astype(BF16)


def _s5_build(a1_ref, a2_ref, tk_s, wsin_s, mout_s):
    q = SSM_CHUNK
    st = PAIR_STATE
    tp = TILE_PAIRS
    pw = LANES // 4
    cw = SSM_GROUP
    hp = lax.Precision.HIGHEST
    a1 = a1_ref[...].reshape(tp * st, LANES)
    a2 = a2_ref[...].reshape(tp * st, LANES)
    blk = lambda i: i >> 5
    grp = lambda i: (i >> 4) & 1
    ch = lambda i: i & (cw - 1)

    srow = lax.broadcasted_iota(jnp.int32, (tp * st, 1), 0)
    icol = lax.broadcasted_iota(jnp.int32, (1, PAIR_IN), 1)
    irow = lax.broadcasted_iota(jnp.int32, (PAIR_IN, 1), 0)
    scol = lax.broadcasted_iota(jnp.int32, (1, tp * st), 1)
    same_sc = ((srow >> 6) & 1) == grp(icol)
    same_is = grp(irow) == ((scol >> 6) & 1)
    kg_row = lax.broadcasted_iota(jnp.int32, (PAIR_CH, 1), 0)
    same_ks = grp(kg_row) == ((scol >> 6) & 1)
    erow = (lax.broadcasted_iota(jnp.int32, (PAIR_IN, PAIR_CH), 0) & (PAIR_CH - 1)
            == lax.broadcasted_iota(jnp.int32, (PAIR_IN, PAIR_CH), 1)).astype(BF16)

    c_re = _place(a2, _sel_cols(PAIR_IN, 0, ch))
    c_im = _place(a2, _sel_cols(PAIR_IN, cw, ch))

    def c_times(p_re, p_im):
        return (jnp.where(same_sc, c_re * p_re - c_im * p_im, 0.0),
                jnp.where(same_sc, -(c_re * p_im + c_im * p_re), 0.0))

    tk_fwd = None
    for d in range(2):
        pbase = 2 * d * pw
        bbase = 2 * cw + 2 * d * cw
        power = lambda f: (_place(a1, _sel_cols(PAIR_IN, pbase, f)),
                           _place(a1, _sel_cols(PAIR_IN, pbase + pw, f)))

        m_re, m_im = c_times(*power((lambda i: blk(i) + 1) if d == 0 else (lambda i: q - blk(i))))
        for pp in range(tp):
            mout_s[pp, (2 * d) * st:(2 * d + 1) * st, :] = m_re[pp * st:(pp + 1) * st].astype(BF16)
            mout_s[pp, (2 * d + 1) * st:(2 * d + 2) * st, :] = m_im[pp * st:(pp + 1) * st].astype(BF16)

        spow = (lambda i: q - 1 - blk(i)) if d == 0 else blk
        pt_re = _place_nt(_sel_rows(PAIR_IN, pbase, spow), a1)
        pt_im = _place_nt(_sel_rows(PAIR_IN, pbase + pw, spow), a1)
        bt_re = _place_nt(_sel_rows(PAIR_IN, bbase, ch), a2)
        bt_im = _place_nt(_sel_rows(PAIR_IN, bbase + cw, ch), a2)
        w_re = jnp.where(same_is, pt_re * bt_re - pt_im * bt_im, 0.0)
        w_im = jnp.where(same_is, pt_re * bt_im + pt_im * bt_re, 0.0)
        for pp in range(tp):
            wsin_s[pp, :, (2 * d) * st:(2 * d + 1) * st] = w_re[:, pp * st:(pp + 1) * st].astype(BF16)
            wsin_s[pp, :, (2 * d + 1) * st:(2 * d + 2) * st] = w_im[:, pp * st:(pp + 1) * st].astype(BF16)

        k_re, k_im = c_times(*power(blk if d == 0 else (lambda i: q - 1 - blk(i))))
        b_re = jnp.where(same_ks, _place_nt(_sel_rows(PAIR_CH, bbase, ch), a2), 0.0)
        b_im = jnp.where(same_ks, _place_nt(_sel_rows(PAIR_CH, bbase + cw, ch), a2), 0.0)
        tk_bwd = []
        for pp in range(tp):
            sl = slice(pp * st, (pp + 1) * st)
            lagk = (jnp.dot(b_re[:, sl], k_re[sl], preferred_element_type=F32, precision=hp)
                    + jnp.dot(b_im[:, sl], k_im[sl], preferred_element_type=F32, precision=hp))
            l1, l2, l3 = _split3(lagk)
            rep = _dot(erow, l1) + _dot(erow, l2) + _dot(erow, l3)
            blocks = []
            for s in range(q):
                piece = rep[s * PAIR_CH:(s + 1) * PAIR_CH]
                if d == 0:
                    shifted = piece if s == 0 else pltpu.roll(piece, s * PAIR_CH, 1)
                    blocks.append(jnp.where(blk(icol) >= s, shifted, 0.0))
                else:
                    shift = ((s + 1) * PAIR_CH) % PAIR_IN
                    shifted = piece if shift == 0 else pltpu.roll(piece, shift, 1)
                    blocks.append(jnp.where(blk(icol) <= s, shifted, 0.0))
            tk_bwd.append(blocks)
        if d == 0:
            tk_fwd = tk_bwd
        else:
            for pp in range(tp):
                for s in range(q):
                    tk_s[pp, s * PAIR_CH:(s + 1) * PAIR_CH, :] = (
                        tk_fwd[pp][s] + tk_bwd[pp][s]).astype(BF16)


def _lane_piece_mask(piece):
    lane = lax.broadcasted_iota(jnp.int32, (1, LANES), 1)
    return (lane // PAIR_CH) == piece


def _s5_kernel(u_ref, a1_ref, a2_ref, lamd_ref, lamp_ref, d_ref, y_ref,
               x_s, y_s, g_s, tk_s, wsin_s, mout_s, *, nch, nbat):
    q = SSM_CHUNK
    rows = nbat * nch
    spt = LANES // PAIR_CH

    @pl.when(pl.program_id(1) == 0)
    def _():
        _s5_build(a1_ref, a2_ref, tk_s, wsin_s, mout_s)

    step_rows = lambda s: u_ref[:, s].astype(F32).reshape(rows, LANES)
    masks = [_lane_piece_mask(j) for j in range(spt)]
    for kt in range(q // spt):
        rolled = []
        for j in range(spt):
            xs = step_rows(kt * spt + j)
            rolled.append([xs if sh == 0 else pltpu.roll(xs, sh * PAIR_CH, 1) for sh in range(spt)])
        for pp in range(TILE_PAIRS):
            tile = None
            for j in range(spt):
                src = rolled[j][(j - pp) % spt]
                tile = src if tile is None else jnp.where(masks[j], src, tile)
            x_s[pp, :, kt * LANES:(kt + 1) * LANES] = tile.astype(BF16)

    for pp in range(TILE_PAIRS):
        _s5_pair(x_s.at[pp], y_s.at[pp], g_s, tk_s.at[pp], wsin_s.at[pp], mout_s.at[pp],
                 lamd_ref.at[pp], lamp_ref.at[pp], nch=nch, nbat=nbat)

    for kt in range(q // spt):
        rolled = []
        for pp in range(TILE_PAIRS):
            yt = y_s[pp, :, kt * LANES:(kt + 1) * LANES]
            rolled.append([yt if sh == 0 else pltpu.roll(yt, sh * PAIR_CH, 1) for sh in range(spt)])
        for j in range(spt):
            tile = None
            for pp in range(TILE_PAIRS):
                src = rolled[pp][(pp - j) % spt]
                tile = src if tile is None else jnp.where(masks[pp], src, tile)
            s = kt * spt + j
            out = tile + d_ref[...] * step_rows(s)
            y_ref[:, s] = out.reshape(y_ref.shape[0], y_ref.shape[2], LANES).astype(y_ref.dtype)


def _s5_pair(x_ref, y_ref, g_s, tk_ref, wsin_ref, mout_ref, lamd_ref, lamp_ref, *, nch, nbat):
    rows = nbat * nch
    st = PAIR_STATE
    n_tiles = nch // SUBLANES
    x = x_ref[...]

    sloc = _dot(x, wsin_ref[...])
    chunk = lax.broadcasted_iota(jnp.int32, (rows, 1), 0) & (nch - 1)
    rit = _row_in_tile(rows)
    comps = []
    for comp in range(N_COMP):
        reverse = comp >= 2
        piece = sloc[:, comp * st:(comp + 1) * st]
        if reverse:
            piece = jnp.where(chunk == nch - 1, 0.0, pltpu.roll(piece, rows - 1, 0))
        else:
            piece = jnp.where(chunk == 0, 0.0, pltpu.roll(piece, 1, 0))
        comps.append(piece)

    for di, d in enumerate(SCAN_STEPS):
        for base in (0, 2):
            reverse = base == 2
            lr = lamd_ref[di * N_COMP + base:di * N_COMP + base + 1, :]
            li = lamd_ref[di * N_COMP + base + 1:di * N_COMP + base + 2, :]
            if reverse:
                take = rit < SUBLANES - d
                shift = rows - d
            else:
                take = rit >= d
                shift = d
            gr, gi = comps[base], comps[base + 1]
            sr = jnp.where(take, pltpu.roll(gr, shift, 0), 0.0)
            si = jnp.where(take, pltpu.roll(gi, shift, 0), 0.0)
            comps[base] = gr + lr * sr - li * si
            comps[base + 1] = gi + lr * si + li * sr
    for comp in range(N_COMP):
        g_s[:, comp * st:(comp + 1) * st] = comps[comp]

    pf_r, pf_i = lamp_ref[0:SUBLANES, :], lamp_ref[SUBLANES:2 * SUBLANES, :]
    pb_r, pb_i = lamp_ref[2 * SUBLANES:3 * SUBLANES, :], lamp_ref[3 * SUBLANES:4 * SUBLANES, :]

    def step(k, carry):
        out = []
        for bi in range(nbat):
            cfr, cfi, cbr, cbi = carry[4 * bi:4 * bi + 4]
            rf = pl.multiple_of(bi * nch + k * SUBLANES, SUBLANES)
            rb = pl.multiple_of(bi * nch + (n_tiles - 1 - k) * SUBLANES, SUBLANES)
            gfr = g_s[pl.ds(rf, SUBLANES), 0:st] + pf_r * cfr - pf_i * cfi
            gfi = g_s[pl.ds(rf, SUBLANES), st:2 * st] + pf_r * cfi + pf_i * cfr
            gbr = g_s[pl.ds(rb, SUBLANES), 2 * st:3 * st] + pb_r * cbr - pb_i * cbi
            gbi = g_s[pl.ds(rb, SUBLANES), 3 * st:4 * st] + pb_r * cbi + pb_i * cbr
            g_s[pl.ds(rf, SUBLANES), 0:st] = gfr
            g_s[pl.ds(rf, SUBLANES), st:2 * st] = gfi
            g_s[pl.ds(rb, SUBLANES), 2 * st:3 * st] = gbr
            g_s[pl.ds(rb, SUBLANES), 3 * st:4 * st] = gbi
            last = lambda v: jnp.broadcast_to(v[SUBLANES - 1:SUBLANES, :], (SUBLANES, st))
            first = lambda v: jnp.broadcast_to(v[0:1, :], (SUBLANES, st))
            out += [last(gfr), last(gfi), first(gbr), first(gbi)]
        return tuple(out)

    z = jnp.zeros((SUBLANES, st), F32)
    lax.fori_loop(0, n_tiles, step, (z,) * (4 * nbat))

    y_ref[...] = _dot(x, tk_ref[...]) + _dot(g_s[...].astype(BF16), mout_ref[...])


def _s5(layer, u2, a1, a2, lamd, lamp, dskip, bsz, seq, nbat):
    q = SSM_CHUNK
    nch = seq // q
    assert nch & (nch - 1) == 0 and nch % SUBLANES == 0, "chunks per sequence must be a power of two"
    rows = nbat * nch
    assert (nbat * seq) % SSM_TILE == 0
    nt = nbat * seq // SSM_TILE
    cpt = SSM_TILE // q
    st = PAIR_STATE
    tp = TILE_PAIRS
    tile_spec = lambda shape: pl.BlockSpec(
        (None, None) + shape, lambda o, i: (layer, o) + (0,) * len(shape))
    io_spec = pl.BlockSpec((nt, q, cpt, LANES), lambda o, i: (i, 0, 0, o))
    u4 = u2.reshape(u2.shape[0] // SSM_TILE, q, cpt, BRANCH_WIDTH)
    return pl.pallas_call(
        functools.partial(_s5_kernel, nch=nch, nbat=nbat),
        grid=(BRANCH_WIDTH // LANES, bsz // nbat),
        in_specs=[
            io_spec,
            tile_spec((tp, st, LANES)),
            tile_spec((tp, st, LANES)),
            tile_spec((tp, len(SCAN_STEPS) * N_COMP, st)),
            tile_spec((tp, N_COMP * SUBLANES, st)),
            tile_spec((1, LANES)),
        ],
        out_specs=io_spec,
        out_shape=jax.ShapeDtypeStruct(u4.shape, BF16),
        scratch_shapes=[pltpu.VMEM((tp, rows, PAIR_IN), BF16),
                        pltpu.VMEM((tp, rows, PAIR_IN), F32),
                        pltpu.VMEM((rows, N_COMP * st), F32),
                        pltpu.VMEM((tp, PAIR_IN, PAIR_IN), BF16),
                        pltpu.VMEM((tp, PAIR_IN, N_COMP * st), BF16),
                        pltpu.VMEM((tp, N_COMP * st, PAIR_IN), BF16)],
        compiler_params=pltpu.CompilerParams(
            dimension_semantics=("parallel", "arbitrary"), vmem_limit_bytes=VMEM_LIMIT),
        name="s5_chunked",
    )(u4, a1, a2, lamd, lamp, dskip).reshape(u2.shape)


def _s5_params(a_re, a_im, log_step, b_re, b_im, c_re, c_im, d_skip):
    q = SSM_CHUNK
    g, n, c = SSM_GROUPS, SSM_STATE, SSM_GROUP
    no, og = SSM_PAIRS, 2
    step = jnp.exp(log_step)[..., None]
    dr = a_re * step
    di = a_im * step
    mag = jnp.exp(dr)
    abar_r = mag * jnp.cos(di)
    abar_i = mag * jnp.sin(di)
    den = a_re * a_re + a_im * a_im
    nr = abar_r - 1.0
    sr = (nr * a_re + abar_i * a_im) / den
    si = (abar_i * a_re - nr * a_im) / den
    bb_r = sr[..., None] * b_re - si[..., None] * b_im
    bb_i = sr[..., None] * b_im + si[..., None] * b_re

    def powers(kk):
        kk = jnp.asarray(kk, F32)
        pmag = jnp.exp(dr[..., None] * kk)
        return pmag * jnp.cos(di[..., None] * kk), pmag * jnp.sin(di[..., None] * kk)

    pr, pi = powers(jnp.arange(LANES // 4))
    a1 = jnp.concatenate([pr[0], pi[0], pr[1], pi[1]], axis=-1).reshape(no, PAIR_STATE, LANES)
    by_state = lambda w_: w_.reshape(no, PAIR_STATE, c)
    a2 = jnp.concatenate(
        [by_state(c_re.transpose(0, 2, 1)), by_state(c_im.transpose(0, 2, 1)),
         by_state(bb_r[0]), by_state(bb_i[0]), by_state(bb_r[1]), by_state(bb_i[1]),
         jnp.zeros((no, PAIR_STATE, LANES - 6 * c), F32)], axis=-1)

    def scan_rows(kk):
        sr_, si_ = powers(q * jnp.asarray(kk))
        rows = jnp.stack([sr_[0], si_[0], sr_[1], si_[1]], axis=0)
        return rows.reshape(N_COMP, no, og, n, len(kk))

    lamd = scan_rows(SCAN_STEPS).transpose(1, 4, 0, 2, 3).reshape(
        no, len(SCAN_STEPS) * N_COMP, PAIR_STATE)
    fwd = scan_rows(list(range(1, SUBLANES + 1)))[:2]
    bwd = scan_rows(list(range(SUBLANES, 0, -1)))[2:]
    lamp = jnp.concatenate([fwd, bwd], axis=0).transpose(1, 0, 4, 2, 3).reshape(
        no, N_COMP * SUBLANES, PAIR_STATE)
    dskip = d_skip.reshape(BRANCH_WIDTH // LANES, 1, LANES)
    by_tile = lambda w_: w_.reshape((BRANCH_WIDTH // LANES, TILE_PAIRS) + w_.shape[1:])
    return by_tile(a1), by_tile(a2), by_tile(lamd), by_tile(lamp), dskip


def _merge_kernel(x_ref, g_ref, oa_ref, hs_ref, gr_ref, ys_ref, unperm_ref, wgate_ref, wb_ref,
                  wglu_ref, bglu_ref, wout_ref, o_ref):
    x = x_ref[...]
    hn = _rms(x, g_ref[...]).astype(BF16)
    bw = BRANCH_WIDTH
    o_lru = (jax.nn.gelu(gr_ref[...].astype(F32)) * hs_ref[...]).astype(BF16)
    zg = _dot(jax.nn.gelu(ys_ref[...].astype(F32)).astype(BF16), wglu_ref[...]) + bglu_ref[...]
    o_ssm = (zg[:, :bw] * _sigmoid(zg[:, bw:])).astype(BF16)
    o_ssm = _dot(unperm_ref[...], o_ssm).astype(BF16)
    mixed = None
    for kb, ob in enumerate((oa_ref[...], o_lru, o_ssm)):
        gate = _sigmoid(_dot(hn, wgate_ref[:, kb * D_MODEL:(kb + 1) * D_MODEL]))
        term = _dot(ob, wb_ref[kb]) * gate
        mixed = term if mixed is None else mixed + term
    o_ref[...] = x + _dot(mixed.astype(BF16), wout_ref[...])


def _merge(layer, x2, g, o_attn, h_sum, gr, y_ssm, unperm, w_gate, w_branch, w_glu, b_glu, w_out):
    t = x2.shape[0]
    tm = SSM_TILE
    row = lambda w: pl.BlockSpec((tm, w), lambda i: (i, 0))
    return pl.pallas_call(
        _merge_kernel,
        grid=(t // tm,),
        in_specs=[row(D_MODEL), _layer_spec(layer, (1, D_MODEL)), row(BRANCH_WIDTH),
                  row(BRANCH_WIDTH), row(BRANCH_WIDTH), row(BRANCH_WIDTH), _const_spec((tm, tm)),
                  _layer_spec(layer, (D_MODEL, N_BRANCH * D_MODEL)),
                  _layer_spec(layer, (N_BRANCH, BRANCH_WIDTH, D_MODEL)),
                  _layer_spec(layer, (BRANCH_WIDTH, 2 * BRANCH_WIDTH)),
                  _layer_spec(layer, (1, 2 * BRANCH_WIDTH)),
                  _layer_spec(layer, (D_MODEL, D_MODEL))],
        out_specs=row(D_MODEL),
        out_shape=jax.ShapeDtypeStruct((t, D_MODEL), F32),
        compiler_params=pltpu.CompilerParams(
            dimension_semantics=("parallel",), vmem_limit_bytes=VMEM_LIMIT),
        name="merge",
    )(x2, g, o_attn, h_sum, gr, y_ssm, unperm, w_gate, w_branch, w_glu, b_glu, w_out)


def _route(logits):
    lane = lax.broadcasted_iota(jnp.int32, logits.shape, 1).astype(F32)
    neg = -jnp.inf
    first = lambda mask: jnp.min(jnp.where(mask, lane, float(ROUTE_COLS)), axis=-1, keepdims=True)
    gmask = lane < N_GROUPS
    gmax = jnp.max(jnp.where(gmask, logits, neg), axis=-1, keepdims=True)
    gidx = first(gmask & (logits == gmax))
    gsum = jnp.sum(jnp.where(gmask, jnp.exp(logits - gmax), 0.0), axis=-1, keepdims=True)
    g_w = 1.0 / gsum
    e_lo = N_GROUPS + EXPERTS_PER_GROUP * gidx
    emask = (lane >= e_lo) & (lane < e_lo + EXPERTS_PER_GROUP)
    v1 = jnp.max(jnp.where(emask, logits, neg), axis=-1, keepdims=True)
    i1 = first(emask & (logits == v1))
    emask2 = emask & (lane != i1)
    v2 = jnp.max(jnp.where(emask2, logits, neg), axis=-1, keepdims=True)
    i2 = first(emask2 & (logits == v2))
    t = jnp.exp(v2 - v1)
    w1 = 1.0 / (1.0 + t)
    w2 = t * w1
    return jnp.where(lane == i1, w1 * g_w, jnp.where(lane == i2, w2 * g_w, 0.0))


def _moe_kernel(x_ref, g_ref, wrh_ref, wrl_ref, br_ref, w13_ref, w2_ref, fg_ref, o_ref,
                *, final_norm):
    x = x_ref[...]
    tm = x.shape[0]
    hn = _rms(x, g_ref[...])
    hb = hn.astype(BF16)
    lo = (hn - hb.astype(F32)).astype(BF16)
    logits = (_dot(hb, wrh_ref[...]) + _dot(hb, wrl_ref[...]) + _dot(lo, wrh_ref[...])
              + br_ref[...])
    cw = _route(logits)
    lane = lax.broadcasted_iota(jnp.int32, cw.shape, 1)
    y = None
    for grp in range(N_GROUPS):
        h = _dot(hb, w13_ref[grp])
        cols = []
        for e in range(EXPERTS_PER_GROUP):
            col = N_GROUPS + grp * EXPERTS_PER_GROUP + e
            wcol = jnp.sum(jnp.where(lane == col, cw, 0.0), axis=-1, keepdims=True)
            cols.append(jnp.broadcast_to(wcol, (tm, D_EXPERT)))
        h1 = h[:, :GROUP_HIDDEN]
        hg = (h1 * _sigmoid(h1) * h[:, GROUP_HIDDEN:] * jnp.concatenate(cols, axis=1)).astype(BF16)
        term = _dot(hg, w2_ref[grp])
        y = term if y is None else y + term
    y = x + y
    if final_norm:
        y = _rms(y, fg_ref[...])
    o_ref[...] = y


def _moe(layer, x2, g, wr_hi, wr_lo, b_route, w13, w2, final_g, tm, final_norm):
    t = x2.shape[0]
    row = pl.BlockSpec((tm, D_MODEL), lambda i: (i, 0))
    once = pl.Buffered(1)
    return pl.pallas_call(
        functools.partial(_moe_kernel, final_norm=final_norm),
        grid=(t // tm,),
        in_specs=[row,
                  _layer_spec(layer, (1, D_MODEL)),
                  _layer_spec(layer, (D_MODEL, ROUTE_COLS)),
                  _layer_spec(layer, (D_MODEL, ROUTE_COLS)),
                  _layer_spec(layer, (1, ROUTE_COLS)),
                  _layer_spec(layer, (N_GROUPS, D_MODEL, 2 * GROUP_HIDDEN), pipeline_mode=once),
                  _layer_spec(layer, (N_GROUPS, GROUP_HIDDEN, D_MODEL), pipeline_mode=once),
                  _const_spec((1, D_MODEL))],
        out_specs=row,
        out_shape=jax.ShapeDtypeStruct((t, D_MODEL), F32),
        compiler_params=pltpu.CompilerParams(
            dimension_semantics=("parallel",), vmem_limit_bytes=VMEM_LIMIT),
        name="moe_final" if final_norm else "moe",
    )(x2, g, wr_hi, wr_lo, b_route, w13, w2, final_g)


def _prepare(p):
    w_in = p['w_in']
    depth = w_in.shape[0]
    pad = ROUTE_COLS - N_GROUPS - N_EXPERTS
    w_route = jnp.concatenate(
        [p['moe_w_group'], p['moe_w_expert'], jnp.zeros((depth, D_MODEL, pad), F32)], axis=-1)
    wr_hi = w_route.astype(BF16)
    grp = lambda w: w.transpose(0, 1, 3, 2, 4).reshape(depth, N_GROUPS, D_MODEL, GROUP_HIDDEN)
    s5_a1, s5_a2, lamd, lamp, dskip = jax.vmap(_s5_params)(
        p['ssm_a_re'], p['ssm_a_im'], p['ssm_log_step'], p['ssm_b_re'], p['ssm_b_im'],
        p['ssm_c_re'], p['ssm_c_im'], p['ssm_d'])
    return dict(
        norm1_g=p['norm1_g'][:, None],
        w_main=w_in[:, :, :MAIN_COLS].astype(BF16),
        w_gate=w_in[:, :, MAIN_COLS:].astype(BF16),
        attn_sink=p['attn_sink'],
        conv_w=p['lru_conv_w'],
        conv_b=p['lru_conv_b'][:, None],
        lru_wg=jnp.concatenate([_block_diag(p['lru_w_r']), _block_diag(p['lru_w_i'])],
                               axis=-1).astype(BF16),
        lru_bg=jnp.concatenate([p['lru_b_r'], p['lru_b_i']], axis=-1)[:, :, None],
        lru_sp=jax.nn.softplus(-p['lru_lambda'])[:, :, None],
        s5_a1=s5_a1, s5_a2=s5_a2, lamd=lamd, lamp=lamp, dskip=dskip,
        w_branch=p['w_branch'].astype(BF16),
        w_glu=p['ssm_w_glu'].astype(BF16),
        b_glu=p['ssm_b_glu'][:, None],
        w_out=p['w_out'].astype(BF16),
        norm2_g=p['norm2_g'][:, None],
        wr_hi=wr_hi,
        wr_lo=(w_route - wr_hi.astype(F32)).astype(BF16),
        b_route=jnp.concatenate(
            [p['moe_b_group'], p['moe_b_expert'], jnp.zeros((depth, pad), F32)], axis=-1)[:, None],
        w13=jnp.concatenate([grp(p['moe_w1']), grp(p['moe_w3'])], axis=-1).astype(BF16),
        w2=p['moe_w2'].reshape(depth, N_GROUPS, GROUP_HIDDEN, D_MODEL).astype(BF16),
    )


def _layer(layer, x2, bsz, seq, w, final_g, final_norm):
    t = bsz * seq
    bw = BRANCH_WIDTH
    assert t % SSM_TILE == 0
    perm = _chunk_perm()
    qkv, xr, gr, u = _inproj(layer, x2, w['norm1_g'], w['w_main'], jnp.asarray(perm, BF16))
    o_attn = _attention(layer, qkv.reshape(bsz, seq, QKV_WIDTH), w['attn_sink']).reshape(t, bw)
    h_sum = _lru(layer, xr.reshape(bsz, seq, bw), w['conv_w'], w['conv_b'], w['lru_wg'],
                 w['lru_bg'], w['lru_sp']).reshape(t, bw)
    y_ssm = _s5(layer, u, w['s5_a1'], w['s5_a2'], w['lamd'], w['lamp'], w['dskip'],
                bsz, seq, _pick_tile(bsz, 4))
    x2 = _merge(layer, x2, w['norm1_g'], o_attn, h_sum, gr, y_ssm, jnp.asarray(perm.T, BF16),
                w['w_gate'], w['w_branch'], w['w_glu'], w['b_glu'], w['w_out'])
    return _moe(layer, x2, w['norm2_g'], w['wr_hi'], w['wr_lo'], w['b_route'], w['w13'], w['w2'],
                final_g, _pick_tile(t, 512), final_norm)


_PARAM_KEYS = ('norm1_g', 'w_in', 'attn_sink', 'lru_conv_w', 'lru_conv_b', 'lru_w_r', 'lru_b_r',
               'lru_w_i', 'lru_b_i', 'lru_lambda', 'ssm_a_re', 'ssm_a_im', 'ssm_log_step',
               'ssm_b_re', 'ssm_b_im', 'ssm_c_re', 'ssm_c_im', 'ssm_d', 'ssm_w_glu', 'ssm_b_glu',
               'w_branch', 'w_out', 'norm2_g', 'moe_w_group', 'moe_b_group', 'moe_w_expert',
               'moe_b_expert', 'moe_w1', 'moe_w3', 'moe_w2')


def kernel(x, norm1_g, w_in, attn_sink, lru_conv_w, lru_conv_b, lru_w_r, lru_b_r, lru_w_i, lru_b_i,
           lru_lambda, ssm_a_re, ssm_a_im, ssm_log_step, ssm_b_re, ssm_b_im, ssm_c_re, ssm_c_im,
           ssm_d, ssm_w_glu, ssm_b_glu, w_branch, w_out, norm2_g, moe_w_group, moe_b_group,
           moe_w_expert, moe_b_expert, moe_w1, moe_w3, moe_w2, final_norm_g):
    params = dict(zip(_PARAM_KEYS, (
        norm1_g, w_in, attn_sink, lru_conv_w, lru_conv_b, lru_w_r, lru_b_r, lru_w_i, lru_b_i,
        lru_lambda, ssm_a_re, ssm_a_im, ssm_log_step, ssm_b_re, ssm_b_im, ssm_c_re, ssm_c_im,
        ssm_d, ssm_w_glu, ssm_b_glu, w_branch, w_out, norm2_g, moe_w_group, moe_b_group,
        moe_w_expert, moe_b_expert, moe_w1, moe_w3, moe_w2)))
    bsz, seq, _ = x.shape
    depth = norm1_g.shape[0]
    w = _prepare(params)
    x2 = x.reshape(bsz * seq, D_MODEL)
    for layer in range(depth):
        x2 = _layer(layer, x2, bsz, seq, w, final_norm_g[None], final_norm=(layer == depth - 1))
    return x2.reshape(bsz, seq, D_MODEL)
```

```python
import functools

import jax
import jax.numpy as jnp
import numpy as np
from jax import lax
from jax.experimental import pallas as pl
from jax.experimental.pallas import tpu as pltpu

F32 = jnp.float32
BF16 = jnp.bfloat16

D_MODEL = 1024
N_Q_HEADS = 8
N_KV_HEADS = 2
HEAD_DIM = 64
WINDOW = 128
BLOCK = 128
BRANCH_WIDTH = 512
KV_WIDTH = N_KV_HEADS * HEAD_DIM
LRU_BLOCKS = 8
LRU_C = 8.0
SSM_GROUP = 16
SSM_GROUPS = BRANCH_WIDTH // SSM_GROUP
SSM_STATE = 64
N_BRANCH = 3
N_GROUPS = 4
EXPERTS_PER_GROUP = 4
N_EXPERTS = N_GROUPS * EXPERTS_PER_GROUP
D_EXPERT = 256
GROUP_HIDDEN = EXPERTS_PER_GROUP * D_EXPERT
NORM_EPS = 1e-6
MASK_VALUE = -1e30

LANES = 128
SUBLANES = 8
QKV_WIDTH = BRANCH_WIDTH + 2 * KV_WIDTH
MAIN_COLS = QKV_WIDTH + 3 * BRANCH_WIDTH
HEADS_PER_KV = N_Q_HEADS // N_KV_HEADS
SSM_CHUNK = 16
PAIR_CH = 2 * SSM_GROUP
SSM_PAIRS = SSM_GROUPS // 2
TILE_PAIRS = LANES // PAIR_CH
PAIR_IN = SSM_CHUNK * PAIR_CH
PAIR_STATE = 2 * SSM_STATE
SSM_TILE = 512
N_COMP = 4
SCAN_STEPS = (1, 2, 4)
ROUTE_COLS = LANES
MOE_ALIGN = 16
MOE_CHUNK = 160
CONV_HALO = 16
SQRT_CLAMP = 1e-30
VMEM_LIMIT = 56 * 1024 * 1024


def _rms(x, g):
    ms = jnp.mean(x * x, axis=-1, keepdims=True)
    return x * lax.rsqrt(ms + NORM_EPS) * g


def _sigmoid(x):
    return 0.5 * jnp.tanh(0.5 * x) + 0.5


def _dot(a, b):
    return jnp.dot(a, b, preferred_element_type=F32)


def _dot_nt(a, b):
    return lax.dot_general(a, b, (((1,), (1,)), ((), ())), preferred_element_type=F32)


def _const_spec(shape):
    nd = len(shape)
    return pl.BlockSpec(shape, lambda *_: (0,) * nd)


def _layer_spec(layer, shape, **kw):
    nd = len(shape)
    return pl.BlockSpec((None,) + tuple(shape), lambda *_: (layer,) + (0,) * nd, **kw)


def _row_in_tile(rows):
    return lax.broadcasted_iota(jnp.int32, (rows, 1), 0) & (SUBLANES - 1)


def _pick_tile(n, want):
    t = min(n, want)
    while n % t:
        t //= 2
    return t


def _chunk_perm():
    t = np.arange(SSM_TILE)
    dst = (t % SSM_CHUNK) * (SSM_TILE // SSM_CHUNK) + t // SSM_CHUNK
    perm = np.zeros((SSM_TILE, SSM_TILE), np.float32)
    perm[dst, t] = 1.0
    return perm


def _inproj_kernel(x_ref, g_ref, w_ref, perm_ref, qkv_ref, xr_ref, gr_ref, u_ref):
    hn = _rms(x_ref[...], g_ref[...]).astype(BF16)
    bw = BRANCH_WIDTH
    qkv_ref[...] = _dot(hn, w_ref[:, 0:QKV_WIDTH]).astype(BF16)
    xr_ref[...] = _dot(hn, w_ref[:, QKV_WIDTH:QKV_WIDTH + bw]).astype(BF16)
    gr_ref[...] = _dot(hn, w_ref[:, QKV_WIDTH + bw:QKV_WIDTH + 2 * bw]).astype(BF16)
    u = _dot(hn, w_ref[:, QKV_WIDTH + 2 * bw:MAIN_COLS]).astype(BF16)
    u_ref[...] = _dot(perm_ref[...], u).astype(BF16)


def _inproj(layer, x2, g, w_main, perm):
    t = x2.shape[0]
    tm = SSM_TILE
    row = lambda w: pl.BlockSpec((tm, w), lambda i: (i, 0))
    return pl.pallas_call(
        _inproj_kernel,
        grid=(t // tm,),
        in_specs=[row(D_MODEL), _layer_spec(layer, (1, D_MODEL)),
                  _layer_spec(layer, (D_MODEL, MAIN_COLS)), _const_spec((tm, tm))],
        out_specs=[row(QKV_WIDTH), row(BRANCH_WIDTH), row(BRANCH_WIDTH), row(BRANCH_WIDTH)],
        out_shape=[jax.ShapeDtypeStruct((t, QKV_WIDTH), BF16)]
        + [jax.ShapeDtypeStruct((t, BRANCH_WIDTH), BF16)] * 3,
        compiler_params=pltpu.CompilerParams(
            dimension_semantics=("parallel",), vmem_limit_bytes=VMEM_LIMIT),
        name="inproj",
    )(x2, g, w_main, perm)


def _attn_bias():
    qi = np.arange(BLOCK)[:, None]
    kj = np.arange(3 * BLOCK)[None, :]
    dist = np.abs(qi - kj + BLOCK)
    out = np.empty((N_KV_HEADS, HEADS_PER_KV * BLOCK, 3 * BLOCK), np.float32)
    for head in range(N_Q_HEADS):
        h, g = divmod(head, HEADS_PER_KV)
        slope = 2.0 ** (-8.0 * (head + 1) / N_Q_HEADS)
        out[h, g * BLOCK:(g + 1) * BLOCK] = np.where(dist <= WINDOW, -slope * dist, 1.0)
    return out


def _attn_kernel(sink_ref, bias_ref, q_ref, kvp_ref, kvc_ref, kvn_ref, o_ref, *, nb, layer):
    n = pl.program_id(1)
    kv = jnp.concatenate([kvp_ref[...], kvc_ref[...], kvn_ref[...]], axis=0).astype(F32)
    k = kv[:, :KV_WIDTH]
    v = kv[:, KV_WIDTH:]
    lane = lax.broadcasted_iota(jnp.int32, (1, KV_WIDTH), 1)
    lo = lane < HEAD_DIM

    rows = HEADS_PER_KV * BLOCK
    kj = lax.broadcasted_iota(jnp.int32, (1, 3 * BLOCK), 1)
    in_seq = ((kj >= BLOCK) | (n > 0)) & ((kj < 2 * BLOCK) | (n < nb - 1))
    head_in_kv = lax.broadcasted_iota(jnp.int32, (rows, 1), 0) // BLOCK

    scale = HEAD_DIM ** -0.5
    for h in range(N_KV_HEADS):
        own = lo if h == 0 else jnp.logical_not(lo)
        km = jnp.where(own, k, 0.0)
        vm = jnp.where(own, v, 0.0)
        k2 = ((km + pltpu.roll(km, HEAD_DIM, 1)) * scale).astype(BF16)
        v2 = (vm + pltpu.roll(vm, HEAD_DIM, 1)).astype(BF16)
        parts = []
        sink = jnp.zeros((rows, 1), F32)
        for g in range(HEADS_PER_KV):
            head = h * HEADS_PER_KV + g
            pair = head // 2
            qp = q_ref[:, pair * KV_WIDTH:(pair + 1) * KV_WIDTH]
            keep = lo if head % 2 == 0 else jnp.logical_not(lo)
            parts.append(jnp.where(keep, qp, jnp.zeros_like(qp)))
            sink = jnp.where(head_in_kv == g, sink_ref[layer, head], sink)
        s = _dot_nt(jnp.concatenate(parts, axis=0), k2)
        bias = bias_ref[h]
        s = jnp.where((bias <= 0.0) & in_seq, s + bias, MASK_VALUE)
        m = jnp.maximum(jnp.max(s, axis=-1, keepdims=True), sink)
        p = jnp.exp(s - m)
        denom = jnp.sum(p, axis=-1, keepdims=True) + jnp.exp(sink - m)
        r = _dot(p.astype(BF16), v2) / denom
        for j in range(HEADS_PER_KV // 2):
            even = r[(2 * j) * BLOCK:(2 * j + 1) * BLOCK]
            odd = r[(2 * j + 1) * BLOCK:(2 * j + 2) * BLOCK]
            pair = h * (HEADS_PER_KV // 2) + j
            o_ref[:, pair * KV_WIDTH:(pair + 1) * KV_WIDTH] = jnp.where(lo, even, odd).astype(BF16)


def _attention(layer, qkv3, sink):
    b, l, _ = qkv3.shape
    nb = l // BLOCK
    kv_col = BRANCH_WIDTH // (2 * KV_WIDTH)
    kv_spec = lambda f: pl.BlockSpec((None, BLOCK, 2 * KV_WIDTH), lambda bi, n: (bi, f(n), kv_col))
    return pl.pallas_call(
        functools.partial(_attn_kernel, nb=nb, layer=layer),
        grid=(b, nb),
        in_specs=[
            pl.BlockSpec(memory_space=pltpu.SMEM),
            _const_spec((N_KV_HEADS, HEADS_PER_KV * BLOCK, 3 * BLOCK)),
            pl.BlockSpec((None, BLOCK, BRANCH_WIDTH), lambda bi, n: (bi, n, 0)),
            kv_spec(lambda n: jnp.maximum(n - 1, 0)),
            kv_spec(lambda n: n),
            kv_spec(lambda n: jnp.minimum(n + 1, nb - 1)),
        ],
        out_specs=pl.BlockSpec((None, BLOCK, BRANCH_WIDTH), lambda bi, n: (bi, n, 0)),
        out_shape=jax.ShapeDtypeStruct((b, l, BRANCH_WIDTH), BF16),
        compiler_params=pltpu.CompilerParams(
            dimension_semantics=("parallel", "parallel"), vmem_limit_bytes=VMEM_LIMIT),
        name="window_attn",
    )(sink, jnp.asarray(_attn_bias()), qkv3, qkv3, qkv3, qkv3)


def _tile_scan(a, u, reverse):
    rows = a.shape[0]
    rit = _row_in_tile(rows)
    for d in SCAN_STEPS:
        if reverse:
            take = rit < SUBLANES - d
            shift = rows - d
        else:
            take = rit >= d
            shift = d
        a_sh = jnp.where(take, pltpu.roll(a, shift, 0), 1.0)
        u_sh = jnp.where(take, pltpu.roll(u, shift, 0), 0.0)
        u = u + a * u_sh
        a = a * a_sh
    return a, u


def _lru_kernel(x_ref, cw_ref, cb_ref, wg_ref, bg_ref, sp_ref, o_ref, xe_s, xc_s, a_s, u_s, hf_s,
                *, seq, rc):
    phase = pl.program_id(1)
    bw = BRANCH_WIDTH
    n_tiles = seq // SUBLANES

    def conv():
        zeros = jnp.zeros((CONV_HALO, bw), F32)
        xe_s[0:CONV_HALO, :] = zeros
        xe_s[CONV_HALO + seq:CONV_HALO + seq + CONV_HALO, :] = zeros
        xe_s[CONV_HALO:CONV_HALO + seq, :] = x_ref[...].astype(F32)
        cw = cw_ref[...]
        for c in range(seq // rc):
            base = CONV_HALO + c * rc
            xc = cb_ref[...]
            for tap in range(4):
                xc = xc + cw[tap:tap + 1] * xe_s[base + tap - 2:base + tap - 2 + rc, :]
            xc_s[c * rc:(c + 1) * rc, :] = xc

    def build(reverse):
        for c in range(seq // rc):
            xc = xc_s[c * rc:(c + 1) * rc, :]
            gates = _sigmoid(_dot(xc.astype(BF16), wg_ref[...]) + bg_ref[...])
            log_a = -LRU_C * gates[:, :bw] * sp_ref[...]
            a = jnp.exp(log_a)
            t = 1.0 - a * a
            u = t * lax.rsqrt(jnp.maximum(t, SQRT_CLAMP)) * (gates[:, bw:] * xc)
            a, u = _tile_scan(a, u, reverse)
            a_s[c * rc:(c + 1) * rc, :] = a
            u_s[c * rc:(c + 1) * rc, :] = u

    @pl.when(phase == 0)
    def _():
        conv()
        build(False)

        def step(k, carry):
            r0 = pl.multiple_of(k * SUBLANES, SUBLANES)
            h = u_s[pl.ds(r0, SUBLANES), :] + a_s[pl.ds(r0, SUBLANES), :] * carry
            hf_s[pl.ds(r0, SUBLANES), :] = h
            return jnp.broadcast_to(h[SUBLANES - 1:SUBLANES, :], (SUBLANES, bw))

        lax.fori_loop(0, n_tiles, step, jnp.zeros((SUBLANES, bw), F32), unroll=4)

    @pl.when(phase == 1)
    def _():
        build(True)

        def step(k, carry):
            r0 = pl.multiple_of((n_tiles - 1 - k) * SUBLANES, SUBLANES)
            h = u_s[pl.ds(r0, SUBLANES), :] + a_s[pl.ds(r0, SUBLANES), :] * carry
            o_ref[pl.ds(r0, SUBLANES), :] = h + hf_s[pl.ds(r0, SUBLANES), :]
            return jnp.broadcast_to(h[0:1, :], (SUBLANES, bw))

        lax.fori_loop(0, n_tiles, step, jnp.zeros((SUBLANES, bw), F32), unroll=4)


def _lru(layer, xr3, conv_w, conv_b, wg, bg, sp):
    b, l, w = xr3.shape
    seq_spec = pl.BlockSpec((None, l, w), lambda bi, ph: (bi, 0, 0))
    per_dir = lambda shape: pl.BlockSpec(
        (None, None) + shape, lambda bi, ph: (layer, ph) + (0,) * len(shape))
    return pl.pallas_call(
        functools.partial(_lru_kernel, seq=l, rc=_pick_tile(l, 512)),
        grid=(b, 2),
        in_specs=[seq_spec, _layer_spec(layer, (4, w)), _layer_spec(layer, (1, w)),
                  per_dir((w, 2 * w)), per_dir((1, 2 * w)), per_dir((1, w))],
        out_specs=seq_spec,
        out_shape=jax.ShapeDtypeStruct((b, l, w), F32),
        scratch_shapes=[pltpu.VMEM((l + 2 * CONV_HALO, w), F32)] + [pltpu.VMEM((l, w), F32)] * 4,
        compiler_params=pltpu.CompilerParams(
            dimension_semantics=("parallel", "arbitrary"), vmem_limit_bytes=VMEM_LIMIT),
        name="rg_lru",
    )(xr3, conv_w, conv_b, wg, bg, sp)


def _block_diag(wb):
    nbk, bd = wb.shape[-3], wb.shape[-2]
    eye = jnp.eye(nbk, dtype=wb.dtype)
    out = jnp.einsum('...hij,hk->...hikj', wb, eye)
    return out.reshape(wb.shape[:-3] + (nbk * bd, nbk * bd))


def _split3(a):
    a1 = a.astype(BF16)
    r1 = a - a1.astype(F32)
    a2 = r1.astype(BF16)
    a3 = (r1 - a2.astype(F32)).astype(BF16)
    return a1, a2, a3


def _place(a, sel):
    p1, p2, p3 = _split3(a)
    return _dot(p1, sel) + _dot(p2, sel) + _dot(p3, sel)


def _place_nt(sel, a):
    p1, p2, p3 = _split3(a)
    return _dot_nt(sel, p1) + _dot_nt(sel, p2) + _dot_nt(sel, p3)


def _sel_cols(n_cols, base, idx_of_col):
    lane = lax.broadcasted_iota(jnp.int32, (LANES, n_cols), 0)
    col = lax.broadcasted_iota(jnp.int32, (LANES, n_cols), 1)
    return (lane == base + idx_of_col(col)).astype(BF16)


def _sel_rows(n_rows, base, idx_of_row):
    row = lax.broadcasted_iota(jnp.int32, (n_rows, LANES), 0)
    lane = lax.broadcasted_iota(jnp.int32, (n_rows, LANES), 1)
    return (lane == base + idx_of_row(row)).astype(BF16)


def _s5_build(a1_ref, a2_ref, tk_s, wsin_s, mout_s):
    q = SSM_CHUNK
    st = PAIR_STATE
    tp = TILE_PAIRS
    pw = LANES // 4
    cw = SSM_GROUP
    hp = lax.Precision.HIGHEST
    a1 = a1_ref[...].reshape(tp * st, LANES)
    a2 = a2_ref[...].reshape(tp * st, LANES)
    blk = lambda i: i >> 5
    grp = lambda i: (i >> 4) & 1
    ch = lambda i: i & (cw - 1)

    srow = lax.broadcasted_iota(jnp.int32, (tp * st, 1), 0)
    icol = lax.broadcasted_iota(jnp.int32, (1, PAIR_IN), 1)
    irow = lax.broadcasted_iota(jnp.int32, (PAIR_IN, 1), 0)
    scol = lax.broadcasted_iota(jnp.int32, (1, tp * st), 1)
    same_sc = ((srow >> 6) & 1) == grp(icol)
    same_is = grp(irow) == ((scol >> 6) & 1)
    kg_row = lax.broadcasted_iota(jnp.int32, (PAIR_CH, 1), 0)
    same_ks = grp(kg_row) == ((scol >> 6) & 1)
    erow = (lax.broadcasted_iota(jnp.int32, (PAIR_IN, PAIR_CH), 0) & (PAIR_CH - 1)
            == lax.broadcasted_iota(jnp.int32, (PAIR_IN, PAIR_CH), 1)).astype(BF16)

    c_re = _place(a2, _sel_cols(PAIR_IN, 0, ch))
    c_im = _place(a2, _sel_cols(PAIR_IN, cw, ch))

    def c_times(p_re, p_im):
        return (jnp.where(same_sc, c_re * p_re - c_im * p_im, 0.0),
                jnp.where(same_sc, -(c_re * p_im + c_im * p_re), 0.0))

    tk_fwd = None
    for d in range(2):
        pbase = 2 * d * pw
        bbase = 2 * cw + 2 * d * cw
        power = lambda f: (_place(a1, _sel_cols(PAIR_IN, pbase, f)),
                           _place(a1, _sel_cols(PAIR_IN, pbase + pw, f)))

        m_re, m_im = c_times(*power((lambda i: blk(i) + 1) if d == 0 else (lambda i: q - blk(i))))
        for pp in range(tp):
            mout_s[pp, (2 * d) * st:(2 * d + 1) * st, :] = m_re[pp * st:(pp + 1) * st].astype(BF16)
            mout_s[pp, (2 * d + 1) * st:(2 * d + 2) * st, :] = m_im[pp * st:(pp + 1) * st].astype(BF16)

        spow = (lambda i: q - 1 - blk(i)) if d == 0 else blk
        pt_re = _place_nt(_sel_rows(PAIR_IN, pbase, spow), a1)
        pt_im = _place_nt(_sel_rows(PAIR_IN, pbase + pw, spow), a1)
        bt_re = _place_nt(_sel_rows(PAIR_IN, bbase, ch), a2)
        bt_im = _place_nt(_sel_rows(PAIR_IN, bbase + cw, ch), a2)
        w_re = jnp.where(same_is, pt_re * bt_re - pt_im * bt_im, 0.0)
        w_im = jnp.where(same_is, pt_re * bt_im + pt_im * bt_re, 0.0)
        for pp in range(tp):
            wsin_s[pp, :, (2 * d) * st:(2 * d + 1) * st] = w_re[:, pp * st:(pp + 1) * st].astype(BF16)
            wsin_s[pp, :, (2 * d + 1) * st:(2 * d + 2) * st] = w_im[:, pp * st:(pp + 1) * st].astype(BF16)

        k_re, k_im = c_times(*power(blk if d == 0 else (lambda i: q - 1 - blk(i))))
        b_re = jnp.where(same_ks, _place_nt(_sel_rows(PAIR_CH, bbase, ch), a2), 0.0)
        b_im = jnp.where(same_ks, _place_nt(_sel_rows(PAIR_CH, bbase + cw, ch), a2), 0.0)
        tk_bwd = []
        for pp in range(tp):
            sl = slice(pp * st, (pp + 1) * st)
            lagk = (jnp.dot(b_re[:, sl], k_re[sl], preferred_element_type=F32, precision=hp)
                    + jnp.dot(b_im[:, sl], k_im[sl], preferred_element_type=F32, precision=hp))
            l1, l2, l3 = _split3(lagk)
            rep = _dot(erow, l1) + _dot(erow, l2) + _dot(erow, l3)
            blocks = []
            for s in range(q):
                piece = rep[s * PAIR_CH:(s + 1) * PAIR_CH]
                if d == 0:
                    shifted = piece if s == 0 else pltpu.roll(piece, s * PAIR_CH, 1)
                    blocks.append(jnp.where(blk(icol) >= s, shifted, 0.0))
                else:
                    shift = ((s + 1) * PAIR_CH) % PAIR_IN
                    shifted = piece if shift == 0 else pltpu.roll(piece, shift, 1)
                    blocks.append(jnp.where(blk(icol) <= s, shifted, 0.0))
            tk_bwd.append(blocks)
        if d == 0:
            tk_fwd = tk_bwd
        else:
            for pp in range(tp):
                for s in range(q):
                    tk_s[pp, s * PAIR_CH:(s + 1) * PAIR_CH, :] = (
                        tk_fwd[pp][s] + tk_bwd[pp][s]).astype(BF16)


def _lane_piece_mask(piece):
    lane = lax.broadcasted_iota(jnp.int32, (1, LANES), 1)
    return (lane // PAIR_CH) == piece


def _s5_kernel(u_ref, a1_ref, a2_ref, lamd_ref, lamp_ref, d_ref, y_ref,
               x_s, y_s, g_s, tk_s, wsin_s, mout_s, *, nch, nbat):
    q = SSM_CHUNK
    rows = nbat * nch
    spt = LANES // PAIR_CH

    @pl.when(pl.program_id(1) == 0)
    def _():
        _s5_build(a1_ref, a2_ref, tk_s, wsin_s, mout_s)

    step_rows = lambda s: u_ref[:, s].astype(F32).reshape(rows, LANES)
    masks = [_lane_piece_mask(j) for j in range(spt)]
    for kt in range(q // spt):
        rolled = []
        for j in range(spt):
            xs = step_rows(kt * spt + j)
            rolled.append([xs if sh == 0 else pltpu.roll(xs, sh * PAIR_CH, 1) for sh in range(spt)])
        for pp in range(TILE_PAIRS):
            tile = None
            for j in range(spt):
                src = rolled[j][(j - pp) % spt]
                tile = src if tile is None else jnp.where(masks[j], src, tile)
            x_s[pp, :, kt * LANES:(kt + 1) * LANES] = tile.astype(BF16)

    for pp in range(TILE_PAIRS):
        _s5_pair(x_s.at[pp], y_s.at[pp], g_s, tk_s.at[pp], wsin_s.at[pp], mout_s.at[pp],
                 lamd_ref.at[pp], lamp_ref.at[pp], nch=nch, nbat=nbat)

    for kt in range(q // spt):
        rolled = []
        for pp in range(TILE_PAIRS):
            yt = y_s[pp, :, kt * LANES:(kt + 1) * LANES]
            rolled.append([yt if sh == 0 else pltpu.roll(yt, sh * PAIR_CH, 1) for sh in range(spt)])
        for j in range(spt):
            tile = None
            for pp in range(TILE_PAIRS):
                src = rolled[pp][(pp - j) % spt]
                tile = src if tile is None else jnp.where(masks[pp], src, tile)
            s = kt * spt + j
            out = tile + d_ref[...] * step_rows(s)
            y_ref[:, s] = out.reshape(y_ref.shape[0], y_ref.shape[2], LANES).astype(y_ref.dtype)


def _s5_pair(x_ref, y_ref, g_s, tk_ref, wsin_ref, mout_ref, lamd_ref, lamp_ref, *, nch, nbat):
    rows = nbat * nch
    st = PAIR_STATE
    n_tiles = nch // SUBLANES
    x = x_ref[...]

    sloc = _dot(x, wsin_ref[...])
    chunk = lax.broadcasted_iota(jnp.int32, (rows, 1), 0) & (nch - 1)
    rit = _row_in_tile(rows)
    comps = []
    for comp in range(N_COMP):
        reverse = comp >= 2
        piece = sloc[:, comp * st:(comp + 1) * st]
        if reverse:
            piece = jnp.where(chunk == nch - 1, 0.0, pltpu.roll(piece, rows - 1, 0))
        else:
            piece = jnp.where(chunk == 0, 0.0, pltpu.roll(piece, 1, 0))
        comps.append(piece)

    for di, d in enumerate(SCAN_STEPS):
        for base in (0, 2):
            reverse = base == 2
            lr = lamd_ref[di * N_COMP + base:di * N_COMP + base + 1, :]
            li = lamd_ref[di * N_COMP + base + 1:di * N_COMP + base + 2, :]
            if reverse:
                take = rit < SUBLANES - d
                shift = rows - d
            else:
                take = rit >= d
                shift = d
            gr, gi = comps[base], comps[base + 1]
            sr = jnp.where(take, pltpu.roll(gr, shift, 0), 0.0)
            si = jnp.where(take, pltpu.roll(gi, shift, 0), 0.0)
            comps[base] = gr + lr * sr - li * si
            comps[base + 1] = gi + lr * si + li * sr
    for comp in range(N_COMP):
        g_s[:, comp * st:(comp + 1) * st] = comps[comp]

    pf_r, pf_i = lamp_ref[0:SUBLANES, :], lamp_ref[SUBLANES:2 * SUBLANES, :]
    pb_r, pb_i = lamp_ref[2 * SUBLANES:3 * SUBLANES, :], lamp_ref[3 * SUBLANES:4 * SUBLANES, :]

    def step(k, carry):
        out = []
        for bi in range(nbat):
            cfr, cfi, cbr, cbi = carry[4 * bi:4 * bi + 4]
            rf = pl.multiple_of(bi * nch + k * SUBLANES, SUBLANES)
            rb = pl.multiple_of(bi * nch + (n_tiles - 1 - k) * SUBLANES, SUBLANES)
            gfr = g_s[pl.ds(rf, SUBLANES), 0:st] + pf_r * cfr - pf_i * cfi
            gfi = g_s[pl.ds(rf, SUBLANES), st:2 * st] + pf_r * cfi + pf_i * cfr
            gbr = g_s[pl.ds(rb, SUBLANES), 2 * st:3 * st] + pb_r * cbr - pb_i * cbi
            gbi = g_s[pl.ds(rb, SUBLANES), 3 * st:4 * st] + pb_r * cbi + pb_i * cbr
            g_s[pl.ds(rf, SUBLANES), 0:st] = gfr
            g_s[pl.ds(rf, SUBLANES), st:2 * st] = gfi
            g_s[pl.ds(rb, SUBLANES), 2 * st:3 * st] = gbr
            g_s[pl.ds(rb, SUBLANES), 3 * st:4 * st] = gbi
            last = lambda v: jnp.broadcast_to(v[SUBLANES - 1:SUBLANES, :], (SUBLANES, st))
            first = lambda v: jnp.broadcast_to(v[0:1, :], (SUBLANES, st))
            out += [last(gfr), last(gfi), first(gbr), first(gbi)]
        return tuple(out)

    z = jnp.zeros((SUBLANES, st), F32)
    lax.fori_loop(0, n_tiles, step, (z,) * (4 * nbat))

    y_ref[...] = _dot(x, tk_ref[...]) + _dot(g_s[...].astype(BF16), mout_ref[...])


def _s5(layer, u2, a1, a2, lamd, lamp, dskip, bsz, seq, nbat):
    q = SSM_CHUNK
    nch = seq // q
    assert nch & (nch - 1) == 0 and nch % SUBLANES == 0, "chunks per sequence must be a power of two"
    rows = nbat * nch
    assert (nbat * seq) % SSM_TILE == 0
    nt = nbat * seq // SSM_TILE
    cpt = SSM_TILE // q
    st = PAIR_STATE
    tp = TILE_PAIRS
    tile_spec = lambda shape: pl.BlockSpec(
        (None, None) + shape, lambda o, i: (layer, o) + (0,) * len(shape))
    io_spec = pl.BlockSpec((nt, q, cpt, LANES), lambda o, i: (i, 0, 0, o))
    u4 = u2.reshape(u2.shape[0] // SSM_TILE, q, cpt, BRANCH_WIDTH)
    return pl.pallas_call(
        functools.partial(_s5_kernel, nch=nch, nbat=nbat),
        grid=(BRANCH_WIDTH // LANES, bsz // nbat),
        in_specs=[
            io_spec,
            tile_spec((tp, st, LANES)),
            tile_spec((tp, st, LANES)),
            tile_spec((tp, len(SCAN_STEPS) * N_COMP, st)),
            tile_spec((tp, N_COMP * SUBLANES, st)),
            tile_spec((1, LANES)),
        ],
        out_specs=io_spec,
        out_shape=jax.ShapeDtypeStruct(u4.shape, BF16),
        scratch_shapes=[pltpu.VMEM((tp, rows, PAIR_IN), BF16),
                        pltpu.VMEM((tp, rows, PAIR_IN), F32),
                        pltpu.VMEM((rows, N_COMP * st), F32),
                        pltpu.VMEM((tp, PAIR_IN, PAIR_IN), BF16),
                        pltpu.VMEM((tp, PAIR_IN, N_COMP * st), BF16),
                        pltpu.VMEM((tp, N_COMP * st, PAIR_IN), BF16)],
        compiler_params=pltpu.CompilerParams(
            dimension_semantics=("parallel", "arbitrary"), vmem_limit_bytes=VMEM_LIMIT),
        name="s5_chunked",
    )(u4, a1, a2, lamd, lamp, dskip).reshape(u2.shape)


def _s5_params(a_re, a_im, log_step, b_re, b_im, c_re, c_im, d_skip):
    q = SSM_CHUNK
    g, n, c = SSM_GROUPS, SSM_STATE, SSM_GROUP
    no, og = SSM_PAIRS, 2
    step = jnp.exp(log_step)[..., None]
    dr = a_re * step
    di = a_im * step
    mag = jnp.exp(dr)
    abar_r = mag * jnp.cos(di)
    abar_i = mag * jnp.sin(di)
    den = a_re * a_re + a_im * a_im
    nr = abar_r - 1.0
    sr = (nr * a_re + abar_i * a_im) / den
    si = (abar_i * a_re - nr * a_im) / den
    bb_r = sr[..., None] * b_re - si[..., None] * b_im
    bb_i = sr[..., None] * b_im + si[..., None] * b_re

    def powers(kk):
        kk = jnp.asarray(kk, F32)
        pmag = jnp.exp(dr[..., None] * kk)
        return pmag * jnp.cos(di[..., None] * kk), pmag * jnp.sin(di[..., None] * kk)

    pr, pi = powers(jnp.arange(LANES // 4))
    a1 = jnp.concatenate([pr[0], pi[0], pr[1], pi[1]], axis=-1).reshape(no, PAIR_STATE, LANES)
    by_state = lambda w_: w_.reshape(no, PAIR_STATE, c)
    a2 = jnp.concatenate(
        [by_state(c_re.transpose(0, 2, 1)), by_state(c_im.transpose(0, 2, 1)),
         by_state(bb_r[0]), by_state(bb_i[0]), by_state(bb_r[1]), by_state(bb_i[1]),
         jnp.zeros((no, PAIR_STATE, LANES - 6 * c), F32)], axis=-1)

    def scan_rows(kk):
        sr_, si_ = powers(q * jnp.asarray(kk))
        rows = jnp.stack([sr_[0], si_[0], sr_[1], si_[1]], axis=0)
        return rows.reshape(N_COMP, no, og, n, len(kk))

    lamd = scan_rows(SCAN_STEPS).transpose(1, 4, 0, 2, 3).reshape(
        no, len(SCAN_STEPS) * N_COMP, PAIR_STATE)
    fwd = scan_rows(list(range(1, SUBLANES + 1)))[:2]
    bwd = scan_rows(list(range(SUBLANES, 0, -1)))[2:]
    lamp = jnp.concatenate([fwd, bwd], axis=0).transpose(1, 0, 4, 2, 3).reshape(
        no, N_COMP * SUBLANES, PAIR_STATE)
    dskip = d_skip.reshape(BRANCH_WIDTH // LANES, 1, LANES)
    by_tile = lambda w_: w_.reshape((BRANCH_WIDTH // LANES, TILE_PAIRS) + w_.shape[1:])
    return by_tile(a1), by_tile(a2), by_tile(lamd), by_tile(lamp), dskip


def _merge_kernel(x_ref, g_ref, oa_ref, hs_ref, gr_ref, ys_ref, unperm_ref, wgate_ref, wb_ref,
                  wglu_ref, bglu_ref, wout_ref, o_ref):
    x = x_ref[...]
    hn = _rms(x, g_ref[...]).astype(BF16)
    bw = BRANCH_WIDTH
    o_lru = (jax.nn.gelu(gr_ref[...].astype(F32)) * hs_ref[...]).astype(BF16)
    zg = _dot(jax.nn.gelu(ys_ref[...].astype(F32)).astype(BF16), wglu_ref[...]) + bglu_ref[...]
    o_ssm = (zg[:, :bw] * _sigmoid(zg[:, bw:])).astype(BF16)
    o_ssm = _dot(unperm_ref[...], o_ssm).astype(BF16)
    mixed = None
    for kb, ob in enumerate((oa_ref[...], o_lru, o_ssm)):
        gate = _sigmoid(_dot(hn, wgate_ref[:, kb * D_MODEL:(kb + 1) * D_MODEL]))
        term = _dot(ob, wb_ref[kb]) * gate
        mixed = term if mixed is None else mixed + term
    o_ref[...] = x + _dot(mixed.astype(BF16), wout_ref[...])


def _merge(layer, x2, g, o_attn, h_sum, gr, y_ssm, unperm, w_gate, w_branch, w_glu, b_glu, w_out):
    t = x2.shape[0]
    tm = SSM_TILE
    row = lambda w: pl.BlockSpec((tm, w), lambda i: (i, 0))
    return pl.pallas_call(
        _merge_kernel,
        grid=(t // tm,),
        in_specs=[row(D_MODEL), _layer_spec(layer, (1, D_MODEL)), row(BRANCH_WIDTH),
                  row(BRANCH_WIDTH), row(BRANCH_WIDTH), row(BRANCH_WIDTH), _const_spec((tm, tm)),
                  _layer_spec(layer, (D_MODEL, N_BRANCH * D_MODEL)),
                  _layer_spec(layer, (N_BRANCH, BRANCH_WIDTH, D_MODEL)),
                  _layer_spec(layer, (BRANCH_WIDTH, 2 * BRANCH_WIDTH)),
                  _layer_spec(layer, (1, 2 * BRANCH_WIDTH)),
                  _layer_spec(layer, (D_MODEL, D_MODEL))],
        out_specs=row(D_MODEL),
        out_shape=jax.ShapeDtypeStruct((t, D_MODEL), F32),
        compiler_params=pltpu.CompilerParams(
            dimension_semantics=("parallel",), vmem_limit_bytes=VMEM_LIMIT),
        name="merge",
    )(x2, g, o_attn, h_sum, gr, y_ssm, unperm, w_gate, w_branch, w_glu, b_glu, w_out)


def _route(logits):
    lane = lax.broadcasted_iota(jnp.int32, logits.shape, 1).astype(F32)
    neg = -jnp.inf
    first = lambda mask: jnp.min(jnp.where(mask, lane, float(ROUTE_COLS)), axis=-1, keepdims=True)
    gmask = lane < N_GROUPS
    gmax = jnp.max(jnp.where(gmask, logits, neg), axis=-1, keepdims=True)
    gidx = first(gmask & (logits == gmax))
    gsum = jnp.sum(jnp.where(gmask, jnp.exp(logits - gmax), 0.0), axis=-1, keepdims=True)
    g_w = 1.0 / gsum
    e_lo = N_GROUPS + EXPERTS_PER_GROUP * gidx
    emask = (lane >= e_lo) & (lane < e_lo + EXPERTS_PER_GROUP)
    v1 = jnp.max(jnp.where(emask, logits, neg), axis=-1, keepdims=True)
    i1 = first(emask & (logits == v1))
    emask2 = emask & (lane != i1)
    v2 = jnp.max(jnp.where(emask2, logits, neg), axis=-1, keepdims=True)
    i2 = first(emask2 & (logits == v2))
    t = jnp.exp(v2 - v1)
    w1 = 1.0 / (1.0 + t)
    w2 = t * w1
    cw = jnp.where(lane == i1, w1 * g_w, jnp.where(lane == i2, w2 * g_w, 0.0))
    return cw, jnp.where(lane == gidx, 1.0, 0.0)


def _moe_kernel(x_ref, g_ref, wrh_ref, wrl_ref, br_ref, tril_ref, w13_ref, w2_ref, fg_ref, o_ref,
                xs_s, cws_s, ys_s, *, final_norm):
    x = x_ref[...]
    tm = x.shape[0]
    ns = xs_s.shape[0]
    hn = _rms(x, g_ref[...])
    hb = hn.astype(BF16)
    lo = (hn - hb.astype(F32)).astype(BF16)
    logits = (_dot(hb, wrh_ref[...]) + _dot(hb, wrl_ref[...]) + _dot(lo, wrh_ref[...])
              + br_ref[...])
    cw, ghot = _route(logits)

    lane = lax.broadcasted_iota(jnp.int32, (1, ROUTE_COLS), 1)
    rank = _dot(tril_ref[...], ghot.astype(BF16))
    total = jnp.sum(ghot, axis=0, keepdims=True)
    counts, offsets = [], []
    off = jnp.int32(0)
    for grp in range(N_GROUPS):
        cnt = jnp.sum(jnp.where(lane == grp, total, 0.0)).astype(jnp.int32)
        counts.append(cnt)
        offsets.append(off)
        off = off + ((cnt + (MOE_ALIGN - 1)) // MOE_ALIGN) * MOE_ALIGN
    off_vec = jnp.zeros((1, ROUTE_COLS), F32)
    for grp in range(1, N_GROUPS):
        off_vec = jnp.where(lane == grp, offsets[grp].astype(F32), off_vec)
    pos_terms = ghot * (rank + off_vec)
    pos_col = jnp.sum(pos_terms, axis=-1, keepdims=True)
    pos_row = _place_nt(jnp.ones((SUBLANES, ROUTE_COLS), BF16), pos_terms)[0:1]
    sort = (lax.broadcasted_iota(jnp.int32, (ns, tm), 0).astype(F32) == pos_row).astype(BF16)
    unsort = (lax.broadcasted_iota(jnp.int32, (tm, ns), 1).astype(F32) == pos_col).astype(BF16)

    xs_s[...] = _dot(sort, hb).astype(BF16)
    c1, c2, c3 = _split3(cw)
    cws_s[...] = _dot(sort, c1) + _dot(sort, c2) + _dot(sort, c3)
    ys_s[...] = jnp.zeros_like(ys_s)

    for grp in range(N_GROUPS):
        def chunk(k, carry, grp=grp):
            r0 = pl.multiple_of(offsets[grp] + k * MOE_CHUNK, MOE_ALIGN)
            h = _dot(xs_s[pl.ds(r0, MOE_CHUNK), :], w13_ref[grp])
            cwb = cws_s[pl.ds(r0, MOE_CHUNK), :]
            cols = []
            for e in range(EXPERTS_PER_GROUP):
                col = N_GROUPS + grp * EXPERTS_PER_GROUP + e
                wcol = jnp.sum(jnp.where(lane == col, cwb, 0.0), axis=-1, keepdims=True)
                cols.append(jnp.broadcast_to(wcol, (MOE_CHUNK, D_EXPERT)))
            h1 = h[:, :GROUP_HIDDEN]
            hg = (h1 * _sigmoid(h1) * h[:, GROUP_HIDDEN:] * jnp.concatenate(cols, axis=1))
            ys_s[pl.ds(r0, MOE_CHUNK), :] = _dot(hg.astype(BF16), w2_ref[grp])
            return carry

        lax.fori_loop(0, (counts[grp] + (MOE_CHUNK - 1)) // MOE_CHUNK, chunk, 0)

    y = x + _dot(unsort, ys_s[...].astype(BF16))
    if final_norm:
        y = _rms(y, fg_ref[...])
    o_ref[...] = y


def _moe(layer, x2, g, wr_hi, wr_lo, b_route, w13, w2, final_g, tm, final_norm):
    t = x2.shape[0]
    ns = -(-(tm + N_GROUPS * (MOE_ALIGN - 1) + MOE_CHUNK) // LANES) * LANES
    row = pl.BlockSpec((tm, D_MODEL), lambda i: (i, 0))
    once = pl.Buffered(1)
    tril = jnp.asarray(np.tril(np.ones((tm, tm), np.float32), -1), BF16)
    return pl.pallas_call(
        functools.partial(_moe_kernel, final_norm=final_norm),
        grid=(t // tm,),
        in_specs=[row,
                  _layer_spec(layer, (1, D_MODEL)),
                  _layer_spec(layer, (D_MODEL, ROUTE_COLS)),
                  _layer_spec(layer, (D_MODEL, ROUTE_COLS)),
                  _layer_spec(layer, (1, ROUTE_COLS)),
                  _const_spec((tm, tm)),
                  _layer_spec(layer, (N_GROUPS, D_MODEL, 2 * GROUP_HIDDEN), pipeline_mode=once),
                  _layer_spec(layer, (N_GROUPS, GROUP_HIDDEN, D_MODEL), pipeline_mode=once),
                  _const_spec((1, D_MODEL))],
        out_specs=row,
        out_shape=jax.ShapeDtypeStruct((t, D_MODEL), F32),
        scratch_shapes=[pltpu.VMEM((ns, D_MODEL), BF16), pltpu.VMEM((ns, ROUTE_COLS), F32),
                        pltpu.VMEM((ns, D_MODEL), F32)],
        compiler_params=pltpu.CompilerParams(
            dimension_semantics=("parallel",), vmem_limit_bytes=VMEM_LIMIT),
        name="moe_final" if final_norm else "moe",
    )(x2, g, wr_hi, wr_lo, b_route, tril, w13, w2, final_g)


def _prepare(p):
    w_in = p['w_in']
    depth = w_in.shape[0]
    pad = ROUTE_COLS - N_GROUPS - N_EXPERTS
    w_route = jnp.concatenate(
        [p['moe_w_group'], p['moe_w_expert'], jnp.zeros((depth, D_MODEL, pad), F32)], axis=-1)
    wr_hi = w_route.astype(BF16)
    grp = lambda w: w.transpose(0, 1, 3, 2, 4).reshape(depth, N_GROUPS, D_MODEL, GROUP_HIDDEN)
    s5_a1, s5_a2, lamd, lamp, dskip = jax.vmap(_s5_params)(
        p['ssm_a_re'], p['ssm_a_im'], p['ssm_log_step'], p['ssm_b_re'], p['ssm_b_im'],
        p['ssm_c_re'], p['ssm_c_im'], p['ssm_d'])
    return dict(
        norm1_g=p['norm1_g'][:, None],
        w_main=w_in[:, :, :MAIN_COLS].astype(BF16),
        w_gate=w_in[:, :, MAIN_COLS:].astype(BF16),
        attn_sink=p['attn_sink'],
        conv_w=p['lru_conv_w'],
        conv_b=p['lru_conv_b'][:, None],
        lru_wg=jnp.concatenate([_block_diag(p['lru_w_r']), _block_diag(p['lru_w_i'])],
                               axis=-1).astype(BF16),
        lru_bg=jnp.concatenate([p['lru_b_r'], p['lru_b_i']], axis=-1)[:, :, None],
        lru_sp=jax.nn.softplus(-p['lru_lambda'])[:, :, None],
        s5_a1=s5_a1, s5_a2=s5_a2, lamd=lamd, lamp=lamp, dskip=dskip,
        w_branch=p['w_branch'].astype(BF16),
        w_glu=p['ssm_w_glu'].astype(BF16),
        b_glu=p['ssm_b_glu'][:, None],
        w_out=p['w_out'].astype(BF16),
        norm2_g=p['norm2_g'][:, None],
        wr_hi=wr_hi,
        wr_lo=(w_route - wr_hi.astype(F32)).astype(BF16),
        b_route=jnp.concatenate(
            [p['moe_b_group'], p['moe_b_expert'], jnp.zeros((depth, pad), F32)], axis=-1)[:, None],
        w13=jnp.concatenate([grp(p['moe_w1']), grp(p['moe_w3'])], axis=-1).astype(BF16),
        w2=p['moe_w2'].reshape(depth, N_GROUPS, GROUP_HIDDEN, D_MODEL).astype(BF16),
    )


def _layer(layer, x2, bsz, seq, w, final_g, final_norm):
    t = bsz * seq
    bw = BRANCH_WIDTH
    assert t % SSM_TILE == 0
    perm = _chunk_perm()
    qkv, xr, gr, u = _inproj(layer, x2, w['norm1_g'], w['w_main'], jnp.asarray(perm, BF16))
    o_attn = _attention(layer, qkv.reshape(bsz, seq, QKV_WIDTH), w['attn_sink']).reshape(t, bw)
    h_sum = _lru(layer, xr.reshape(bsz, seq, bw), w['conv_w'], w['conv_b'], w['lru_wg'],
                 w['lru_bg'], w['lru_sp']).reshape(t, bw)
    y_ssm = _s5(layer, u, w['s5_a1'], w['s5_a2'], w['lamd'], w['lamp'], w['dskip'],
                bsz, seq, _pick_tile(bsz, 4))
    x2 = _merge(layer, x2, w['norm1_g'], o_attn, h_sum, gr, y_ssm, jnp.asarray(perm.T, BF16),
                w['w_gate'], w['w_branch'], w['w_glu'], w['b_glu'], w['w_out'])
    return _moe(layer, x2, w['norm2_g'], w['wr_hi'], w['wr_lo'], w['b_route'], w['w13'], w['w2'],
                final_g, _pick_tile(t, 512), final_norm)


_PARAM_KEYS = ('norm1_g', 'w_in', 'attn_sink', 'lru_conv_w', 'lru_conv_b', 'lru_w_r', 'lru_b_r',
               'lru_w_i', 'lru_b_i', 'lru_lambda', 'ssm_a_re', 'ssm_a_im', 'ssm_log_step',
               'ssm_b_re', 'ssm_b_im', 'ssm_c_re', 'ssm_c_im', 'ssm_d', 'ssm_w_glu', 'ssm_b_glu',
               'w_branch', 'w_out', 'norm2_g', 'moe_w_group', 'moe_b_group', 'moe_w_expert',
               'moe_b_expert', 'moe_w1', 'moe_w3', 'moe_w2')


def kernel(x, norm1_g, w_in, attn_sink, lru_conv_w, lru_conv_b, lru_w_r, lru_b_r, lru_w_i, lru_b_i,
           lru_lambda, ssm_a_re, ssm_a_im, ssm_log_step, ssm_b_re, ssm_b_im, ssm_c_re, ssm_c_im,
           ssm_d, ssm_w_glu, ssm_b_glu, w_branch, w_out, norm2_g, moe_w_group, moe_b_group,
           moe_w_expert, moe_b_expert, moe_w1, moe_w3, moe_w2, final_norm_g):
    params = dict(zip(_PARAM_KEYS, (
        norm1_g, w_in, attn_sink, lru_conv_w, lru_conv_b, lru_w_r, lru_b_r, lru_w_i, lru_b_i,
        lru_lambda, ssm_a_re, ssm_a_im, ssm_log_step, ssm_b_re, ssm_b_im, ssm_c_re, ssm_c_im,
        ssm_d, ssm_w_glu, ssm_b_glu, w_branch, w_out, norm2_g, moe_w_group, moe_b_group,
        moe_w_expert, moe_b_expert, moe_w1, moe_w3, moe_w2)))
    bsz, seq, _ = x.shape
    depth = norm1_g.shape[0]
    w = _prepare(params)
    x2 = x.reshape(bsz * seq, D_MODEL)
    for layer in range(depth):
        x2 = _layer(layer, x2, bsz, seq, w, final_norm_g[None], final_norm=(layer == depth - 1))
    return x2.reshape(bsz, seq, D_MODEL)
```

```python
import functools

import jax
import jax.numpy as jnp
import numpy as np
from jax import lax
from jax.experimental import pallas as pl
from jax.experimental.pallas import tpu as pltpu

F32 = jnp.float32
BF16 = jnp.bfloat16

D_MODEL = 1024
N_Q_HEADS = 8
N_KV_HEADS = 2
HEAD_DIM = 64
WINDOW = 128
BLOCK = 128
BRANCH_WIDTH = 512
KV_WIDTH = N_KV_HEADS * HEAD_DIM
LRU_BLOCKS = 8
LRU_C = 8.0
SSM_GROUP = 16
SSM_GROUPS = BRANCH_WIDTH // SSM_GROUP
SSM_STATE = 64
N_BRANCH = 3
N_GROUPS = 4
EXPERTS_PER_GROUP = 4
N_EXPERTS = N_GROUPS * EXPERTS_PER_GROUP
D_EXPERT = 256
GROUP_HIDDEN = EXPERTS_PER_GROUP * D_EXPERT
NORM_EPS = 1e-6
MASK_VALUE = -1e30

LANES = 128
SUBLANES = 8
QKV_WIDTH = BRANCH_WIDTH + 2 * KV_WIDTH
MAIN_COLS = QKV_WIDTH + 3 * BRANCH_WIDTH
HEADS_PER_KV = N_Q_HEADS // N_KV_HEADS
SSM_CHUNK = 16
PAIR_CH = 2 * SSM_GROUP
SSM_PAIRS = SSM_GROUPS // 2
TILE_PAIRS = LANES // PAIR_CH
PAIR_IN = SSM_CHUNK * PAIR_CH
PAIR_STATE = 2 * SSM_STATE
SSM_TILE = 512
N_COMP = 4
SCAN_STEPS = (1, 2, 4)
ROUTE_COLS = LANES
MOE_ALIGN = 16
MOE_CHUNK = 160
CONV_HALO = 16
SQRT_CLAMP = 1e-30
VMEM_LIMIT = 56 * 1024 * 1024


def _rms(x, g):
    ms = jnp.mean(x * x, axis=-1, keepdims=True)
    return x * lax.rsqrt(ms + NORM_EPS) * g


def _sigmoid(x):
    return 0.5 * jnp.tanh(0.5 * x) + 0.5


def _dot(a, b):
    return jnp.dot(a, b, preferred_element_type=F32)


def _dot_nt(a, b):
    return lax.dot_general(a, b, (((1,), (1,)), ((), ())), preferred_element_type=F32)


def _const_spec(shape):
    nd = len(shape)
    return pl.BlockSpec(shape, lambda *_: (0,) * nd)


def _layer_spec(layer, shape, **kw):
    nd = len(shape)
    return pl.BlockSpec((None,) + tuple(shape), lambda *_: (layer,) + (0,) * nd, **kw)


def _row_in_tile(rows):
    return lax.broadcasted_iota(jnp.int32, (rows, 1), 0) & (SUBLANES - 1)


def _pick_tile(n, want):
    t = min(n, want)
    while n % t:
        t //= 2
    return t


def _chunk_perm():
    t = np.arange(SSM_TILE)
    dst = (t % SSM_CHUNK) * (SSM_TILE // SSM_CHUNK) + t // SSM_CHUNK
    perm = np.zeros((SSM_TILE, SSM_TILE), np.float32)
    perm[dst, t] = 1.0
    return perm


def _inproj_kernel(x_ref, g_ref, w_ref, perm_ref, qkv_ref, xr_ref, gr_ref, u_ref):
    hn = _rms(x_ref[...], g_ref[...]).astype(BF16)
    bw = BRANCH_WIDTH
    qkv_ref[...] = _dot(hn, w_ref[:, 0:QKV_WIDTH]).astype(BF16)
    xr_ref[...] = _dot(hn, w_ref[:, QKV_WIDTH:QKV_WIDTH + bw]).astype(BF16)
    gr_ref[...] = _dot(hn, w_ref[:, QKV_WIDTH + bw:QKV_WIDTH + 2 * bw]).astype(BF16)
    u = _dot(hn, w_ref[:, QKV_WIDTH + 2 * bw:MAIN_COLS]).astype(BF16)
    u_ref[...] = _dot(perm_ref[...], u).astype(BF16)


def _inproj(layer, x2, g, w_main, perm):
    t = x2.shape[0]
    tm = SSM_TILE
    row = lambda w: pl.BlockSpec((tm, w), lambda i: (i, 0))
    return pl.pallas_call(
        _inproj_kernel,
        grid=(t // tm,),
        in_specs=[row(D_MODEL), _layer_spec(layer, (1, D_MODEL)),
                  _layer_spec(layer, (D_MODEL, MAIN_COLS)), _const_spec((tm, tm))],
        out_specs=[row(QKV_WIDTH), row(BRANCH_WIDTH), row(BRANCH_WIDTH), row(BRANCH_WIDTH)],
        out_shape=[jax.ShapeDtypeStruct((t, QKV_WIDTH), BF16)]
        + [jax.ShapeDtypeStruct((t, BRANCH_WIDTH), BF16)] * 3,
        compiler_params=pltpu.CompilerParams(
            dimension_semantics=("parallel",), vmem_limit_bytes=VMEM_LIMIT),
        name="inproj",
    )(x2, g, w_main, perm)


def _attn_bias():
    qi = np.arange(BLOCK)[:, None]
    kj = np.arange(3 * BLOCK)[None, :]
    dist = np.abs(qi - kj + BLOCK)
    out = np.empty((N_KV_HEADS, HEADS_PER_KV * BLOCK, 3 * BLOCK), np.float32)
    for head in range(N_Q_HEADS):
        h, g = divmod(head, HEADS_PER_KV)
        slope = 2.0 ** (-8.0 * (head + 1) / N_Q_HEADS)
        out[h, g * BLOCK:(g + 1) * BLOCK] = np.where(dist <= WINDOW, -slope * dist, 1.0)
    return out


def _attn_kernel(sink_ref, bias_ref, q_ref, kvp_ref, kvc_ref, kvn_ref, o_ref, *, nb, layer):
    n = pl.program_id(1)
    kv = jnp.concatenate([kvp_ref[...], kvc_ref[...], kvn_ref[...]], axis=0).astype(F32)
    k = kv[:, :KV_WIDTH]
    v = kv[:, KV_WIDTH:]
    lane = lax.broadcasted_iota(jnp.int32, (1, KV_WIDTH), 1)
    lo = lane < HEAD_DIM

    rows = HEADS_PER_KV * BLOCK
    kj = lax.broadcasted_iota(jnp.int32, (1, 3 * BLOCK), 1)
    in_seq = ((kj >= BLOCK) | (n > 0)) & ((kj < 2 * BLOCK) | (n < nb - 1))
    head_in_kv = lax.broadcasted_iota(jnp.int32, (rows, 1), 0) // BLOCK

    scale = HEAD_DIM ** -0.5
    for h in range(N_KV_HEADS):
        own = lo if h == 0 else jnp.logical_not(lo)
        km = jnp.where(own, k, 0.0)
        vm = jnp.where(own, v, 0.0)
        k2 = ((km + pltpu.roll(km, HEAD_DIM, 1)) * scale).astype(BF16)
        v2 = (vm + pltpu.roll(vm, HEAD_DIM, 1)).astype(BF16)
        parts = []
        sink = jnp.zeros((rows, 1), F32)
        for g in range(HEADS_PER_KV):
            head = h * HEADS_PER_KV + g
            pair = head // 2
            qp = q_ref[:, pair * KV_WIDTH:(pair + 1) * KV_WIDTH]
            keep = lo if head % 2 == 0 else jnp.logical_not(lo)
            parts.append(jnp.where(keep, qp, jnp.zeros_like(qp)))
            sink = jnp.where(head_in_kv == g, sink_ref[layer, head], sink)
        s = _dot_nt(jnp.concatenate(parts, axis=0), k2)
        bias = bias_ref[h]
        s = jnp.where((bias <= 0.0) & in_seq, s + bias, MASK_VALUE)
        m = jnp.maximum(jnp.max(s, axis=-1, keepdims=True), sink)
        p = jnp.exp(s - m)
        denom = jnp.sum(p, axis=-1, keepdims=True) + jnp.exp(sink - m)
        r = _dot(p.astype(BF16), v2) / denom
        for j in range(HEADS_PER_KV // 2):
            even = r[(2 * j) * BLOCK:(2 * j + 1) * BLOCK]
            odd = r[(2 * j + 1) * BLOCK:(2 * j + 2) * BLOCK]
            pair = h * (HEADS_PER_KV // 2) + j
            o_ref[:, pair * KV_WIDTH:(pair + 1) * KV_WIDTH] = jnp.where(lo, even, odd).astype(BF16)


def _attention(layer, qkv3, sink):
    b, l, _ = qkv3.shape
    nb = l // BLOCK
    kv_col = BRANCH_WIDTH // (2 * KV_WIDTH)
    kv_spec = lambda f: pl.BlockSpec((None, BLOCK, 2 * KV_WIDTH), lambda bi, n: (bi, f(n), kv_col))
    return pl.pallas_call(
        functools.partial(_attn_kernel, nb=nb, layer=layer),
        grid=(b, nb),
        in_specs=[
            pl.BlockSpec(memory_space=pltpu.SMEM),
            _const_spec((N_KV_HEADS, HEADS_PER_KV * BLOCK, 3 * BLOCK)),
            pl.BlockSpec((None, BLOCK, BRANCH_WIDTH), lambda bi, n: (bi, n, 0)),
            kv_spec(lambda n: jnp.maximum(n - 1, 0)),
            kv_spec(lambda n: n),
            kv_spec(lambda n: jnp.minimum(n + 1, nb - 1)),
        ],
        out_specs=pl.BlockSpec((None, BLOCK, BRANCH_WIDTH), lambda bi, n: (bi, n, 0)),
        out_shape=jax.ShapeDtypeStruct((b, l, BRANCH_WIDTH), BF16),
        compiler_params=pltpu.CompilerParams(
            dimension_semantics=("parallel", "parallel"), vmem_limit_bytes=VMEM_LIMIT),
        name="window_attn",
    )(sink, jnp.asarray(_attn_bias()), qkv3, qkv3, qkv3, qkv3)


def _tile_scan(a, u, reverse):
    rows = a.shape[0]
    rit = _row_in_tile(rows)
    for d in SCAN_STEPS:
        if reverse:
            take = rit < SUBLANES - d
            shift = rows - d
        else:
            take = rit >= d
            shift = d
        a_sh = jnp.where(take, pltpu.roll(a, shift, 0), 1.0)
        u_sh = jnp.where(take, pltpu.roll(u, shift, 0), 0.0)
        u = u + a * u_sh
        a = a * a_sh
    return a, u


def _lru_kernel(x_ref, cw_ref, cb_ref, wg_ref, bg_ref, sp_ref, o_ref, xe_s, xc_s, a_s, u_s, hf_s,
                *, seq, rc):
    phase = pl.program_id(1)
    bw = BRANCH_WIDTH
    n_tiles = seq // SUBLANES

    def conv():
        zeros = jnp.zeros((CONV_HALO, bw), F32)
        xe_s[0:CONV_HALO, :] = zeros
        xe_s[CONV_HALO + seq:CONV_HALO + seq + CONV_HALO, :] = zeros
        xe_s[CONV_HALO:CONV_HALO + seq, :] = x_ref[...].astype(F32)
        cw = cw_ref[...]
        for c in range(seq // rc):
            base = CONV_HALO + c * rc
            xc = cb_ref[...]
            for tap in range(4):
                xc = xc + cw[tap:tap + 1] * xe_s[base + tap - 2:base + tap - 2 + rc, :]
            xc_s[c * rc:(c + 1) * rc, :] = xc

    def build(reverse):
        for c in range(seq // rc):
            xc = xc_s[c * rc:(c + 1) * rc, :]
            gates = _sigmoid(_dot(xc.astype(BF16), wg_ref[...]) + bg_ref[...])
            log_a = -LRU_C * gates[:, :bw] * sp_ref[...]
            a = jnp.exp(log_a)
            t = 1.0 - a * a
            u = t * lax.rsqrt(jnp.maximum(t, SQRT_CLAMP)) * (gates[:, bw:] * xc)
            a, u = _tile_scan(a, u, reverse)
            a_s[c * rc:(c + 1) * rc, :] = a
            u_s[c * rc:(c + 1) * rc, :] = u

    @pl.when(phase == 0)
    def _():
        conv()
        build(False)

        def step(k, carry):
            r0 = pl.multiple_of(k * SUBLANES, SUBLANES)
            h = u_s[pl.ds(r0, SUBLANES), :] + a_s[pl.ds(r0, SUBLANES), :] * carry
            hf_s[pl.ds(r0, SUBLANES), :] = h
            return jnp.broadcast_to(h[SUBLANES - 1:SUBLANES, :], (SUBLANES, bw))

        lax.fori_loop(0, n_tiles, step, jnp.zeros((SUBLANES, bw), F32), unroll=4)

    @pl.when(phase == 1)
    def _():
        build(True)

        def step(k, carry):
            r0 = pl.multiple_of((n_tiles - 1 - k) * SUBLANES, SUBLANES)
            h = u_s[pl.ds(r0, SUBLANES), :] + a_s[pl.ds(r0, SUBLANES), :] * carry
            o_ref[pl.ds(r0, SUBLANES), :] = h + hf_s[pl.ds(r0, SUBLANES), :]
            return jnp.broadcast_to(h[0:1, :], (SUBLANES, bw))

        lax.fori_loop(0, n_tiles, step, jnp.zeros((SUBLANES, bw), F32), unroll=4)


def _lru(layer, xr3, conv_w, conv_b, wg, bg, sp):
    b, l, w = xr3.shape
    seq_spec = pl.BlockSpec((None, l, w), lambda bi, ph: (bi, 0, 0))
    per_dir = lambda shape: pl.BlockSpec(
        (None, None) + shape, lambda bi, ph: (layer, ph) + (0,) * len(shape))
    return pl.pallas_call(
        functools.partial(_lru_kernel, seq=l, rc=_pick_tile(l, 512)),
        grid=(b, 2),
        in_specs=[seq_spec, _layer_spec(layer, (4, w)), _layer_spec(layer, (1, w)),
                  per_dir((w, 2 * w)), per_dir((1, 2 * w)), per_dir((1, w))],
        out_specs=seq_spec,
        out_shape=jax.ShapeDtypeStruct((b, l, w), F32),
        scratch_shapes=[pltpu.VMEM((l + 2 * CONV_HALO, w), F32)] + [pltpu.VMEM((l, w), F32)] * 4,
        compiler_params=pltpu.CompilerParams(
            dimension_semantics=("parallel", "arbitrary"), vmem_limit_bytes=VMEM_LIMIT),
        name="rg_lru",
    )(xr3, conv_w, conv_b, wg, bg, sp)


def _block_diag(wb):
    nbk, bd = wb.shape[-3], wb.shape[-2]
    eye = jnp.eye(nbk, dtype=wb.dtype)
    out = jnp.einsum('...hij,hk->...hikj', wb, eye)
    return out.reshape(wb.shape[:-3] + (nbk * bd, nbk * bd))


def _split3(a):
    a1 = a.astype(BF16)
    r1 = a - a1.astype(F32)
    a2 = r1.astype(BF16)
    a3 = (r1 - a2.astype(F32)).astype(BF16)
    return a1, a2, a3


def _place(a, sel):
    p1, p2, p3 = _split3(a)
    return _dot(p1, sel) + _dot(p2, sel) + _dot(p3, sel)


def _place_nt(sel, a):
    p1, p2, p3 = _split3(a)
    return _dot_nt(sel, p1) + _dot_nt(sel, p2) + _dot_nt(sel, p3)


def _sel_cols(n_cols, base, idx_of_col):
    lane = lax.broadcasted_iota(jnp.int32, (LANES, n_cols), 0)
    col = lax.broadcasted_iota(jnp.int32, (LANES, n_cols), 1)
    return (lane == base + idx_of_col(col)).astype(BF16)


def _sel_rows(n_rows, base, idx_of_row):
    row = lax.broadcasted_iota(jnp.int32, (n_rows, LANES), 0)
    lane = lax.broadcasted_iota(jnp.int32, (n_rows, LANES), 1)
    return (lane == base + idx_of_row(row)).astype(BF16)


def _s5_build(a1_ref, a2_ref, tk_s, wsin_s, mout_s):
    q = SSM_CHUNK
    st = PAIR_STATE
    tp = TILE_PAIRS
    pw = LANES // 4
    cw = SSM_GROUP
    hp = lax.Precision.HIGHEST
    a1 = a1_ref[...].reshape(tp * st, LANES)
    a2 = a2_ref[...].reshape(tp * st, LANES)
    blk = lambda i: i >> 5
    grp = lambda i: (i >> 4) & 1
    ch = lambda i: i & (cw - 1)

    srow = lax.broadcasted_iota(jnp.int32, (tp * st, 1), 0)
    icol = lax.broadcasted_iota(jnp.int32, (1, PAIR_IN), 1)
    irow = lax.broadcasted_iota(jnp.int32, (PAIR_IN, 1), 0)
    scol = lax.broadcasted_iota(jnp.int32, (1, tp * st), 1)
    same_sc = ((srow >> 6) & 1) == grp(icol)
    same_is = grp(irow) == ((scol >> 6) & 1)
    kg_row = lax.broadcasted_iota(jnp.int32, (PAIR_CH, 1), 0)
    same_ks = grp(kg_row) == ((scol >> 6) & 1)
    erow = (lax.broadcasted_iota(jnp.int32, (PAIR_IN, PAIR_CH), 0) & (PAIR_CH - 1)
            == lax.broadcasted_iota(jnp.int32, (PAIR_IN, PAIR_CH), 1)).astype(BF16)

    c_re = _place(a2, _sel_cols(PAIR_IN, 0, ch))
    c_im = _place(a2, _sel_cols(PAIR_IN, cw, ch))

    def c_times(p_re, p_im):
        return (jnp.where(same_sc, c_re * p_re - c_im * p_im, 0.0),
                jnp.where(same_sc, -(c_re * p_im + c_im * p_re), 0.0))

    tk_fwd = None
    for d in range(2):
        pbase = 2 * d * pw
        bbase = 2 * cw + 2 * d * cw
        power = lambda f: (_place(a1, _sel_cols(PAIR_IN, pbase, f)),
                           _place(a1, _sel_cols(PAIR_IN, pbase + pw, f)))

        m_re, m_im = c_times(*power((lambda i: blk(i) + 1) if d == 0 else (lambda i: q - blk(i))))
        for pp in range(tp):
            mout_s[pp, (2 * d) * st:(2 * d + 1) * st, :] = m_re[pp * st:(pp + 1) * st].astype(BF16)
            mout_s[pp, (2 * d + 1) * st:(2 * d + 2) * st, :] = m_im[pp * st:(pp + 1) * st].astype(BF16)

        spow = (lambda i: q - 1 - blk(i)) if d == 0 else blk
        pt_re = _place_nt(_sel_rows(PAIR_IN, pbase, spow), a1)
        pt_im = _place_nt(_sel_rows(PAIR_IN, pbase + pw, spow), a1)
        bt_re = _place_nt(_sel_rows(PAIR_IN, bbase, ch), a2)
        bt_im = _place_nt(_sel_rows(PAIR_IN, bbase + cw, ch), a2)
        w_re = jnp.where(same_is, pt_re * bt_re - pt_im * bt_im, 0.0)
        w_im = jnp.where(same_is, pt_re * bt_im + pt_im * bt_re, 0.0)
        for pp in range(tp):
            wsin_s[pp, :, (2 * d) * st:(2 * d + 1) * st] = w_re[:, pp * st:(pp + 1) * st].astype(BF16)
            wsin_s[pp, :, (2 * d + 1) * st:(2 * d + 2) * st] = w_im[:, pp * st:(pp + 1) * st].astype(BF16)

        k_re, k_im = c_times(*power(blk if d == 0 else (lambda i: q - 1 - blk(i))))
        b_re = jnp.where(same_ks, _place_nt(_sel_rows(PAIR_CH, bbase, ch), a2), 0.0)
        b_im = jnp.where(same_ks, _place_nt(_sel_rows(PAIR_CH, bbase + cw, ch), a2), 0.0)
        tk_bwd = []
        for pp in range(tp):
            sl = slice(pp * st, (pp + 1) * st)
            lagk = (jnp.dot(b_re[:, sl], k_re[sl], preferred_element_type=F32, precision=hp)
                    + jnp.dot(b_im[:, sl], k_im[sl], preferred_element_type=F32, precision=hp))
            l1, l2, l3 = _split3(lagk)
            rep = _dot(erow, l1) + _dot(erow, l2) + _dot(erow, l3)
            blocks = []
            for s in range(q):
                piece = rep[s * PAIR_CH:(s + 1) * PAIR_CH]
                if d == 0:
                    shifted = piece if s == 0 else pltpu.roll(piece, s * PAIR_CH, 1)
                    blocks.append(jnp.where(blk(icol) >= s, shifted, 0.0))
                else:
                    shift = ((s + 1) * PAIR_CH) % PAIR_IN
                    shifted = piece if shift == 0 else pltpu.roll(piece, shift, 1)
                    blocks.append(jnp.where(blk(icol) <= s, shifted, 0.0))
            tk_bwd.append(blocks)
        if d == 0:
            tk_fwd = tk_bwd
        else:
            for pp in range(tp):
                for s in range(q):
                    tk_s[pp, s * PAIR_CH:(s + 1) * PAIR_CH, :] = (
                        tk_fwd[pp][s] + tk_bwd[pp][s]).astype(BF16)


def _lane_piece_mask(piece):
    lane = lax.broadcasted_iota(jnp.int32, (1, LANES), 1)
    return (lane // PAIR_CH) == piece


def _s5_kernel(u_ref, a1_ref, a2_ref, lamd_ref, lamp_ref, d_ref, y_ref,
               x_s, y_s, g_s, tk_s, wsin_s, mout_s, *, nch, nbat):
    q = SSM_CHUNK
    rows = nbat * nch
    spt = LANES // PAIR_CH

    @pl.when(pl.program_id(1) == 0)
    def _():
        _s5_build(a1_ref, a2_ref, tk_s, wsin_s, mout_s)

    step_rows = lambda s: u_ref[:, s].astype(F32).reshape(rows, LANES)
    masks = [_lane_piece_mask(j) for j in range(spt)]
    for kt in range(q // spt):
        rolled = []
        for j in range(spt):
            xs = step_rows(kt * spt + j)
            rolled.append([xs if sh == 0 else pltpu.roll(xs, sh * PAIR_CH, 1) for sh in range(spt)])
        for pp in range(TILE_PAIRS):
            tile = None
            for j in range(spt):
                src = rolled[j][(j - pp) % spt]
                tile = src if tile is None else jnp.where(masks[j], src, tile)
            x_s[pp, :, kt * LANES:(kt + 1) * LANES] = tile.astype(BF16)

    for pp in range(TILE_PAIRS):
        _s5_pair(x_s.at[pp], y_s.at[pp], g_s, tk_s.at[pp], wsin_s.at[pp], mout_s.at[pp],
                 lamd_ref.at[pp], lamp_ref.at[pp], nch=nch, nbat=nbat)

    for kt in range(q // spt):
        rolled = []
        for pp in range(TILE_PAIRS):
            yt = y_s[pp, :, kt * LANES:(kt + 1) * LANES]
            rolled.append([yt if sh == 0 else pltpu.roll(yt, sh * PAIR_CH, 1) for sh in range(spt)])
        for j in range(spt):
            tile = None
            for pp in range(TILE_PAIRS):
                src = rolled[pp][(pp - j) % spt]
                tile = src if tile is None else jnp.where(masks[pp], src, tile)
            s = kt * spt + j
            out = tile + d_ref[...] * step_rows(s)
            y_ref[:, s] = out.reshape(y_ref.shape[0], y_ref.shape[2], LANES).astype(y_ref.dtype)


def _s5_pair(x_ref, y_ref, g_s, tk_ref, wsin_ref, mout_ref, lamd_ref, lamp_ref, *, nch, nbat):
    rows = nbat * nch
    st = PAIR_STATE
    n_tiles = nch // SUBLANES
    x = x_ref[...]

    sloc = _dot(x, wsin_ref[...])
    chunk = lax.broadcasted_iota(jnp.int32, (rows, 1), 0) & (nch - 1)
    rit = _row_in_tile(rows)
    comps = []
    for comp in range(N_COMP):
        reverse = comp >= 2
        piece = sloc[:, comp * st:(comp + 1) * st]
        if reverse:
            piece = jnp.where(chunk == nch - 1, 0.0, pltpu.roll(piece, rows - 1, 0))
        else:
            piece = jnp.where(chunk == 0, 0.0, pltpu.roll(piece, 1, 0))
        comps.append(piece)

    for di, d in enumerate(SCAN_STEPS):
        for base in (0, 2):
            reverse = base == 2
            lr = lamd_ref[di * N_COMP + base:di * N_COMP + base + 1, :]
            li = lamd_ref[di * N_COMP + base + 1:di * N_COMP + base + 2, :]
            if reverse:
                take = rit < SUBLANES - d
                shift = rows - d
            else:
                take = rit >= d
                shift = d
            gr, gi = comps[base], comps[base + 1]
            sr = jnp.where(take, pltpu.roll(gr, shift, 0), 0.0)
            si = jnp.where(take, pltpu.roll(gi, shift, 0), 0.0)
            comps[base] = gr + lr * sr - li * si
            comps[base + 1] = gi + lr * si + li * sr
    for comp in range(N_COMP):
        g_s[:, comp * st:(comp + 1) * st] = comps[comp]

    pf_r, pf_i = lamp_ref[0:SUBLANES, :], lamp_ref[SUBLANES:2 * SUBLANES, :]
    pb_r, pb_i = lamp_ref[2 * SUBLANES:3 * SUBLANES, :], lamp_ref[3 * SUBLANES:4 * SUBLANES, :]

    def step(k, carry):
        out = []
        for bi in range(nbat):
            cfr, cfi, cbr, cbi = carry[4 * bi:4 * bi + 4]
            rf = pl.multiple_of(bi * nch + k * SUBLANES, SUBLANES)
            rb = pl.multiple_of(bi * nch + (n_tiles - 1 - k) * SUBLANES, SUBLANES)
            gfr = g_s[pl.ds(rf, SUBLANES), 0:st] + pf_r * cfr - pf_i * cfi
            gfi = g_s[pl.ds(rf, SUBLANES), st:2 * st] + pf_r * cfi + pf_i * cfr
            gbr = g_s[pl.ds(rb, SUBLANES), 2 * st:3 * st] + pb_r * cbr - pb_i * cbi
            gbi = g_s[pl.ds(rb, SUBLANES), 3 * st:4 * st] + pb_r * cbi + pb_i * cbr
            g_s[pl.ds(rf, SUBLANES), 0:st] = gfr
            g_s[pl.ds(rf, SUBLANES), st:2 * st] = gfi
            g_s[pl.ds(rb, SUBLANES), 2 * st:3 * st] = gbr
            g_s[pl.ds(rb, SUBLANES), 3 * st:4 * st] = gbi
            last = lambda v: jnp.broadcast_to(v[SUBLANES - 1:SUBLANES, :], (SUBLANES, st))
            first = lambda v: jnp.broadcast_to(v[0:1, :], (SUBLANES, st))
            out += [last(gfr), last(gfi), first(gbr), first(gbi)]
        return tuple(out)

    z = jnp.zeros((SUBLANES, st), F32)
    lax.fori_loop(0, n_tiles, step, (z,) * (4 * nbat))

    y_ref[...] = _dot(x, tk_ref[...]) + _dot(g_s[...].astype(BF16), mout_ref[...])


def _s5(layer, u2, a1, a2, lamd, lamp, dskip, bsz, seq, nbat):
    q = SSM_CHUNK
    nch = seq // q
    assert nch & (nch - 1) == 0 and nch % SUBLANES == 0, "chunks per sequence must be a power of two"
    rows = nbat * nch
    assert (nbat * seq) % SSM_TILE == 0
    nt = nbat * seq // SSM_TILE
    cpt = SSM_TILE // q
    st = PAIR_STATE
    tp = TILE_PAIRS
    tile_spec = lambda shape: pl.BlockSpec(
        (None, None) + shape, lambda o, i: (layer, o) + (0,) * len(shape))
    io_spec = pl.BlockSpec((nt, q, cpt, LANES), lambda o, i: (i, 0, 0, o))
    u4 = u2.reshape(u2.shape[0] // SSM_TILE, q, cpt, BRANCH_WIDTH)
    return pl.pallas_call(
        functools.partial(_s5_kernel, nch=nch, nbat=nbat),
        grid=(BRANCH_WIDTH // LANES, bsz // nbat),
        in_specs=[
            io_spec,
            tile_spec((tp, st, LANES)),
            tile_spec((tp, st, LANES)),
            tile_spec((tp, len(SCAN_STEPS) * N_COMP, st)),
            tile_spec((tp, N_COMP * SUBLANES, st)),
            tile_spec((1, LANES)),
        ],
        out_specs=io_spec,
        out_shape=jax.ShapeDtypeStruct(u4.shape, BF16),
        scratch_shapes=[pltpu.VMEM((tp, rows, PAIR_IN), BF16),
                        pltpu.VMEM((tp, rows, PAIR_IN), F32),
                        pltpu.VMEM((rows, N_COMP * st), F32),
                        pltpu.VMEM((tp, PAIR_IN, PAIR_IN), BF16),
                        pltpu.VMEM((tp, PAIR_IN, N_COMP * st), BF16),
                        pltpu.VMEM((tp, N_COMP * st, PAIR_IN), BF16)],
        compiler_params=pltpu.CompilerParams(
            dimension_semantics=("parallel", "arbitrary"), vmem_limit_bytes=VMEM_LIMIT),
        name="s5_chunked",
    )(u4, a1, a2, lamd, lamp, dskip).reshape(u2.shape)


def _s5_params(a_re, a_im, log_step, b_re, b_im, c_re, c_im, d_skip):
    q = SSM_CHUNK
    g, n, c = SSM_GROUPS, SSM_STATE, SSM_GROUP
    no, og = SSM_PAIRS, 2
    step = jnp.exp(log_step)[..., None]
    dr = a_re * step
    di = a_im * step
    mag = jnp.exp(dr)
    abar_r = mag * jnp.cos(di)
    abar_i = mag * jnp.sin(di)
    den = a_re * a_re + a_im * a_im
    nr = abar_r - 1.0
    sr = (nr * a_re + abar_i * a_im) / den
    si = (abar_i * a_re - nr * a_im) / den
    bb_r = sr[..., None] * b_re - si[..., None] * b_im
    bb_i = sr[..., None] * b_im + si[..., None] * b_re

    def powers(kk):
        kk = jnp.asarray(kk, F32)
        pmag = jnp.exp(dr[..., None] * kk)
        return pmag * jnp.cos(di[..., None] * kk), pmag * jnp.sin(di[..., None] * kk)

    pr, pi = powers(jnp.arange(LANES // 4))
    a1 = jnp.concatenate([pr[0], pi[0], pr[1], pi[1]], axis=-1).reshape(no, PAIR_STATE, LANES)
    by_state = lambda w_: w_.reshape(no, PAIR_STATE, c)
    a2 = jnp.concatenate(
        [by_state(c_re.transpose(0, 2, 1)), by_state(c_im.transpose(0, 2, 1)),
         by_state(bb_r[0]), by_state(bb_i[0]), by_state(bb_r[1]), by_state(bb_i[1]),
         jnp.zeros((no, PAIR_STATE, LANES - 6 * c), F32)], axis=-1)

    def scan_rows(kk):
        sr_, si_ = powers(q * jnp.asarray(kk))
        rows = jnp.stack([sr_[0], si_[0], sr_[1], si_[1]], axis=0)
        return rows.reshape(N_COMP, no, og, n, len(kk))

    lamd = scan_rows(SCAN_STEPS).transpose(1, 4, 0, 2, 3).reshape(
        no, len(SCAN_STEPS) * N_COMP, PAIR_STATE)
    fwd = scan_rows(list(range(1, SUBLANES + 1)))[:2]
    bwd = scan_rows(list(range(SUBLANES, 0, -1)))[2:]
    lamp = jnp.concatenate([fwd, bwd], axis=0).transpose(1, 0, 4, 2, 3).reshape(
        no, N_COMP * SUBLANES, PAIR_STATE)
    dskip = d_skip.reshape(BRANCH_WIDTH // LANES, 1, LANES)
    by_tile = lambda w_: w_.reshape((BRANCH_WIDTH // LANES, TILE_PAIRS) + w_.shape[1:])
    return by_tile(a1), by_tile(a2), by_tile(lamd), by_tile(lamp), dskip


def _merge_kernel(x_ref, g_ref, oa_ref, hs_ref, gr_ref, ys_ref, unperm_ref, wgate_ref, wb_ref,
                  wglu_ref, bglu_ref, wout_ref, o_ref):
    x = x_ref[...]
    hn = _rms(x, g_ref[...]).astype(BF16)
    bw = BRANCH_WIDTH
    o_lru = (jax.nn.gelu(gr_ref[...].astype(F32)) * hs_ref[...]).astype(BF16)
    zg = _dot(jax.nn.gelu(ys_ref[...].astype(F32)).astype(BF16), wglu_ref[...]) + bglu_ref[...]
    o_ssm = (zg[:, :bw] * _sigmoid(zg[:, bw:])).astype(BF16)
    o_ssm = _dot(unperm_ref[...], o_ssm).astype(BF16)
    mixed = None
    for kb, ob in enumerate((oa_ref[...], o_lru, o_ssm)):
        gate = _sigmoid(_dot(hn, wgate_ref[:, kb * D_MODEL:(kb + 1) * D_MODEL]))
        term = _dot(ob, wb_ref[kb]) * gate
        mixed = term if mixed is None else mixed + term
    o_ref[...] = x + _dot(mixed.astype(BF16), wout_ref[...])


def _merge(layer, x2, g, o_attn, h_sum, gr, y_ssm, unperm, w_gate, w_branch, w_glu, b_glu, w_out):
    t = x2.shape[0]
    tm = SSM_TILE
    row = lambda w: pl.BlockSpec((tm, w), lambda i: (i, 0))
    return pl.pallas_call(
        _merge_kernel,
        grid=(t // tm,),
        in_specs=[row(D_MODEL), _layer_spec(layer, (1, D_MODEL)), row(BRANCH_WIDTH),
                  row(BRANCH_WIDTH), row(BRANCH_WIDTH), row(BRANCH_WIDTH), _const_spec((tm, tm)),
                  _layer_spec(layer, (D_MODEL, N_BRANCH * D_MODEL)),
                  _layer_spec(layer, (N_BRANCH, BRANCH_WIDTH, D_MODEL)),
                  _layer_spec(layer, (BRANCH_WIDTH, 2 * BRANCH_WIDTH)),
                  _layer_spec(layer, (1, 2 * BRANCH_WIDTH)),
                  _layer_spec(layer, (D_MODEL, D_MODEL))],
        out_specs=row(D_MODEL),
        out_shape=jax.ShapeDtypeStruct((t, D_MODEL), F32),
        compiler_params=pltpu.CompilerParams(
            dimension_semantics=("parallel",), vmem_limit_bytes=VMEM_LIMIT),
        name="merge",
    )(x2, g, o_attn, h_sum, gr, y_ssm, unperm, w_gate, w_branch, w_glu, b_glu, w_out)


def _route(logits):
    lane = lax.broadcasted_iota(jnp.int32, logits.shape, 1).astype(F32)
    neg = -jnp.inf
    first = lambda mask: jnp.min(jnp.where(mask, lane, float(ROUTE_COLS)), axis=-1, keepdims=True)
    gmask = lane < N_GROUPS
    gmax = jnp.max(jnp.where(gmask, logits, neg), axis=-1, keepdims=True)
    gidx = first(gmask & (logits == gmax))
    gsum = jnp.sum(jnp.where(gmask, jnp.exp(logits - gmax), 0.0), axis=-1, keepdims=True)
    g_w = 1.0 / gsum
    e_lo = N_GROUPS + EXPERTS_PER_GROUP * gidx
    emask = (lane >= e_lo) & (lane < e_lo + EXPERTS_PER_GROUP)
    v1 = jnp.max(jnp.where(emask, logits, neg), axis=-1, keepdims=True)
    i1 = first(emask & (logits == v1))
    emask2 = emask & (lane != i1)
    v2 = jnp.max(jnp.where(emask2, logits, neg), axis=-1, keepdims=True)
    i2 = first(emask2 & (logits == v2))
    t = jnp.exp(v2 - v1)
    w1 = 1.0 / (1.0 + t)
    w2 = t * w1
    cw = jnp.where(lane == i1, w1 * g_w, jnp.where(lane == i2, w2 * g_w, 0.0))
    return cw, jnp.where(lane == gidx, 1.0, 0.0)


def _moe_kernel(x_ref, g_ref, wr_ref, br_ref, tril_ref, w13_ref, w2_ref, fg_ref, o_ref,
                xs_s, cws_s, ys_s, *, final_norm):
    x = x_ref[...]
    tm = x.shape[0]
    ns = xs_s.shape[0]
    hn = _rms(x, g_ref[...])
    hb = hn.astype(BF16)
    lo = (hn - hb.astype(F32)).astype(BF16)
    hb_w = _dot(hb, wr_ref[...])
    logits = (hb_w[:, :ROUTE_COLS] + hb_w[:, ROUTE_COLS:] + _dot(lo, wr_ref[:, :ROUTE_COLS])
              + br_ref[...])
    cw, ghot = _route(logits)

    lane = lax.broadcasted_iota(jnp.int32, (1, ROUTE_COLS), 1)
    rank = _dot(tril_ref[...], ghot.astype(BF16))
    total = jnp.sum(ghot, axis=0, keepdims=True)
    counts, offsets = [], []
    off = jnp.int32(0)
    for grp in range(N_GROUPS):
        cnt = jnp.sum(jnp.where(lane == grp, total, 0.0)).astype(jnp.int32)
        counts.append(cnt)
        offsets.append(off)
        off = off + ((cnt + (MOE_ALIGN - 1)) // MOE_ALIGN) * MOE_ALIGN
    off_vec = jnp.zeros((1, ROUTE_COLS), F32)
    for grp in range(1, N_GROUPS):
        off_vec = jnp.where(lane == grp, offsets[grp].astype(F32), off_vec)
    pos_terms = ghot * (rank + off_vec)
    pos_col = jnp.sum(pos_terms, axis=-1, keepdims=True)
    pos_row = _place_nt(jnp.ones((SUBLANES, ROUTE_COLS), BF16), pos_terms)[0:1]
    sort = (lax.broadcasted_iota(jnp.int32, (ns, tm), 0).astype(F32) == pos_row).astype(BF16)
    unsort = (lax.broadcasted_iota(jnp.int32, (tm, ns), 1).astype(F32) == pos_col).astype(BF16)

    xs_s[...] = _dot(sort, hb).astype(BF16)
    c1 = cw.astype(BF16)
    c2 = (cw - c1.astype(F32)).astype(BF16)
    cw_sorted = _dot(sort, jnp.concatenate([c1, c2], axis=1))
    cws_s[...] = cw_sorted[:, :ROUTE_COLS] + cw_sorted[:, ROUTE_COLS:]
    ys_s[...] = jnp.zeros_like(ys_s)

    for grp in range(N_GROUPS):
        def chunk(k, carry, grp=grp):
            r0 = pl.multiple_of(offsets[grp] + k * MOE_CHUNK, MOE_ALIGN)
            h = _dot(xs_s[pl.ds(r0, MOE_CHUNK), :], w13_ref[grp])
            cwb = cws_s[pl.ds(r0, MOE_CHUNK), :]
            cols = []
            for e in range(EXPERTS_PER_GROUP):
                col = N_GROUPS + grp * EXPERTS_PER_GROUP + e
                wcol = jnp.sum(jnp.where(lane == col, cwb, 0.0), axis=-1, keepdims=True)
                cols.append(jnp.broadcast_to(wcol, (MOE_CHUNK, D_EXPERT)))
            h1 = h[:, :GROUP_HIDDEN]
            hg = (h1 * _sigmoid(h1) * h[:, GROUP_HIDDEN:] * jnp.concatenate(cols, axis=1))
            ys_s[pl.ds(r0, MOE_CHUNK), :] = _dot(hg.astype(BF16), w2_ref[grp])
            return carry

        lax.fori_loop(0, (counts[grp] + (MOE_CHUNK - 1)) // MOE_CHUNK, chunk, 0)

    y = x + _dot(unsort, ys_s[...].astype(BF16))
    if final_norm:
        y = _rms(y, fg_ref[...])
    o_ref[...] = y


def _moe(layer, x2, g, w_route, b_route, w13, w2, final_g, tm, final_norm):
    t = x2.shape[0]
    ns = -(-(tm + N_GROUPS * (MOE_ALIGN - 1) + MOE_CHUNK) // LANES) * LANES
    row = pl.BlockSpec((tm, D_MODEL), lambda i: (i, 0))
    once = pl.Buffered(1)
    tril = jnp.asarray(np.tril(np.ones((tm, tm), np.float32), -1), BF16)
    return pl.pallas_call(
        functools.partial(_moe_kernel, final_norm=final_norm),
        grid=(t // tm,),
        in_specs=[row,
                  _layer_spec(layer, (1, D_MODEL)),
                  _layer_spec(layer, (D_MODEL, 2 * ROUTE_COLS)),
                  _layer_spec(layer, (1, ROUTE_COLS)),
                  _const_spec((tm, tm)),
                  _layer_spec(layer, (N_GROUPS, D_MODEL, 2 * GROUP_HIDDEN), pipeline_mode=once),
                  _layer_spec(layer, (N_GROUPS, GROUP_HIDDEN, D_MODEL), pipeline_mode=once),
                  _const_spec((1, D_MODEL))],
        out_specs=row,
        out_shape=jax.ShapeDtypeStruct((t, D_MODEL), F32),
        scratch_shapes=[pltpu.VMEM((ns, D_MODEL), BF16), pltpu.VMEM((ns, ROUTE_COLS), F32),
                        pltpu.VMEM((ns, D_MODEL), F32)],
        compiler_params=pltpu.CompilerParams(
            dimension_semantics=("parallel",), vmem_limit_bytes=VMEM_LIMIT),
        name="moe_final" if final_norm else "moe",
    )(x2, g, w_route, b_route, tril, w13, w2, final_g)


def _prepare(p):
    w_in = p['w_in']
    depth = w_in.shape[0]
    pad = ROUTE_COLS - N_GROUPS - N_EXPERTS
    w_route = jnp.concatenate(
        [p['moe_w_group'], p['moe_w_expert'], jnp.zeros((depth, D_MODEL, pad), F32)], axis=-1)
    wr_hi = w_route.astype(BF16)
    grp = lambda w: w.transpose(0, 1, 3, 2, 4).reshape(depth, N_GROUPS, D_MODEL, GROUP_HIDDEN)
    s5_a1, s5_a2, lamd, lamp, dskip = jax.vmap(_s5_params)(
        p['ssm_a_re'], p['ssm_a_im'], p['ssm_log_step'], p['ssm_b_re'], p['ssm_b_im'],
        p['ssm_c_re'], p['ssm_c_im'], p['ssm_d'])
    return dict(
        norm1_g=p['norm1_g'][:, None],
        w_main=w_in[:, :, :MAIN_COLS].astype(BF16),
        w_gate=w_in[:, :, MAIN_COLS:].astype(BF16),
        attn_sink=p['attn_sink'],
        conv_w=p['lru_conv_w'],
        conv_b=p['lru_conv_b'][:, None],
        lru_wg=jnp.concatenate([_block_diag(p['lru_w_r']), _block_diag(p['lru_w_i'])],
                               axis=-1).astype(BF16),
        lru_bg=jnp.concatenate([p['lru_b_r'], p['lru_b_i']], axis=-1)[:, :, None],
        lru_sp=jax.nn.softplus(-p['lru_lambda'])[:, :, None],
        s5_a1=s5_a1, s5_a2=s5_a2, lamd=lamd, lamp=lamp, dskip=dskip,
        w_branch=p['w_branch'].astype(BF16),
        w_glu=p['ssm_w_glu'].astype(BF16),
        b_glu=p['ssm_b_glu'][:, None],
        w_out=p['w_out'].astype(BF16),
        norm2_g=p['norm2_g'][:, None],
        w_route=jnp.concatenate([wr_hi, (w_route - wr_hi.astype(F32)).astype(BF16)], axis=-1),
        b_route=jnp.concatenate(
            [p['moe_b_group'], p['moe_b_expert'], jnp.zeros((depth, pad), F32)], axis=-1)[:, None],
        w13=jnp.concatenate([grp(p['moe_w1']), grp(p['moe_w3'])], axis=-1).astype(BF16),
        w2=p['moe_w2'].reshape(depth, N_GROUPS, GROUP_HIDDEN, D_MODEL).astype(BF16),
    )


def _layer(layer, x2, bsz, seq, w, final_g, final_norm):
    t = bsz * seq
    bw = BRANCH_WIDTH
    assert t % SSM_TILE == 0
    perm = _chunk_perm()
    qkv, xr, gr, u = _inproj(layer, x2, w['norm1_g'], w['w_main'], jnp.asarray(perm, BF16))
    o_attn = _attention(layer, qkv.reshape(bsz, seq, QKV_WIDTH), w['attn_sink']).reshape(t, bw)
    h_sum = _lru(layer, xr.reshape(bsz, seq, bw), w['conv_w'], w['conv_b'], w['lru_wg'],
                 w['lru_bg'], w['lru_sp']).reshape(t, bw)
    y_ssm = _s5(layer, u, w['s5_a1'], w['s5_a2'], w['lamd'], w['lamp'], w['dskip'],
                bsz, seq, _pick_tile(bsz, 4))
    x2 = _merge(layer, x2, w['norm1_g'], o_attn, h_sum, gr, y_ssm, jnp.asarray(perm.T, BF16),
                w['w_gate'], w['w_branch'], w['w_glu'], w['b_glu'], w['w_out'])
    return _moe(layer, x2, w['norm2_g'], w['w_route'], w['b_route'], w['w13'], w['w2'],
                final_g, _pick_tile(t, 512), final_norm)


_PARAM_KEYS = ('norm1_g', 'w_in', 'attn_sink', 'lru_conv_w', 'lru_conv_b', 'lru_w_r', 'lru_b_r',
               'lru_w_i', 'lru_b_i', 'lru_lambda', 'ssm_a_re', 'ssm_a_im', 'ssm_log_step',
               'ssm_b_re', 'ssm_b_im', 'ssm_c_re', 'ssm_c_im', 'ssm_d', 'ssm_w_glu', 'ssm_b_glu',
               'w_branch', 'w_out', 'norm2_g', 'moe_w_group', 'moe_b_group', 'moe_w_expert',
               'moe_b_expert', 'moe_w1', 'moe_w3', 'moe_w2')


def kernel(x, norm1_g, w_in, attn_sink, lru_conv_w, lru_conv_b, lru_w_r, lru_b_r, lru_w_i, lru_b_i,
           lru_lambda, ssm_a_re, ssm_a_im, ssm_log_step, ssm_b_re, ssm_b_im, ssm_c_re, ssm_c_im,
           ssm_d, ssm_w_glu, ssm_b_glu, w_branch, w_out, norm2_g, moe_w_group, moe_b_group,
           moe_w_expert, moe_b_expert, moe_w1, moe_w3, moe_w2, final_norm_g):
    params = dict(zip(_PARAM_KEYS, (
        norm1_g, w_in, attn_sink, lru_conv_w, lru_conv_b, lru_w_r, lru_b_r, lru_w_i, lru_b_i,
        lru_lambda, ssm_a_re, ssm_a_im, ssm_log_step, ssm_b_re, ssm_b_im, ssm_c_re, ssm_c_im,
        ssm_d, ssm_w_glu, ssm_b_glu, w_branch, w_out, norm2_g, moe_w_group, moe_b_group,
        moe_w_expert, moe_b_expert, moe_w1, moe_w3, moe_w2)))
    bsz, seq, _ = x.shape
    depth = norm1_g.shape[0]
    w = _prepare(params)
    x2 = x.reshape(bsz * seq, D_MODEL)
    for layer in range(depth):
        x2 = _layer(layer, x2, bsz, seq, w, final_norm_g[None], final_norm=(layer == depth - 1))
    return x2.reshape(bsz, seq, D_MODEL)
```

```python
import functools

import jax
import jax.numpy as jnp
import numpy as np
from jax import lax
from jax.experimental import pallas as pl
from jax.experimental.pallas import tpu as pltpu

F32 = jnp.float32
BF16 = jnp.bfloat16

D_MODEL = 1024
N_Q_HEADS = 8
N_KV_HEADS = 2
HEAD_DIM = 64
WINDOW = 128
BLOCK = 128
BRANCH_WIDTH = 512
KV_WIDTH = N_KV_HEADS * HEAD_DIM
LRU_C = 8.0
SSM_GROUP = 16
SSM_GROUPS = BRANCH_WIDTH // SSM_GROUP
SSM_STATE = 64
N_BRANCH = 3
N_GROUPS = 4
EXPERTS_PER_GROUP = 4
N_EXPERTS = N_GROUPS * EXPERTS_PER_GROUP
D_EXPERT = 256
GROUP_HIDDEN = EXPERTS_PER_GROUP * D_EXPERT
NORM_EPS = 1e-6
MASK_VALUE = -1e30

LANES = 128
SUBLANES = 8
QKV_WIDTH = BRANCH_WIDTH + 2 * KV_WIDTH
MAIN_COLS = QKV_WIDTH + 3 * BRANCH_WIDTH
HEADS_PER_KV = N_Q_HEADS // N_KV_HEADS
SSM_CHUNK = 16
PAIR_CH = 2 * SSM_GROUP
SSM_PAIRS = SSM_GROUPS // 2
TILE_PAIRS = LANES // PAIR_CH
PAIR_IN = SSM_CHUNK * PAIR_CH
PAIR_STATE = 2 * SSM_STATE
SSM_TILE = 512
N_COMP = 4
SCAN_STEPS = (1, 2, 4)
ROUTE_COLS = LANES
MOE_ALIGN = 16
MOE_CHUNK = 160
CONV_HALO = 16
SQRT_CLAMP = 1e-30
VMEM_LIMIT = 56 * 1024 * 1024


def _rms(x, g):
    ms = jnp.mean(x * x, axis=-1, keepdims=True)
    return x * lax.rsqrt(ms + NORM_EPS) * g


def _sigmoid(x):
    return 0.5 * jnp.tanh(0.5 * x) + 0.5


def _dot(a, b):
    return jnp.dot(a, b, preferred_element_type=F32)


def _dot_nt(a, b):
    return lax.dot_general(a, b, (((1,), (1,)), ((), ())), preferred_element_type=F32)


def _const_spec(shape):
    nd = len(shape)
    return pl.BlockSpec(shape, lambda *_: (0,) * nd)


def _layer_spec(layer, shape, **kw):
    nd = len(shape)
    return pl.BlockSpec((None,) + tuple(shape), lambda *_: (layer,) + (0,) * nd, **kw)


def _row_in_tile(rows):
    return lax.broadcasted_iota(jnp.int32, (rows, 1), 0) & (SUBLANES - 1)


def _pick_tile(n, want):
    t = min(n, want)
    while n % t:
        t //= 2
    return t


def _chunk_perm():
    t = np.arange(SSM_TILE)
    dst = (t % SSM_CHUNK) * (SSM_TILE // SSM_CHUNK) + t // SSM_CHUNK
    perm = np.zeros((SSM_TILE, SSM_TILE), np.float32)
    perm[dst, t] = 1.0
    return perm


def _inproj_kernel(x_ref, g_ref, w_ref, perm_ref, qkv_ref, xr_ref, gr_ref, u_ref):
    hn = _rms(x_ref[...], g_ref[...]).astype(BF16)
    bw = BRANCH_WIDTH
    qkv_ref[...] = _dot(hn, w_ref[:, 0:QKV_WIDTH]).astype(BF16)
    xr_ref[...] = _dot(hn, w_ref[:, QKV_WIDTH:QKV_WIDTH + bw]).astype(BF16)
    gr_ref[...] = _dot(hn, w_ref[:, QKV_WIDTH + bw:QKV_WIDTH + 2 * bw]).astype(BF16)
    u = _dot(hn, w_ref[:, QKV_WIDTH + 2 * bw:MAIN_COLS]).astype(BF16)
    u_ref[...] = _dot(perm_ref[...], u).astype(BF16)


def _inproj(layer, x2, g, w_main, perm):
    t = x2.shape[0]
    tm = SSM_TILE
    row = lambda w: pl.BlockSpec((tm, w), lambda i: (i, 0))
    return pl.pallas_call(
        _inproj_kernel,
        grid=(t // tm,),
        in_specs=[row(D_MODEL), _layer_spec(layer, (1, D_MODEL)),
                  _layer_spec(layer, (D_MODEL, MAIN_COLS)), _const_spec((tm, tm))],
        out_specs=[row(QKV_WIDTH), row(BRANCH_WIDTH), row(BRANCH_WIDTH), row(BRANCH_WIDTH)],
        out_shape=[jax.ShapeDtypeStruct((t, QKV_WIDTH), BF16)]
        + [jax.ShapeDtypeStruct((t, BRANCH_WIDTH), BF16)] * 3,
        compiler_params=pltpu.CompilerParams(
            dimension_semantics=("parallel",), vmem_limit_bytes=VMEM_LIMIT),
        name="inproj",
    )(x2, g, w_main, perm)


def _attn_bias():
    qi = np.arange(BLOCK)[:, None]
    kj = np.arange(3 * BLOCK)[None, :]
    dist = np.abs(qi - kj + BLOCK)
    out = np.empty((N_KV_HEADS, HEADS_PER_KV * BLOCK, 3 * BLOCK), np.float32)
    for head in range(N_Q_HEADS):
        h, g = divmod(head, HEADS_PER_KV)
        slope = 2.0 ** (-8.0 * (head + 1) / N_Q_HEADS)
        out[h, g * BLOCK:(g + 1) * BLOCK] = np.where(dist <= WINDOW, -slope * dist, 1.0)
    return out


def _attn_block(sink_ref, bias_ref, q, kv, n, nb, layer):
    k = kv[:, :KV_WIDTH]
    v = kv[:, KV_WIDTH:]
    lane = lax.broadcasted_iota(jnp.int32, (1, KV_WIDTH), 1)
    lo = lane < HEAD_DIM

    rows = HEADS_PER_KV * BLOCK
    kj = lax.broadcasted_iota(jnp.int32, (1, 3 * BLOCK), 1)
    in_seq = ((kj >= BLOCK) | (n > 0)) & ((kj < 2 * BLOCK) | (n < nb - 1))
    valid = (bias_ref[0] <= 0.0) & in_seq
    head_in_kv = lax.broadcasted_iota(jnp.int32, (rows, 1), 0) // BLOCK

    scale = HEAD_DIM ** -0.5
    outs = [None] * (N_Q_HEADS // 2)
    for h in range(N_KV_HEADS):
        own = lo if h == 0 else jnp.logical_not(lo)
        km = jnp.where(own, k, 0.0)
        vm = jnp.where(own, v, 0.0)
        k2 = ((km + pltpu.roll(km, HEAD_DIM, 1)) * scale).astype(BF16)
        v2 = (vm + pltpu.roll(vm, HEAD_DIM, 1)).astype(BF16)
        parts = []
        sink = jnp.zeros((rows, 1), F32)
        for g in range(HEADS_PER_KV):
            head = h * HEADS_PER_KV + g
            pair = head // 2
            qp = q[:, pair * KV_WIDTH:(pair + 1) * KV_WIDTH]
            keep = lo if head % 2 == 0 else jnp.logical_not(lo)
            parts.append(jnp.where(keep, qp, jnp.zeros_like(qp)))
            sink = jnp.where(head_in_kv == g, sink_ref[layer, head], sink)
        s = _dot_nt(jnp.concatenate(parts, axis=0), k2)
        s = jnp.where(valid, s + bias_ref[h], MASK_VALUE)
        m = jnp.maximum(jnp.max(s, axis=-1, keepdims=True), sink)
        p = jnp.exp(s - m)
        denom = jnp.sum(p, axis=-1, keepdims=True) + jnp.exp(sink - m)
        r = _dot(p.astype(BF16), v2) / denom
        for j in range(HEADS_PER_KV // 2):
            even = r[(2 * j) * BLOCK:(2 * j + 1) * BLOCK]
            odd = r[(2 * j + 1) * BLOCK:(2 * j + 2) * BLOCK]
            outs[h * (HEADS_PER_KV // 2) + j] = jnp.where(lo, even, odd).astype(BF16)
    return outs


def _tile_scan(a, u, reverse):
    rows = a.shape[0]
    rit = _row_in_tile(rows)
    for d in SCAN_STEPS:
        if reverse:
            take = rit < SUBLANES - d
            shift = rows - d
        else:
            take = rit >= d
            shift = d
        a_sh = jnp.where(take, pltpu.roll(a, shift, 0), 1.0)
        u_sh = jnp.where(take, pltpu.roll(u, shift, 0), 0.0)
        u = u + a * u_sh
        a = a * a_sh
    return a, u


def _lru_kernel(x_ref, cw_ref, cb_ref, wg_ref, bg_ref, c_ref, o_ref, xe_s, xc_s, a_s, u_s, hf_s,
                *, seq, rc):
    phase = pl.program_id(1)
    bw = BRANCH_WIDTH
    n_tiles = seq // SUBLANES

    def conv():
        zeros = jnp.zeros((CONV_HALO, bw), F32)
        xe_s[0:CONV_HALO, :] = zeros
        xe_s[CONV_HALO + seq:CONV_HALO + seq + CONV_HALO, :] = zeros
        xe_s[CONV_HALO:CONV_HALO + seq, :] = x_ref[...].astype(F32)
        cw = cw_ref[...]
        for c in range(seq // rc):
            base = CONV_HALO + c * rc
            xc = cb_ref[...]
            for tap in range(4):
                xc = xc + cw[tap:tap + 1] * xe_s[base + tap - 2:base + tap - 2 + rc, :]
            xc_s[c * rc:(c + 1) * rc, :] = xc

    def build(reverse):
        for c in range(seq // rc):
            xc = xc_s[c * rc:(c + 1) * rc, :]
            tz = jnp.tanh(_dot(xc.astype(BF16), wg_ref[...]) + bg_ref[...])
            log_a = c_ref[...] * tz[:, :bw] + c_ref[...]
            a = jnp.exp(log_a)
            t = 1.0 - a * a
            xh = 0.5 * xc
            u = t * lax.rsqrt(jnp.maximum(t, SQRT_CLAMP)) * (tz[:, bw:] * xh + xh)
            a, u = _tile_scan(a, u, reverse)
            a_s[c * rc:(c + 1) * rc, :] = a
            u_s[c * rc:(c + 1) * rc, :] = u

    @pl.when(phase == 0)
    def _():
        conv()
        build(False)

        def step(k, carry):
            r0 = pl.multiple_of(k * SUBLANES, SUBLANES)
            h = u_s[pl.ds(r0, SUBLANES), :] + a_s[pl.ds(r0, SUBLANES), :] * carry
            hf_s[pl.ds(r0, SUBLANES), :] = h
            return jnp.broadcast_to(h[SUBLANES - 1:SUBLANES, :], (SUBLANES, bw))

        lax.fori_loop(0, n_tiles, step, jnp.zeros((SUBLANES, bw), F32), unroll=4)

    @pl.when(phase == 1)
    def _():
        build(True)

        def step(k, carry):
            r0 = pl.multiple_of((n_tiles - 1 - k) * SUBLANES, SUBLANES)
            h = u_s[pl.ds(r0, SUBLANES), :] + a_s[pl.ds(r0, SUBLANES), :] * carry
            o_ref[pl.ds(r0, SUBLANES), :] = h + hf_s[pl.ds(r0, SUBLANES), :]
            return jnp.broadcast_to(h[0:1, :], (SUBLANES, bw))

        lax.fori_loop(0, n_tiles, step, jnp.zeros((SUBLANES, bw), F32), unroll=4)


def _lru(layer, xr3, conv_w, conv_b, wg, bg, sp):
    b, l, w = xr3.shape
    seq_spec = pl.BlockSpec((None, l, w), lambda bi, ph: (bi, 0, 0))
    per_dir = lambda shape: pl.BlockSpec(
        (None, None) + shape, lambda bi, ph: (layer, ph) + (0,) * len(shape))
    return pl.pallas_call(
        functools.partial(_lru_kernel, seq=l, rc=_pick_tile(l, 512)),
        grid=(b, 2),
        in_specs=[seq_spec, _layer_spec(layer, (4, w)), _layer_spec(layer, (1, w)),
                  per_dir((w, 2 * w)), per_dir((1, 2 * w)), per_dir((1, w))],
        out_specs=seq_spec,
        out_shape=jax.ShapeDtypeStruct((b, l, w), F32),
        scratch_shapes=[pltpu.VMEM((l + 2 * CONV_HALO, w), F32)] + [pltpu.VMEM((l, w), F32)] * 4,
        compiler_params=pltpu.CompilerParams(
            dimension_semantics=("parallel", "arbitrary"), vmem_limit_bytes=VMEM_LIMIT),
        name="rg_lru",
    )(xr3, conv_w, conv_b, wg, bg, sp)


def _block_diag(wb):
    nbk, bd = wb.shape[-3], wb.shape[-2]
    eye = jnp.eye(nbk, dtype=wb.dtype)
    out = jnp.einsum('...hij,hk->...hikj', wb, eye)
    return out.reshape(wb.shape[:-3] + (nbk * bd, nbk * bd))


def _split3(a):
    a1 = a.astype(BF16)
    r1 = a - a1.astype(F32)
    a2 = r1.astype(BF16)
    a3 = (r1 - a2.astype(F32)).astype(BF16)
    return a1, a2, a3


def _place(a, sel):
    p1, p2, p3 = _split3(a)
    return _dot(p1, sel) + _dot(p2, sel) + _dot(p3, sel)


def _place_nt(sel, a):
    p1, p2, p3 = _split3(a)
    return _dot_nt(sel, p1) + _dot_nt(sel, p2) + _dot_nt(sel, p3)


def _sel_cols(n_cols, base, idx_of_col):
    lane = lax.broadcasted_iota(jnp.int32, (LANES, n_cols), 0)
    col = lax.broadcasted_iota(jnp.int32, (LANES, n_cols), 1)
    return (lane == base + idx_of_col(col)).astype(BF16)


def _sel_rows(n_rows, base, idx_of_row):
    row = lax.broadcasted_iota(jnp.int32, (n_rows, LANES), 0)
    lane = lax.broadcasted_iota(jnp.int32, (n_rows, LANES), 1)
    return (lane == base + idx_of_row(row)).astype(BF16)


def _s5_build(a1_ref, a2_ref, tk_s, wsin_s, mout_s):
    q = SSM_CHUNK
    st = PAIR_STATE
    tp = TILE_PAIRS
    pw = LANES // 4
    cw = SSM_GROUP
    hp = lax.Precision.HIGHEST
    a1 = a1_ref[...].reshape(tp * st, LANES)
    a2 = a2_ref[...].reshape(tp * st, LANES)
    blk = lambda i: i >> 5
    grp = lambda i: (i >> 4) & 1
    ch = lambda i: i & (cw - 1)

    srow = lax.broadcasted_iota(jnp.int32, (tp * st, 1), 0)
    icol = lax.broadcasted_iota(jnp.int32, (1, PAIR_IN), 1)
    irow = lax.broadcasted_iota(jnp.int32, (PAIR_IN, 1), 0)
    scol = lax.broadcasted_iota(jnp.int32, (1, tp * st), 1)
    same_sc = ((srow >> 6) & 1) == grp(icol)
    same_is = grp(irow) == ((scol >> 6) & 1)
    kg_row = lax.broadcasted_iota(jnp.int32, (PAIR_CH, 1), 0)
    same_ks = grp(kg_row) == ((scol >> 6) & 1)
    erow = (lax.broadcasted_iota(jnp.int32, (PAIR_IN, PAIR_CH), 0) & (PAIR_CH - 1)
            == lax.broadcasted_iota(jnp.int32, (PAIR_IN, PAIR_CH), 1)).astype(BF16)

    c_re = _place(a2, _sel_cols(PAIR_IN, 0, ch))
    c_im = _place(a2, _sel_cols(PAIR_IN, cw, ch))

    def c_times(p_re, p_im):
        return (jnp.where(same_sc, c_re * p_re - c_im * p_im, 0.0),
                jnp.where(same_sc, -(c_re * p_im + c_im * p_re), 0.0))

    tk_fwd = None
    for d in range(2):
        pbase = 2 * d * pw
        bbase = 2 * cw + 2 * d * cw
        power = lambda f: (_place(a1, _sel_cols(PAIR_IN, pbase, f)),
                           _place(a1, _sel_cols(PAIR_IN, pbase + pw, f)))

        m_re, m_im = c_times(*power((lambda i: blk(i) + 1) if d == 0 else (lambda i: q - blk(i))))
        for pp in range(tp):
            mout_s[pp, (2 * d) * st:(2 * d + 1) * st, :] = m_re[pp * st:(pp + 1) * st].astype(BF16)
            mout_s[pp, (2 * d + 1) * st:(2 * d + 2) * st, :] = m_im[pp * st:(pp + 1) * st].astype(BF16)

        spow = (lambda i: q - 1 - blk(i)) if d == 0 else blk
        pt_re = _place_nt(_sel_rows(PAIR_IN, pbase, spow), a1)
        pt_im = _place_nt(_sel_rows(PAIR_IN, pbase + pw, spow), a1)
        bt_re = _place_nt(_sel_rows(PAIR_IN, bbase, ch), a2)
        bt_im = _place_nt(_sel_rows(PAIR_IN, bbase + cw, ch), a2)
        w_re = jnp.where(same_is, pt_re * bt_re - pt_im * bt_im, 0.0)
        w_im = jnp.where(same_is, pt_re * bt_im + pt_im * bt_re, 0.0)
        for pp in range(tp):
            wsin_s[pp, :, (2 * d) * st:(2 * d + 1) * st] = w_re[:, pp * st:(pp + 1) * st].astype(BF16)
            wsin_s[pp, :, (2 * d + 1) * st:(2 * d + 2) * st] = w_im[:, pp * st:(pp + 1) * st].astype(BF16)

        k_re, k_im = c_times(*power(blk if d == 0 else (lambda i: q - 1 - blk(i))))
        b_re = jnp.where(same_ks, _place_nt(_sel_rows(PAIR_CH, bbase, ch), a2), 0.0)
        b_im = jnp.where(same_ks, _place_nt(_sel_rows(PAIR_CH, bbase + cw, ch), a2), 0.0)
        tk_bwd = []
        for pp in range(tp):
            sl = slice(pp * st, (pp + 1) * st)
            lagk = (jnp.dot(b_re[:, sl], k_re[sl], preferred_element_type=F32, precision=hp)
                    + jnp.dot(b_im[:, sl], k_im[sl], preferred_element_type=F32, precision=hp))
            l1, l2, l3 = _split3(lagk)
            rep = _dot(erow, l1) + _dot(erow, l2) + _dot(erow, l3)
            blocks = []
            for s in range(q):
                piece = rep[s * PAIR_CH:(s + 1) * PAIR_CH]
                if d == 0:
                    shifted = piece if s == 0 else pltpu.roll(piece, s * PAIR_CH, 1)
                    blocks.append(jnp.where(blk(icol) >= s, shifted, 0.0))
                else:
                    shift = ((s + 1) * PAIR_CH) % PAIR_IN
                    shifted = piece if shift == 0 else pltpu.roll(piece, shift, 1)
                    blocks.append(jnp.where(blk(icol) <= s, shifted, 0.0))
            tk_bwd.append(blocks)
        if d == 0:
            tk_fwd = tk_bwd
        else:
            for pp in range(tp):
                for s in range(q):
                    tk_s[pp, s * PAIR_CH:(s + 1) * PAIR_CH, :] = (
                        tk_fwd[pp][s] + tk_bwd[pp][s]).astype(BF16)


def _lane_piece_mask(piece):
    lane = lax.broadcasted_iota(jnp.int32, (1, LANES), 1)
    return (lane // PAIR_CH) == piece


def _s5_kernel(u_ref, a1_ref, a2_ref, lamd_ref, lamp_ref, d_ref, y_ref,
               x_s, y_s, g_s, tk_s, wsin_s, mout_s, *, nch, nbat):
    q = SSM_CHUNK
    rows = nbat * nch
    spt = LANES // PAIR_CH

    @pl.when(pl.program_id(1) == 0)
    def _():
        _s5_build(a1_ref, a2_ref, tk_s, wsin_s, mout_s)

    step_rows = lambda s: u_ref[:, s].astype(F32).reshape(rows, LANES)
    masks = [_lane_piece_mask(j) for j in range(spt)]
    for kt in range(q // spt):
        rolled = []
        for j in range(spt):
            xs = step_rows(kt * spt + j)
            rolled.append([xs if sh == 0 else pltpu.roll(xs, sh * PAIR_CH, 1) for sh in range(spt)])
        for pp in range(TILE_PAIRS):
            tile = None
            for j in range(spt):
                src = rolled[j][(j - pp) % spt]
                tile = src if tile is None else jnp.where(masks[j], src, tile)
            x_s[pp, :, kt * LANES:(kt + 1) * LANES] = tile.astype(BF16)

    for pp in range(TILE_PAIRS):
        _s5_pair(x_s.at[pp], y_s.at[pp], g_s, tk_s.at[pp], wsin_s.at[pp], mout_s.at[pp],
                 lamd_ref.at[pp], lamp_ref.at[pp], nch=nch, nbat=nbat)

    for kt in range(q // spt):
        rolled = []
        for pp in range(TILE_PAIRS):
            yt = y_s[pp, :, kt * LANES:(kt + 1) * LANES]
            rolled.append([yt if sh == 0 else pltpu.roll(yt, sh * PAIR_CH, 1) for sh in range(spt)])
        for j in range(spt):
            tile = None
            for pp in range(TILE_PAIRS):
                src = rolled[pp][(pp - j) % spt]
                tile = src if tile is None else jnp.where(masks[pp], src, tile)
            s = kt * spt + j
            out = tile + d_ref[...] * step_rows(s)
            y_ref[:, s] = out.reshape(y_ref.shape[0], y_ref.shape[2], LANES).astype(y_ref.dtype)


def _s5_pair(x_ref, y_ref, g_s, tk_ref, wsin_ref, mout_ref, lamd_ref, lamp_ref, *, nch, nbat):
    rows = nbat * nch
    st = PAIR_STATE
    n_tiles = nch // SUBLANES
    x = x_ref[...]

    sloc = _dot(x, wsin_ref[...])
    chunk = lax.broadcasted_iota(jnp.int32, (rows, 1), 0) & (nch - 1)
    rit = _row_in_tile(rows)
    comps = []
    for comp in range(N_COMP):
        reverse = comp >= 2
        piece = sloc[:, comp * st:(comp + 1) * st]
        if reverse:
            piece = jnp.where(chunk == nch - 1, 0.0, pltpu.roll(piece, rows - 1, 0))
        else:
            piece = jnp.where(chunk == 0, 0.0, pltpu.roll(piece, 1, 0))
        comps.append(piece)

    for di, d in enumerate(SCAN_STEPS):
        for base in (0, 2):
            reverse = base == 2
            lr = lamd_ref[di * N_COMP + base:di * N_COMP + base + 1, :]
            li = lamd_ref[di * N_COMP + base + 1:di * N_COMP + base + 2, :]
            if reverse:
                take = rit < SUBLANES - d
                shift = rows - d
            else:
                take = rit >= d
                shift = d
            gr, gi = comps[base], comps[base + 1]
            sr = jnp.where(take, pltpu.roll(gr, shift, 0), 0.0)
            si = jnp.where(take, pltpu.roll(gi, shift, 0), 0.0)
            comps[base] = gr + lr * sr - li * si
            comps[base + 1] = gi + lr * si + li * sr
    for comp in range(N_COMP):
        g_s[:, comp * st:(comp + 1) * st] = comps[comp]

    pf_r, pf_i = lamp_ref[0:SUBLANES, :], lamp_ref[SUBLANES:2 * SUBLANES, :]
    pb_r, pb_i = lamp_ref[2 * SUBLANES:3 * SUBLANES, :], lamp_ref[3 * SUBLANES:4 * SUBLANES, :]

    def step(k, carry):
        out = []
        for bi in range(nbat):
            cfr, cfi, cbr, cbi = carry[4 * bi:4 * bi + 4]
            rf = pl.multiple_of(bi * nch + k * SUBLANES, SUBLANES)
            rb = pl.multiple_of(bi * nch + (n_tiles - 1 - k) * SUBLANES, SUBLANES)
            gfr = g_s[pl.ds(rf, SUBLANES), 0:st] + pf_r * cfr - pf_i * cfi
            gfi = g_s[pl.ds(rf, SUBLANES), st:2 * st] + pf_r * cfi + pf_i * cfr
            gbr = g_s[pl.ds(rb, SUBLANES), 2 * st:3 * st] + pb_r * cbr - pb_i * cbi
            gbi = g_s[pl.ds(rb, SUBLANES), 3 * st:4 * st] + pb_r * cbi + pb_i * cbr
            g_s[pl.ds(rf, SUBLANES), 0:st] = gfr
            g_s[pl.ds(rf, SUBLANES), st:2 * st] = gfi
            g_s[pl.ds(rb, SUBLANES), 2 * st:3 * st] = gbr
            g_s[pl.ds(rb, SUBLANES), 3 * st:4 * st] = gbi
            last = lambda v: jnp.broadcast_to(v[SUBLANES - 1:SUBLANES, :], (SUBLANES, st))
            first = lambda v: jnp.broadcast_to(v[0:1, :], (SUBLANES, st))
            out += [last(gfr), last(gfi), first(gbr), first(gbi)]
        return tuple(out)

    z = jnp.zeros((SUBLANES, st), F32)
    lax.fori_loop(0, n_tiles, step, (z,) * (4 * nbat))

    y_ref[...] = _dot(x, tk_ref[...]) + _dot(g_s[...].astype(BF16), mout_ref[...])


def _s5(layer, u2, a1, a2, lamd, lamp, dskip, bsz, seq, nbat):
    q = SSM_CHUNK
    nch = seq // q
    assert nch & (nch - 1) == 0 and nch % SUBLANES == 0, "chunks per sequence must be a power of two"
    rows = nbat * nch
    assert (nbat * seq) % SSM_TILE == 0
    nt = nbat * seq // SSM_TILE
    cpt = SSM_TILE // q
    st = PAIR_STATE
    tp = TILE_PAIRS
    tile_spec = lambda shape: pl.BlockSpec(
        (None, None) + shape, lambda o, i: (layer, o) + (0,) * len(shape))
    io_spec = pl.BlockSpec((nt, q, cpt, LANES), lambda o, i: (i, 0, 0, o))
    u4 = u2.reshape(u2.shape[0] // SSM_TILE, q, cpt, BRANCH_WIDTH)
    return pl.pallas_call(
        functools.partial(_s5_kernel, nch=nch, nbat=nbat),
        grid=(BRANCH_WIDTH // LANES, bsz // nbat),
        in_specs=[
            io_spec,
            tile_spec((tp, st, LANES)),
            tile_spec((tp, st, LANES)),
            tile_spec((tp, len(SCAN_STEPS) * N_COMP, st)),
            tile_spec((tp, N_COMP * SUBLANES, st)),
            tile_spec((1, LANES)),
        ],
        out_specs=io_spec,
        out_shape=jax.ShapeDtypeStruct(u4.shape, BF16),
        scratch_shapes=[pltpu.VMEM((tp, rows, PAIR_IN), BF16),
                        pltpu.VMEM((tp, rows, PAIR_IN), F32),
                        pltpu.VMEM((rows, N_COMP * st), F32),
                        pltpu.VMEM((tp, PAIR_IN, PAIR_IN), BF16),
                        pltpu.VMEM((tp, PAIR_IN, N_COMP * st), BF16),
                        pltpu.VMEM((tp, N_COMP * st, PAIR_IN), BF16)],
        compiler_params=pltpu.CompilerParams(
            dimension_semantics=("parallel", "arbitrary"), vmem_limit_bytes=VMEM_LIMIT),
        name="s5_chunked",
    )(u4, a1, a2, lamd, lamp, dskip).reshape(u2.shape)


def _s5_params(a_re, a_im, log_step, b_re, b_im, c_re, c_im, d_skip):
    q = SSM_CHUNK
    g, n, c = SSM_GROUPS, SSM_STATE, SSM_GROUP
    no, og = SSM_PAIRS, 2
    step = jnp.exp(log_step)[..., None]
    dr = a_re * step
    di = a_im * step
    mag = jnp.exp(dr)
    abar_r = mag * jnp.cos(di)
    abar_i = mag * jnp.sin(di)
    den = a_re * a_re + a_im * a_im
    nr = abar_r - 1.0
    sr = (nr * a_re + abar_i * a_im) / den
    si = (abar_i * a_re - nr * a_im) / den
    bb_r = sr[..., None] * b_re - si[..., None] * b_im
    bb_i = sr[..., None] * b_im + si[..., None] * b_re

    def powers(kk):
        kk = jnp.asarray(kk, F32)
        pmag = jnp.exp(dr[..., None] * kk)
        return pmag * jnp.cos(di[..., None] * kk), pmag * jnp.sin(di[..., None] * kk)

    pr, pi = powers(jnp.arange(LANES // 4))
    a1 = jnp.concatenate([pr[0], pi[0], pr[1], pi[1]], axis=-1).reshape(no, PAIR_STATE, LANES)
    by_state = lambda w_: w_.reshape(no, PAIR_STATE, c)
    a2 = jnp.concatenate(
        [by_state(c_re.transpose(0, 2, 1)), by_state(c_im.transpose(0, 2, 1)),
         by_state(bb_r[0]), by_state(bb_i[0]), by_state(bb_r[1]), by_state(bb_i[1]),
         jnp.zeros((no, PAIR_STATE, LANES - 6 * c), F32)], axis=-1)

    def scan_rows(kk):
        sr_, si_ = powers(q * jnp.asarray(kk))
        rows = jnp.stack([sr_[0], si_[0], sr_[1], si_[1]], axis=0)
        return rows.reshape(N_COMP, no, og, n, len(kk))

    lamd = scan_rows(SCAN_STEPS).transpose(1, 4, 0, 2, 3).reshape(
        no, len(SCAN_STEPS) * N_COMP, PAIR_STATE)
    fwd = scan_rows(list(range(1, SUBLANES + 1)))[:2]
    bwd = scan_rows(list(range(SUBLANES, 0, -1)))[2:]
    lamp = jnp.concatenate([fwd, bwd], axis=0).transpose(1, 0, 4, 2, 3).reshape(
        no, N_COMP * SUBLANES, PAIR_STATE)
    dskip = d_skip.reshape(BRANCH_WIDTH // LANES, 1, LANES)
    by_tile = lambda w_: w_.reshape((BRANCH_WIDTH // LANES, TILE_PAIRS) + w_.shape[1:])
    return by_tile(a1), by_tile(a2), by_tile(lamd), by_tile(lamp), dskip


def _merge_kernel(sink_ref, x_ref, g_ref, bias_ref, q_ref, kvp_ref, kvc_ref, kvn_ref, hs_ref, gr_ref,
                  ys_ref, unperm_ref, wgate_ref, wb_ref, wglu_ref, bglu_ref, wout_ref, o_ref,
                  *, nb, layer):
    x = x_ref[...]
    hn = _rms(x, g_ref[...]).astype(BF16)
    bw = BRANCH_WIDTH
    blocks = x.shape[0] // BLOCK
    kv = jnp.concatenate([kvp_ref[...], kvc_ref[...], kvn_ref[...]], axis=0).astype(F32)
    first = pl.program_id(0) * blocks
    o_attn = jnp.concatenate(
        [jnp.concatenate(_attn_block(sink_ref, bias_ref, q_ref[j * BLOCK:(j + 1) * BLOCK, :],
                                     kv[j * BLOCK:(j + 3) * BLOCK], (first + j) % nb, nb, layer),
                         axis=1) for j in range(blocks)], axis=0)
    o_lru = (jax.nn.gelu(gr_ref[...].astype(F32)) * hs_ref[...]).astype(BF16)
    zg = _dot(jax.nn.gelu(ys_ref[...].astype(F32)).astype(BF16), wglu_ref[...]) + bglu_ref[...]
    o_ssm = (zg[:, :bw] * jnp.tanh(zg[:, bw:]) + zg[:, :bw]).astype(BF16)
    o_ssm = _dot(unperm_ref[...], o_ssm).astype(BF16)
    mixed = None
    for kb, ob in enumerate((o_attn, o_lru, o_ssm)):
        tz = jnp.tanh(_dot(hn, wgate_ref[:, kb * D_MODEL:(kb + 1) * D_MODEL]))
        proj = _dot(ob, wb_ref[kb])
        term = proj * tz + proj
        mixed = term if mixed is None else mixed + term
    o_ref[...] = x + _dot(mixed.astype(BF16), wout_ref[...])


def _merge(layer, x2, g, sink, qkv, seq, h_sum, gr, y_ssm, unperm, w_gate, w_branch, w_glu, b_glu,
           w_out):
    t = x2.shape[0]
    tm = SSM_TILE
    assert seq % BLOCK == 0 and tm % BLOCK == 0
    blocks = tm // BLOCK
    kv_col = BRANCH_WIDTH // (2 * KV_WIDTH)
    row = lambda w: pl.BlockSpec((tm, w), lambda i: (i, 0))
    halo = lambda f: pl.BlockSpec((BLOCK, 2 * KV_WIDTH), lambda i: (f(i), kv_col))
    return pl.pallas_call(
        functools.partial(_merge_kernel, nb=seq // BLOCK, layer=layer),
        grid=(t // tm,),
        in_specs=[pl.BlockSpec(memory_space=pltpu.SMEM),
                  row(D_MODEL), _layer_spec(layer, (1, D_MODEL)),
                  _const_spec((N_KV_HEADS, HEADS_PER_KV * BLOCK, 3 * BLOCK)),
                  row(BRANCH_WIDTH),
                  halo(lambda i: jnp.maximum(i * blocks - 1, 0)),
                  pl.BlockSpec((tm, 2 * KV_WIDTH), lambda i: (i, kv_col)),
                  halo(lambda i: jnp.minimum((i + 1) * blocks, t // BLOCK - 1)),
                  row(BRANCH_WIDTH), row(BRANCH_WIDTH), row(BRANCH_WIDTH), _const_spec((tm, tm)),
                  _layer_spec(layer, (D_MODEL, N_BRANCH * D_MODEL)),
                  _layer_spec(layer, (N_BRANCH, BRANCH_WIDTH, D_MODEL)),
                  _layer_spec(layer, (BRANCH_WIDTH, 2 * BRANCH_WIDTH)),
                  _layer_spec(layer, (1, 2 * BRANCH_WIDTH)),
                  _layer_spec(layer, (D_MODEL, D_MODEL))],
        out_specs=row(D_MODEL),
        out_shape=jax.ShapeDtypeStruct((t, D_MODEL), F32),
        compiler_params=pltpu.CompilerParams(
            dimension_semantics=("parallel",), vmem_limit_bytes=VMEM_LIMIT),
        name="merge",
    )(sink, x2, g, jnp.asarray(_attn_bias()), qkv, qkv, qkv, qkv, h_sum, gr, y_ssm, unperm,
      w_gate, w_branch, w_glu, b_glu, w_out)


def _route(logits):
    lane = lax.broadcasted_iota(jnp.int32, logits.shape, 1).astype(F32)
    neg = -jnp.inf
    first = lambda mask: jnp.min(jnp.where(mask, lane, float(ROUTE_COLS)), axis=-1, keepdims=True)
    gmask = lane < N_GROUPS
    gmax = jnp.max(jnp.where(gmask, logits, neg), axis=-1, keepdims=True)
    gidx = first(gmask & (logits == gmax))
    gsum = jnp.sum(jnp.where(gmask, jnp.exp(logits - gmax), 0.0), axis=-1, keepdims=True)
    g_w = 1.0 / gsum
    e_lo = N_GROUPS + EXPERTS_PER_GROUP * gidx
    emask = (lane >= e_lo) & (lane < e_lo + EXPERTS_PER_GROUP)
    v1 = jnp.max(jnp.where(emask, logits, neg), axis=-1, keepdims=True)
    i1 = first(emask & (logits == v1))
    emask2 = emask & (lane != i1)
    v2 = jnp.max(jnp.where(emask2, logits, neg), axis=-1, keepdims=True)
    i2 = first(emask2 & (logits == v2))
    t = jnp.exp(v2 - v1)
    w1 = 1.0 / (1.0 + t)
    w2 = t * w1
    cw = jnp.where(lane == i1, w1 * g_w, jnp.where(lane == i2, w2 * g_w, 0.0))
    return cw, jnp.where(lane == gidx, 1.0, 0.0)


def _moe_kernel(x_ref, g_ref, wr_ref, br_ref, tril_ref, w13_ref, w2_ref, fg_ref, o_ref,
                xs_s, cws_s, ys_s, *, final_norm):
    x = x_ref[...]
    tm = x.shape[0]
    ns = xs_s.shape[0]
    hn = _rms(x, g_ref[...])
    hb = hn.astype(BF16)
    lo = (hn - hb.astype(F32)).astype(BF16)
    hb_w = _dot(hb, wr_ref[...])
    logits = (hb_w[:, :ROUTE_COLS] + hb_w[:, ROUTE_COLS:] + _dot(lo, wr_ref[:, :ROUTE_COLS])
              + br_ref[...])
    cw, ghot = _route(logits)

    lane = lax.broadcasted_iota(jnp.int32, (1, ROUTE_COLS), 1)
    rank = _dot(tril_ref[...], ghot.astype(BF16))
    total = jnp.sum(ghot, axis=0, keepdims=True)
    counts, offsets = [], []
    off = jnp.int32(0)
    for grp in range(N_GROUPS):
        cnt = jnp.sum(jnp.where(lane == grp, total, 0.0)).astype(jnp.int32)
        counts.append(cnt)
        offsets.append(off)
        off = off + ((cnt + (MOE_ALIGN - 1)) // MOE_ALIGN) * MOE_ALIGN
    off_vec = jnp.zeros((1, ROUTE_COLS), F32)
    for grp in range(1, N_GROUPS):
        off_vec = jnp.where(lane == grp, offsets[grp].astype(F32), off_vec)
    pos_terms = ghot * (rank + off_vec)
    pos_col = jnp.sum(pos_terms, axis=-1, keepdims=True)
    pos_row = _place_nt(jnp.ones((SUBLANES, ROUTE_COLS), BF16), pos_terms)[0:1]
    sort = (lax.broadcasted_iota(jnp.int32, (ns, tm), 0).astype(F32) == pos_row).astype(BF16)
    unsort = (lax.broadcasted_iota(jnp.int32, (tm, ns), 1).astype(F32) == pos_col).astype(BF16)

    xs_s[...] = _dot(sort, hb).astype(BF16)
    c1 = cw.astype(BF16)
    c2 = (cw - c1.astype(F32)).astype(BF16)
    cw_sorted = _dot(sort, jnp.concatenate([c1, c2], axis=1))
    cws_s[...] = cw_sorted[:, :ROUTE_COLS] + cw_sorted[:, ROUTE_COLS:]
    ys_s[...] = jnp.zeros_like(ys_s)

    for grp in range(N_GROUPS):
        def chunk(k, carry, grp=grp):
            r0 = pl.multiple_of(offsets[grp] + k * MOE_CHUNK, MOE_ALIGN)
            h = _dot(xs_s[pl.ds(r0, MOE_CHUNK), :], w13_ref[grp])
            cwb = cws_s[pl.ds(r0, MOE_CHUNK), :]
            cols = []
            for e in range(EXPERTS_PER_GROUP):
                col = N_GROUPS + grp * EXPERTS_PER_GROUP + e
                wcol = jnp.sum(jnp.where(lane == col, cwb, 0.0), axis=-1, keepdims=True)
                cols.append(jnp.broadcast_to(wcol, (MOE_CHUNK, D_EXPERT)))
            h1 = h[:, :GROUP_HIDDEN]
            hg = (h1 * _sigmoid(h1) * h[:, GROUP_HIDDEN:] * jnp.concatenate(cols, axis=1))
            ys_s[pl.ds(r0, MOE_CHUNK), :] = _dot(hg.astype(BF16), w2_ref[grp])
            return carry

        lax.fori_loop(0, (counts[grp] + (MOE_CHUNK - 1)) // MOE_CHUNK, chunk, 0)

    y = x + _dot(unsort, ys_s[...].astype(BF16))
    if final_norm:
        y = _rms(y, fg_ref[...])
    o_ref[...] = y


def _moe(layer, x2, g, w_route, b_route, w13, w2, final_g, tm, final_norm):
    t = x2.shape[0]
    ns = -(-(tm + N_GROUPS * (MOE_ALIGN - 1) + MOE_CHUNK) // LANES) * LANES
    row = pl.BlockSpec((tm, D_MODEL), lambda i: (i, 0))
    once = pl.Buffered(1)
    tril = jnp.asarray(np.tril(np.ones((tm, tm), np.float32), -1), BF16)
    return pl.pallas_call(
        functools.partial(_moe_kernel, final_norm=final_norm),
        grid=(t // tm,),
        in_specs=[row,
                  _layer_spec(layer, (1, D_MODEL)),
                  _layer_spec(layer, (D_MODEL, 2 * ROUTE_COLS)),
                  _layer_spec(layer, (1, ROUTE_COLS)),
                  _const_spec((tm, tm)),
                  _layer_spec(layer, (N_GROUPS, D_MODEL, 2 * GROUP_HIDDEN), pipeline_mode=once),
                  _layer_spec(layer, (N_GROUPS, GROUP_HIDDEN, D_MODEL), pipeline_mode=once),
                  _const_spec((1, D_MODEL))],
        out_specs=row,
        out_shape=jax.ShapeDtypeStruct((t, D_MODEL), F32),
        scratch_shapes=[pltpu.VMEM((ns, D_MODEL), BF16), pltpu.VMEM((ns, ROUTE_COLS), F32),
                        pltpu.VMEM((ns, D_MODEL), F32)],
        compiler_params=pltpu.CompilerParams(
            dimension_semantics=("parallel",), vmem_limit_bytes=VMEM_LIMIT),
        name="moe_final" if final_norm else "moe",
    )(x2, g, w_route, b_route, tril, w13, w2, final_g)


def _prepare(p):
    w_in = p['w_in']
    depth = w_in.shape[0]
    pad = ROUTE_COLS - N_GROUPS - N_EXPERTS
    w_route = jnp.concatenate(
        [p['moe_w_group'], p['moe_w_expert'], jnp.zeros((depth, D_MODEL, pad), F32)], axis=-1)
    wr_hi = w_route.astype(BF16)
    grp = lambda w: w.transpose(0, 1, 3, 2, 4).reshape(depth, N_GROUPS, D_MODEL, GROUP_HIDDEN)
    s5_a1, s5_a2, lamd, lamp, dskip = jax.vmap(_s5_params)(
        p['ssm_a_re'], p['ssm_a_im'], p['ssm_log_step'], p['ssm_b_re'], p['ssm_b_im'],
        p['ssm_c_re'], p['ssm_c_im'], p['ssm_d'])
    return dict(
        norm1_g=p['norm1_g'][:, None],
        w_main=w_in[:, :, :MAIN_COLS].astype(BF16),
        w_gate=(0.5 * w_in[:, :, MAIN_COLS:]).astype(BF16),
        attn_sink=p['attn_sink'],
        conv_w=p['lru_conv_w'],
        conv_b=p['lru_conv_b'][:, None],
        lru_wg=(0.5 * jnp.concatenate([_block_diag(p['lru_w_r']), _block_diag(p['lru_w_i'])],
                                      axis=-1)).astype(BF16),
        lru_bg=0.5 * jnp.concatenate([p['lru_b_r'], p['lru_b_i']], axis=-1)[:, :, None],
        lru_c=(-0.5 * LRU_C) * jax.nn.softplus(-p['lru_lambda'])[:, :, None],
        s5_a1=s5_a1, s5_a2=s5_a2, lamd=lamd, lamp=lamp, dskip=dskip,
        w_branch=(0.5 * p['w_branch']).astype(BF16),
        w_glu=(0.5 * p['ssm_w_glu']).astype(BF16),
        b_glu=0.5 * p['ssm_b_glu'][:, None],
        w_out=p['w_out'].astype(BF16),
        norm2_g=p['norm2_g'][:, None],
        w_route=jnp.concatenate([wr_hi, (w_route - wr_hi.astype(F32)).astype(BF16)], axis=-1),
        b_route=jnp.concatenate(
            [p['moe_b_group'], p['moe_b_expert'], jnp.zeros((depth, pad), F32)], axis=-1)[:, None],
        w13=jnp.concatenate([grp(p['moe_w1']), grp(p['moe_w3'])], axis=-1).astype(BF16),
        w2=p['moe_w2'].reshape(depth, N_GROUPS, GROUP_HIDDEN, D_MODEL).astype(BF16),
    )


def _layer(layer, x2, bsz, seq, w, final_g, final_norm):
    t = bsz * seq
    bw = BRANCH_WIDTH
    assert t % SSM_TILE == 0
    perm = _chunk_perm()
    qkv, xr, gr, u = _inproj(layer, x2, w['norm1_g'], w['w_main'], jnp.asarray(perm, BF16))
    h_sum = _lru(layer, xr.reshape(bsz, seq, bw), w['conv_w'], w['conv_b'], w['lru_wg'],
                 w['lru_bg'], w['lru_c']).reshape(t, bw)
    y_ssm = _s5(layer, u, w['s5_a1'], w['s5_a2'], w['lamd'], w['lamp'], w['dskip'],
                bsz, seq, _pick_tile(bsz, 4))
    x2 = _merge(layer, x2, w['norm1_g'], w['attn_sink'], qkv, seq, h_sum, gr, y_ssm,
                jnp.asarray(perm.T, BF16), w['w_gate'], w['w_branch'], w['w_glu'], w['b_glu'],
                w['w_out'])
    return _moe(layer, x2, w['norm2_g'], w['w_route'], w['b_route'], w['w13'], w['w2'],
                final_g, _pick_tile(t, 512), final_norm)


_PARAM_KEYS = ('norm1_g', 'w_in', 'attn_sink', 'lru_conv_w', 'lru_conv_b', 'lru_w_r', 'lru_b_r',
               'lru_w_i', 'lru_b_i', 'lru_lambda', 'ssm_a_re', 'ssm_a_im', 'ssm_log_step',
               'ssm_b_re', 'ssm_b_im', 'ssm_c_re', 'ssm_c_im', 'ssm_d', 'ssm_w_glu', 'ssm_b_glu',
               'w_branch', 'w_out', 'norm2_g', 'moe_w_group', 'moe_b_group', 'moe_w_expert',
               'moe_b_expert', 'moe_w1', 'moe_w3', 'moe_w2')


def kernel(x, norm1_g, w_in, attn_sink, lru_conv_w, lru_conv_b, lru_w_r, lru_b_r, lru_w_i, lru_b_i,
           lru_lambda, ssm_a_re, ssm_a_im, ssm_log_step, ssm_b_re, ssm_b_im, ssm_c_re, ssm_c_im,
           ssm_d, ssm_w_glu, ssm_b_glu, w_branch, w_out, norm2_g, moe_w_group, moe_b_group,
           moe_w_expert, moe_b_expert, moe_w1, moe_w3, moe_w2, final_norm_g):
    params = dict(zip(_PARAM_KEYS, (
        norm1_g, w_in, attn_sink, lru_conv_w, lru_conv_b, lru_w_r, lru_b_r, lru_w_i, lru_b_i,
        lru_lambda, ssm_a_re, ssm_a_im, ssm_log_step, ssm_b_re, ssm_b_im, ssm_c_re, ssm_c_im,
        ssm_d, ssm_w_glu, ssm_b_glu, w_branch, w_out, norm2_g, moe_w_group, moe_b_group,
        moe_w_expert, moe_b_expert, moe_w1, moe_w3, moe_w2)))
    bsz, seq, _ = x.shape
    depth = norm1_g.shape[0]
    w = _prepare(params)
    x2 = x.reshape(bsz * seq, D_MODEL)
    for layer in range(depth):
        x2 = _layer(layer, x2, bsz, seq, w, final_norm_g[None], final_norm=(layer == depth - 1))
    return x2.reshape(bsz, seq, D_MODEL)
```

```python
import functools

import jax
import jax.numpy as jnp
import numpy as np
from jax import lax
from jax.experimental import pallas as pl
from jax.experimental.pallas import tpu as pltpu

F32 = jnp.float32
BF16 = jnp.bfloat16

D_MODEL = 1024
N_Q_HEADS = 8
N_KV_HEADS = 2
HEAD_DIM = 64
WINDOW = 128
BLOCK = 128
BRANCH_WIDTH = 512
KV_WIDTH = N_KV_HEADS * HEAD_DIM
LRU_C = 8.0
SSM_GROUP = 16
SSM_GROUPS = BRANCH_WIDTH // SSM_GROUP
SSM_STATE = 64
N_BRANCH = 3
N_GROUPS = 4
EXPERTS_PER_GROUP = 4
N_EXPERTS = N_GROUPS * EXPERTS_PER_GROUP
D_EXPERT = 256
GROUP_HIDDEN = EXPERTS_PER_GROUP * D_EXPERT
NORM_EPS = 1e-6
MASK_VALUE = -1e30

LANES = 128
SUBLANES = 8
QKV_WIDTH = BRANCH_WIDTH + 2 * KV_WIDTH
MAIN_COLS = QKV_WIDTH + 3 * BRANCH_WIDTH
HEADS_PER_KV = N_Q_HEADS // N_KV_HEADS
SSM_CHUNK = 16
PAIR_CH = 2 * SSM_GROUP
SSM_PAIRS = SSM_GROUPS // 2
TILE_PAIRS = LANES // PAIR_CH
PAIR_IN = SSM_CHUNK * PAIR_CH
PAIR_STATE = 2 * SSM_STATE
SSM_TILE = 512
N_COMP = 4
SCAN_STEPS = (1, 2, 4)
ROUTE_COLS = LANES
MOE_ALIGN = 16
MOE_CHUNK = 128
CONV_HALO = 16
SQRT_CLAMP = 1e-30
VMEM_LIMIT = 56 * 1024 * 1024


def _rms(x, g):
    ms = jnp.mean(x * x, axis=-1, keepdims=True)
    return x * lax.rsqrt(ms + NORM_EPS) * g


def _sigmoid(x):
    return 0.5 * jnp.tanh(0.5 * x) + 0.5


def _dot(a, b):
    return jnp.dot(a, b, preferred_element_type=F32)


def _dot_nt(a, b):
    return lax.dot_general(a, b, (((1,), (1,)), ((), ())), preferred_element_type=F32)


def _const_spec(shape):
    nd = len(shape)
    return pl.BlockSpec(shape, lambda *_: (0,) * nd)


def _layer_spec(layer, shape, **kw):
    nd = len(shape)
    return pl.BlockSpec((None,) + tuple(shape), lambda *_: (layer,) + (0,) * nd, **kw)


def _row_in_tile(rows):
    return lax.broadcasted_iota(jnp.int32, (rows, 1), 0) & (SUBLANES - 1)


def _pick_tile(n, want):
    t = min(n, want)
    while n % t:
        t //= 2
    return t


def _chunk_perm():
    t = np.arange(SSM_TILE)
    dst = (t % SSM_CHUNK) * (SSM_TILE // SSM_CHUNK) + t // SSM_CHUNK
    perm = np.zeros((SSM_TILE, SSM_TILE), np.float32)
    perm[dst, t] = 1.0
    return perm


def _inproj_kernel(x_ref, g_ref, w_ref, perm_ref, qkv_ref, xr_ref, gr_ref, u_ref):
    hn = _rms(x_ref[...], g_ref[...]).astype(BF16)
    bw = BRANCH_WIDTH
    qkv_ref[...] = _dot(hn, w_ref[:, 0:QKV_WIDTH]).astype(BF16)
    xr_ref[...] = _dot(hn, w_ref[:, QKV_WIDTH:QKV_WIDTH + bw]).astype(BF16)
    gr_ref[...] = _dot(hn, w_ref[:, QKV_WIDTH + bw:QKV_WIDTH + 2 * bw]).astype(BF16)
    u = _dot(hn, w_ref[:, QKV_WIDTH + 2 * bw:MAIN_COLS]).astype(BF16)
    u_ref[...] = _dot(perm_ref[...], u).astype(BF16)


def _inproj(layer, x2, g, w_main, perm):
    t = x2.shape[0]
    tm = SSM_TILE
    row = lambda w: pl.BlockSpec((tm, w), lambda i: (i, 0))
    return pl.pallas_call(
        _inproj_kernel,
        grid=(t // tm,),
        in_specs=[row(D_MODEL), _layer_spec(layer, (1, D_MODEL)),
                  _layer_spec(layer, (D_MODEL, MAIN_COLS)), _const_spec((tm, tm))],
        out_specs=[row(QKV_WIDTH), row(BRANCH_WIDTH), row(BRANCH_WIDTH), row(BRANCH_WIDTH)],
        out_shape=[jax.ShapeDtypeStruct((t, QKV_WIDTH), BF16)]
        + [jax.ShapeDtypeStruct((t, BRANCH_WIDTH), BF16)] * 3,
        compiler_params=pltpu.CompilerParams(
            dimension_semantics=("parallel",), vmem_limit_bytes=VMEM_LIMIT),
        name="inproj",
    )(x2, g, w_main, perm)


def _attn_bias():
    qi = np.arange(BLOCK)[:, None]
    kj = np.arange(3 * BLOCK)[None, :]
    dist = np.abs(qi - kj + BLOCK)
    out = np.empty((N_KV_HEADS, HEADS_PER_KV * BLOCK, 3 * BLOCK), np.float32)
    for head in range(N_Q_HEADS):
        h, g = divmod(head, HEADS_PER_KV)
        slope = 2.0 ** (-8.0 * (head + 1) / N_Q_HEADS)
        out[h, g * BLOCK:(g + 1) * BLOCK] = np.where(dist <= WINDOW, -slope * dist, 1.0)
    return out


def _attn_block(sink_ref, bias_ref, q, kv, n, nb, layer):
    k = kv[:, :KV_WIDTH]
    v = kv[:, KV_WIDTH:]
    lane = lax.broadcasted_iota(jnp.int32, (1, KV_WIDTH), 1)
    lo = lane < HEAD_DIM

    rows = HEADS_PER_KV * BLOCK
    kj = lax.broadcasted_iota(jnp.int32, (1, 3 * BLOCK), 1)
    in_seq = ((kj >= BLOCK) | (n > 0)) & ((kj < 2 * BLOCK) | (n < nb - 1))
    valid = (bias_ref[0] <= 0.0) & in_seq
    head_in_kv = lax.broadcasted_iota(jnp.int32, (rows, 1), 0) // BLOCK

    scale = HEAD_DIM ** -0.5
    outs = [None] * (N_Q_HEADS // 2)
    for h in range(N_KV_HEADS):
        own = lo if h == 0 else jnp.logical_not(lo)
        km = jnp.where(own, k, 0.0)
        vm = jnp.where(own, v, 0.0)
        k2 = ((km + pltpu.roll(km, HEAD_DIM, 1)) * scale).astype(BF16)
        v2 = (vm + pltpu.roll(vm, HEAD_DIM, 1)).astype(BF16)
        parts = []
        sink = jnp.zeros((rows, 1), F32)
        for g in range(HEADS_PER_KV):
            head = h * HEADS_PER_KV + g
            pair = head // 2
            qp = q[:, pair * KV_WIDTH:(pair + 1) * KV_WIDTH]
            keep = lo if head % 2 == 0 else jnp.logical_not(lo)
            parts.append(jnp.where(keep, qp, jnp.zeros_like(qp)))
            sink = jnp.where(head_in_kv == g, sink_ref[layer, head], sink)
        s = _dot_nt(jnp.concatenate(parts, axis=0), k2)
        s = jnp.where(valid, s + bias_ref[h], MASK_VALUE)
        m = jnp.maximum(jnp.max(s, axis=-1, keepdims=True), sink)
        p = jnp.exp(s - m)
        denom = jnp.sum(p, axis=-1, keepdims=True) + jnp.exp(sink - m)
        r = _dot(p.astype(BF16), v2) / denom
        for j in range(HEADS_PER_KV // 2):
            even = r[(2 * j) * BLOCK:(2 * j + 1) * BLOCK]
            odd = r[(2 * j + 1) * BLOCK:(2 * j + 2) * BLOCK]
            outs[h * (HEADS_PER_KV // 2) + j] = jnp.where(lo, even, odd).astype(BF16)
    return outs


def _tile_scan(a, u, reverse):
    rows = a.shape[0]
    rit = _row_in_tile(rows)
    for d in SCAN_STEPS:
        if reverse:
            take = rit < SUBLANES - d
            shift = rows - d
        else:
            take = rit >= d
            shift = d
        a_sh = jnp.where(take, pltpu.roll(a, shift, 0), 1.0)
        u_sh = jnp.where(take, pltpu.roll(u, shift, 0), 0.0)
        u = u + a * u_sh
        a = a * a_sh
    return a, u


def _lru_kernel(x_ref, cw_ref, cb_ref, wg_ref, bg_ref, c_ref, o_ref, xe_s, xc_s, a_s, u_s, hf_s,
                *, seq, rc):
    phase = pl.program_id(1)
    bw = BRANCH_WIDTH
    n_tiles = seq // SUBLANES

    def conv():
        zeros = jnp.zeros((CONV_HALO, bw), F32)
        xe_s[0:CONV_HALO, :] = zeros
        xe_s[CONV_HALO + seq:CONV_HALO + seq + CONV_HALO, :] = zeros
        xe_s[CONV_HALO:CONV_HALO + seq, :] = x_ref[...].astype(F32)
        cw = cw_ref[...]
        for c in range(seq // rc):
            base = CONV_HALO + c * rc
            xc = cb_ref[...]
            for tap in range(4):
                xc = xc + cw[tap:tap + 1] * xe_s[base + tap - 2:base + tap - 2 + rc, :]
            xc_s[c * rc:(c + 1) * rc, :] = xc

    def build(reverse):
        for c in range(seq // rc):
            xc = xc_s[c * rc:(c + 1) * rc, :]
            tz = jnp.tanh(_dot(xc.astype(BF16), wg_ref[...]) + bg_ref[...])
            log_a = c_ref[...] * tz[:, :bw] + c_ref[...]
            a = jnp.exp(log_a)
            t = 1.0 - a * a
            xh = 0.5 * xc
            u = t * lax.rsqrt(jnp.maximum(t, SQRT_CLAMP)) * (tz[:, bw:] * xh + xh)
            a, u = _tile_scan(a, u, reverse)
            a_s[c * rc:(c + 1) * rc, :] = a
            u_s[c * rc:(c + 1) * rc, :] = u

    @pl.when(phase == 0)
    def _():
        conv()
        build(False)

        def step(k, carry):
            r0 = pl.multiple_of(k * SUBLANES, SUBLANES)
            h = u_s[pl.ds(r0, SUBLANES), :] + a_s[pl.ds(r0, SUBLANES), :] * carry
            hf_s[pl.ds(r0, SUBLANES), :] = h
            return jnp.broadcast_to(h[SUBLANES - 1:SUBLANES, :], (SUBLANES, bw))

        lax.fori_loop(0, n_tiles, step, jnp.zeros((SUBLANES, bw), F32), unroll=4)

    @pl.when(phase == 1)
    def _():
        build(True)

        def step(k, carry):
            r0 = pl.multiple_of((n_tiles - 1 - k) * SUBLANES, SUBLANES)
            h = u_s[pl.ds(r0, SUBLANES), :] + a_s[pl.ds(r0, SUBLANES), :] * carry
            o_ref[pl.ds(r0, SUBLANES), :] = h + hf_s[pl.ds(r0, SUBLANES), :]
            return jnp.broadcast_to(h[0:1, :], (SUBLANES, bw))

        lax.fori_loop(0, n_tiles, step, jnp.zeros((SUBLANES, bw), F32), unroll=4)


def _lru(layer, xr3, conv_w, conv_b, wg, bg, sp):
    b, l, w = xr3.shape
    seq_spec = pl.BlockSpec((None, l, w), lambda bi, ph: (bi, 0, 0))
    per_dir = lambda shape: pl.BlockSpec(
        (None, None) + shape, lambda bi, ph: (layer, ph) + (0,) * len(shape))
    return pl.pallas_call(
        functools.partial(_lru_kernel, seq=l, rc=_pick_tile(l, 512)),
        grid=(b, 2),
        in_specs=[seq_spec, _layer_spec(layer, (4, w)), _layer_spec(layer, (1, w)),
                  per_dir((w, 2 * w)), per_dir((1, 2 * w)), per_dir((1, w))],
        out_specs=seq_spec,
        out_shape=jax.ShapeDtypeStruct((b, l, w), F32),
        scratch_shapes=[pltpu.VMEM((l + 2 * CONV_HALO, w), F32)] + [pltpu.VMEM((l, w), F32)] * 4,
        compiler_params=pltpu.CompilerParams(
            dimension_semantics=("parallel", "arbitrary"), vmem_limit_bytes=VMEM_LIMIT),
        name="rg_lru",
    )(xr3, conv_w, conv_b, wg, bg, sp)


def _block_diag(wb):
    nbk, bd = wb.shape[-3], wb.shape[-2]
    eye = jnp.eye(nbk, dtype=wb.dtype)
    out = jnp.einsum('...hij,hk->...hikj', wb, eye)
    return out.reshape(wb.shape[:-3] + (nbk * bd, nbk * bd))


def _split3(a):
    a1 = a.astype(BF16)
    r1 = a - a1.astype(F32)
    a2 = r1.astype(BF16)
    a3 = (r1 - a2.astype(F32)).astype(BF16)
    return a1, a2, a3


def _place(a, sel):
    p1, p2, p3 = _split3(a)
    return _dot(p1, sel) + _dot(p2, sel) + _dot(p3, sel)


def _place_nt(sel, a):
    p1, p2, p3 = _split3(a)
    return _dot_nt(sel, p1) + _dot_nt(sel, p2) + _dot_nt(sel, p3)


def _sel_cols(n_cols, base, idx_of_col):
    lane = lax.broadcasted_iota(jnp.int32, (LANES, n_cols), 0)
    col = lax.broadcasted_iota(jnp.int32, (LANES, n_cols), 1)
    return (lane == base + idx_of_col(col)).astype(BF16)


def _sel_rows(n_rows, base, idx_of_row):
    row = lax.broadcasted_iota(jnp.int32, (n_rows, LANES), 0)
    lane = lax.broadcasted_iota(jnp.int32, (n_rows, LANES), 1)
    return (lane == base + idx_of_row(row)).astype(BF16)


def _s5_build(a1_ref, a2_ref, tk_s, wsin_s, mout_s):
    q = SSM_CHUNK
    st = PAIR_STATE
    tp = TILE_PAIRS
    pw = LANES // 4
    cw = SSM_GROUP
    hp = lax.Precision.HIGHEST
    a1 = a1_ref[...].reshape(tp * st, LANES)
    a2 = a2_ref[...].reshape(tp * st, LANES)
    blk = lambda i: i >> 5
    grp = lambda i: (i >> 4) & 1
    ch = lambda i: i & (cw - 1)

    srow = lax.broadcasted_iota(jnp.int32, (tp * st, 1), 0)
    icol = lax.broadcasted_iota(jnp.int32, (1, PAIR_IN), 1)
    irow = lax.broadcasted_iota(jnp.int32, (PAIR_IN, 1), 0)
    scol = lax.broadcasted_iota(jnp.int32, (1, tp * st), 1)
    same_sc = ((srow >> 6) & 1) == grp(icol)
    same_is = grp(irow) == ((scol >> 6) & 1)
    kg_row = lax.broadcasted_iota(jnp.int32, (PAIR_CH, 1), 0)
    same_ks = grp(kg_row) == ((scol >> 6) & 1)
    erow = (lax.broadcasted_iota(jnp.int32, (PAIR_IN, PAIR_CH), 0) & (PAIR_CH - 1)
            == lax.broadcasted_iota(jnp.int32, (PAIR_IN, PAIR_CH), 1)).astype(BF16)

    c_re = _place(a2, _sel_cols(PAIR_IN, 0, ch))
    c_im = _place(a2, _sel_cols(PAIR_IN, cw, ch))

    def c_times(p_re, p_im):
        return (jnp.where(same_sc, c_re * p_re - c_im * p_im, 0.0),
                jnp.where(same_sc, -(c_re * p_im + c_im * p_re), 0.0))

    tk_fwd = None
    for d in range(2):
        pbase = 2 * d * pw
        bbase = 2 * cw + 2 * d * cw
        power = lambda f: (_place(a1, _sel_cols(PAIR_IN, pbase, f)),
                           _place(a1, _sel_cols(PAIR_IN, pbase + pw, f)))

        m_re, m_im = c_times(*power((lambda i: blk(i) + 1) if d == 0 else (lambda i: q - blk(i))))
        for pp in range(tp):
            mout_s[pp, (2 * d) * st:(2 * d + 1) * st, :] = m_re[pp * st:(pp + 1) * st].astype(BF16)
            mout_s[pp, (2 * d + 1) * st:(2 * d + 2) * st, :] = m_im[pp * st:(pp + 1) * st].astype(BF16)

        spow = (lambda i: q - 1 - blk(i)) if d == 0 else blk
        pt_re = _place_nt(_sel_rows(PAIR_IN, pbase, spow), a1)
        pt_im = _place_nt(_sel_rows(PAIR_IN, pbase + pw, spow), a1)
        bt_re = _place_nt(_sel_rows(PAIR_IN, bbase, ch), a2)
        bt_im = _place_nt(_sel_rows(PAIR_IN, bbase + cw, ch), a2)
        w_re = jnp.where(same_is, pt_re * bt_re - pt_im * bt_im, 0.0)
        w_im = jnp.where(same_is, pt_re * bt_im + pt_im * bt_re, 0.0)
        for pp in range(tp):
            wsin_s[pp, :, (2 * d) * st:(2 * d + 1) * st] = w_re[:, pp * st:(pp + 1) * st].astype(BF16)
            wsin_s[pp, :, (2 * d + 1) * st:(2 * d + 2) * st] = w_im[:, pp * st:(pp + 1) * st].astype(BF16)

        k_re, k_im = c_times(*power(blk if d == 0 else (lambda i: q - 1 - blk(i))))
        b_re = jnp.where(same_ks, _place_nt(_sel_rows(PAIR_CH, bbase, ch), a2), 0.0)
        b_im = jnp.where(same_ks, _place_nt(_sel_rows(PAIR_CH, bbase + cw, ch), a2), 0.0)
        tk_bwd = []
        for pp in range(tp):
            sl = slice(pp * st, (pp + 1) * st)
            lagk = (jnp.dot(b_re[:, sl], k_re[sl], preferred_element_type=F32, precision=hp)
                    + jnp.dot(b_im[:, sl], k_im[sl], preferred_element_type=F32, precision=hp))
            l1, l2, l3 = _split3(lagk)
            rep = _dot(erow, l1) + _dot(erow, l2) + _dot(erow, l3)
            blocks = []
            for s in range(q):
                piece = rep[s * PAIR_CH:(s + 1) * PAIR_CH]
                if d == 0:
                    shifted = piece if s == 0 else pltpu.roll(piece, s * PAIR_CH, 1)
                    blocks.append(jnp.where(blk(icol) >= s, shifted, 0.0))
                else:
                    shift = ((s + 1) * PAIR_CH) % PAIR_IN
                    shifted = piece if shift == 0 else pltpu.roll(piece, shift, 1)
                    blocks.append(jnp.where(blk(icol) <= s, shifted, 0.0))
            tk_bwd.append(blocks)
        if d == 0:
            tk_fwd = tk_bwd
        else:
            for pp in range(tp):
                for s in range(q):
                    tk_s[pp, s * PAIR_CH:(s + 1) * PAIR_CH, :] = (
                        tk_fwd[pp][s] + tk_bwd[pp][s]).astype(BF16)


def _lane_piece_mask(piece):
    lane = lax.broadcasted_iota(jnp.int32, (1, LANES), 1)
    return (lane // PAIR_CH) == piece


def _s5_kernel(u_ref, a1_ref, a2_ref, lamd_ref, lamp_ref, d_ref, y_ref,
               x_s, y_s, g_s, tk_s, wsin_s, mout_s, *, nch, nbat):
    q = SSM_CHUNK
    rows = nbat * nch
    spt = LANES // PAIR_CH

    @pl.when(pl.program_id(1) == 0)
    def _():
        _s5_build(a1_ref, a2_ref, tk_s, wsin_s, mout_s)

    step_rows = lambda s: u_ref[:, s].astype(F32).reshape(rows, LANES)
    masks = [_lane_piece_mask(j) for j in range(spt)]
    for kt in range(q // spt):
        rolled = []
        for j in range(spt):
            xs = step_rows(kt * spt + j)
            rolled.append([xs if sh == 0 else pltpu.roll(xs, sh * PAIR_CH, 1) for sh in range(spt)])
        for pp in range(TILE_PAIRS):
            tile = None
            for j in range(spt):
                src = rolled[j][(j - pp) % spt]
                tile = src if tile is None else jnp.where(masks[j], src, tile)
            x_s[pp, :, kt * LANES:(kt + 1) * LANES] = tile.astype(BF16)

    for pp in range(TILE_PAIRS):
        _s5_pair(x_s.at[pp], y_s.at[pp], g_s, tk_s.at[pp], wsin_s.at[pp], mout_s.at[pp],
                 lamd_ref.at[pp], lamp_ref.at[pp], nch=nch, nbat=nbat)

    for kt in range(q // spt):
        rolled = []
        for pp in range(TILE_PAIRS):
            yt = y_s[pp, :, kt * LANES:(kt + 1) * LANES]
            rolled.append([yt if sh == 0 else pltpu.roll(yt, sh * PAIR_CH, 1) for sh in range(spt)])
        for j in range(spt):
            tile = None
            for pp in range(TILE_PAIRS):
                src = rolled[pp][(pp - j) % spt]
                tile = src if tile is None else jnp.where(masks[pp], src, tile)
            s = kt * spt + j
            out = tile + d_ref[...] * step_rows(s)
            y_ref[:, s] = out.reshape(y_ref.shape[0], y_ref.shape[2], LANES).astype(y_ref.dtype)


def _s5_pair(x_ref, y_ref, g_s, tk_ref, wsin_ref, mout_ref, lamd_ref, lamp_ref, *, nch, nbat):
    rows = nbat * nch
    st = PAIR_STATE
    n_tiles = nch // SUBLANES
    x = x_ref[...]

    sloc = _dot(x, wsin_ref[...])
    chunk = lax.broadcasted_iota(jnp.int32, (rows, 1), 0) & (nch - 1)
    rit = _row_in_tile(rows)
    comps = []
    for comp in range(N_COMP):
        reverse = comp >= 2
        piece = sloc[:, comp * st:(comp + 1) * st]
        if reverse:
            piece = jnp.where(chunk == nch - 1, 0.0, pltpu.roll(piece, rows - 1, 0))
        else:
            piece = jnp.where(chunk == 0, 0.0, pltpu.roll(piece, 1, 0))
        comps.append(piece)

    for di, d in enumerate(SCAN_STEPS):
        for base in (0, 2):
            reverse = base == 2
            lr = lamd_ref[di * N_COMP + base:di * N_COMP + base + 1, :]
            li = lamd_ref[di * N_COMP + base + 1:di * N_COMP + base + 2, :]
            if reverse:
                take = rit < SUBLANES - d
                shift = rows - d
            else:
                take = rit >= d
                shift = d
            gr, gi = comps[base], comps[base + 1]
            sr = jnp.where(take, pltpu.roll(gr, shift, 0), 0.0)
            si = jnp.where(take, pltpu.roll(gi, shift, 0), 0.0)
            comps[base] = gr + lr * sr - li * si
            comps[base + 1] = gi + lr * si + li * sr
    for comp in range(N_COMP):
        g_s[:, comp * st:(comp + 1) * st] = comps[comp]

    pf_r, pf_i = lamp_ref[0:SUBLANES, :], lamp_ref[SUBLANES:2 * SUBLANES, :]
    pb_r, pb_i = lamp_ref[2 * SUBLANES:3 * SUBLANES, :], lamp_ref[3 * SUBLANES:4 * SUBLANES, :]

    def step(k, carry):
        out = []
        for bi in range(nbat):
            cfr, cfi, cbr, cbi = carry[4 * bi:4 * bi + 4]
            rf = pl.multiple_of(bi * nch + k * SUBLANES, SUBLANES)
            rb = pl.multiple_of(bi * nch + (n_tiles - 1 - k) * SUBLANES, SUBLANES)
            gfr = g_s[pl.ds(rf, SUBLANES), 0:st] + pf_r * cfr - pf_i * cfi
            gfi = g_s[pl.ds(rf, SUBLANES), st:2 * st] + pf_r * cfi + pf_i * cfr
            gbr = g_s[pl.ds(rb, SUBLANES), 2 * st:3 * st] + pb_r * cbr - pb_i * cbi
            gbi = g_s[pl.ds(rb, SUBLANES), 3 * st:4 * st] + pb_r * cbi + pb_i * cbr
            g_s[pl.ds(rf, SUBLANES), 0:st] = gfr
            g_s[pl.ds(rf, SUBLANES), st:2 * st] = gfi
            g_s[pl.ds(rb, SUBLANES), 2 * st:3 * st] = gbr
            g_s[pl.ds(rb, SUBLANES), 3 * st:4 * st] = gbi
            last = lambda v: jnp.broadcast_to(v[SUBLANES - 1:SUBLANES, :], (SUBLANES, st))
            first = lambda v: jnp.broadcast_to(v[0:1, :], (SUBLANES, st))
            out += [last(gfr), last(gfi), first(gbr), first(gbi)]
        return tuple(out)

    z = jnp.zeros((SUBLANES, st), F32)
    lax.fori_loop(0, n_tiles, step, (z,) * (4 * nbat))

    y_ref[...] = _dot(x, tk_ref[...]) + _dot(g_s[...].astype(BF16), mout_ref[...])


def _s5(layer, u2, a1, a2, lamd, lamp, dskip, bsz, seq, nbat):
    q = SSM_CHUNK
    nch = seq // q
    assert nch & (nch - 1) == 0 and nch % SUBLANES == 0, "chunks per sequence must be a power of two"
    rows = nbat * nch
    assert (nbat * seq) % SSM_TILE == 0
    nt = nbat * seq // SSM_TILE
    cpt = SSM_TILE // q
    st = PAIR_STATE
    tp = TILE_PAIRS
    tile_spec = lambda shape: pl.BlockSpec(
        (None, None) + shape, lambda o, i: (layer, o) + (0,) * len(shape))
    io_spec = pl.BlockSpec((nt, q, cpt, LANES), lambda o, i: (i, 0, 0, o))
    u4 = u2.reshape(u2.shape[0] // SSM_TILE, q, cpt, BRANCH_WIDTH)
    return pl.pallas_call(
        functools.partial(_s5_kernel, nch=nch, nbat=nbat),
        grid=(BRANCH_WIDTH // LANES, bsz // nbat),
        in_specs=[
            io_spec,
            tile_spec((tp, st, LANES)),
            tile_spec((tp, st, LANES)),
            tile_spec((tp, len(SCAN_STEPS) * N_COMP, st)),
            tile_spec((tp, N_COMP * SUBLANES, st)),
            tile_spec((1, LANES)),
        ],
        out_specs=io_spec,
        out_shape=jax.ShapeDtypeStruct(u4.shape, BF16),
        scratch_shapes=[pltpu.VMEM((tp, rows, PAIR_IN), BF16),
                        pltpu.VMEM((tp, rows, PAIR_IN), F32),
                        pltpu.VMEM((rows, N_COMP * st), F32),
                        pltpu.VMEM((tp, PAIR_IN, PAIR_IN), BF16),
                        pltpu.VMEM((tp, PAIR_IN, N_COMP * st), BF16),
                        pltpu.VMEM((tp, N_COMP * st, PAIR_IN), BF16)],
        compiler_params=pltpu.CompilerParams(
            dimension_semantics=("parallel", "arbitrary"), vmem_limit_bytes=VMEM_LIMIT),
        name="s5_chunked",
    )(u4, a1, a2, lamd, lamp, dskip).reshape(u2.shape)


def _s5_params(a_re, a_im, log_step, b_re, b_im, c_re, c_im, d_skip):
    q = SSM_CHUNK
    g, n, c = SSM_GROUPS, SSM_STATE, SSM_GROUP
    no, og = SSM_PAIRS, 2
    step = jnp.exp(log_step)[..., None]
    dr = a_re * step
    di = a_im * step
    mag = jnp.exp(dr)
    abar_r = mag * jnp.cos(di)
    abar_i = mag * jnp.sin(di)
    den = a_re * a_re + a_im * a_im
    nr = abar_r - 1.0
    sr = (nr * a_re + abar_i * a_im) / den
    si = (abar_i * a_re - nr * a_im) / den
    bb_r = sr[..., None] * b_re - si[..., None] * b_im
    bb_i = sr[..., None] * b_im + si[..., None] * b_re

    def powers(kk):
        kk = jnp.asarray(kk, F32)
        pmag = jnp.exp(dr[..., None] * kk)
        return pmag * jnp.cos(di[..., None] * kk), pmag * jnp.sin(di[..., None] * kk)

    pr, pi = powers(jnp.arange(LANES // 4))
    a1 = jnp.concatenate([pr[0], pi[0], pr[1], pi[1]], axis=-1).reshape(no, PAIR_STATE, LANES)
    by_state = lambda w_: w_.reshape(no, PAIR_STATE, c)
    a2 = jnp.concatenate(
        [by_state(c_re.transpose(0, 2, 1)), by_state(c_im.transpose(0, 2, 1)),
         by_state(bb_r[0]), by_state(bb_i[0]), by_state(bb_r[1]), by_state(bb_i[1]),
         jnp.zeros((no, PAIR_STATE, LANES - 6 * c), F32)], axis=-1)

    def scan_rows(kk):
        sr_, si_ = powers(q * jnp.asarray(kk))
        rows = jnp.stack([sr_[0], si_[0], sr_[1], si_[1]], axis=0)
        return rows.reshape(N_COMP, no, og, n, len(kk))

    lamd = scan_rows(SCAN_STEPS).transpose(1, 4, 0, 2, 3).reshape(
        no, len(SCAN_STEPS) * N_COMP, PAIR_STATE)
    fwd = scan_rows(list(range(1, SUBLANES + 1)))[:2]
    bwd = scan_rows(list(range(SUBLANES, 0, -1)))[2:]
    lamp = jnp.concatenate([fwd, bwd], axis=0).transpose(1, 0, 4, 2, 3).reshape(
        no, N_COMP * SUBLANES, PAIR_STATE)
    dskip = d_skip.reshape(BRANCH_WIDTH // LANES, 1, LANES)
    by_tile = lambda w_: w_.reshape((BRANCH_WIDTH // LANES, TILE_PAIRS) + w_.shape[1:])
    return by_tile(a1), by_tile(a2), by_tile(lamd), by_tile(lamp), dskip


def _merge_kernel(sink_ref, x_ref, g_ref, bias_ref, q_ref, kvp_ref, kvc_ref, kvn_ref, hs_ref, gr_ref,
                  ys_ref, unperm_ref, wgate_ref, wb_ref, wglu_ref, bglu_ref, wout_ref, o_ref,
                  *, nb, layer):
    x = x_ref[...]
    hn = _rms(x, g_ref[...]).astype(BF16)
    bw = BRANCH_WIDTH
    blocks = x.shape[0] // BLOCK
    kv = jnp.concatenate([kvp_ref[...], kvc_ref[...], kvn_ref[...]], axis=0).astype(F32)
    first = pl.program_id(0) * blocks
    o_attn = jnp.concatenate(
        [jnp.concatenate(_attn_block(sink_ref, bias_ref, q_ref[j * BLOCK:(j + 1) * BLOCK, :],
                                     kv[j * BLOCK:(j + 3) * BLOCK], (first + j) % nb, nb, layer),
                         axis=1) for j in range(blocks)], axis=0)
    o_lru = (jax.nn.gelu(gr_ref[...].astype(F32)) * hs_ref[...]).astype(BF16)
    zg = _dot(jax.nn.gelu(ys_ref[...].astype(F32)).astype(BF16), wglu_ref[...]) + bglu_ref[...]
    o_ssm = (zg[:, :bw] * jnp.tanh(zg[:, bw:]) + zg[:, :bw]).astype(BF16)
    o_ssm = _dot(unperm_ref[...], o_ssm).astype(BF16)
    mixed = None
    for kb, ob in enumerate((o_attn, o_lru, o_ssm)):
        tz = jnp.tanh(_dot(hn, wgate_ref[:, kb * D_MODEL:(kb + 1) * D_MODEL]))
        proj = _dot(ob, wb_ref[kb])
        term = proj * tz + proj
        mixed = term if mixed is None else mixed + term
    o_ref[...] = x + _dot(mixed.astype(BF16), wout_ref[...])


def _merge(layer, x2, g, sink, qkv, seq, h_sum, gr, y_ssm, unperm, w_gate, w_branch, w_glu, b_glu,
           w_out):
    t = x2.shape[0]
    tm = SSM_TILE
    assert seq % BLOCK == 0 and tm % BLOCK == 0
    blocks = tm // BLOCK
    kv_col = BRANCH_WIDTH // (2 * KV_WIDTH)
    row = lambda w: pl.BlockSpec((tm, w), lambda i: (i, 0))
    halo = lambda f: pl.BlockSpec((BLOCK, 2 * KV_WIDTH), lambda i: (f(i), kv_col))
    return pl.pallas_call(
        functools.partial(_merge_kernel, nb=seq // BLOCK, layer=layer),
        grid=(t // tm,),
        in_specs=[pl.BlockSpec(memory_space=pltpu.SMEM),
                  row(D_MODEL), _layer_spec(layer, (1, D_MODEL)),
                  _const_spec((N_KV_HEADS, HEADS_PER_KV * BLOCK, 3 * BLOCK)),
                  row(BRANCH_WIDTH),
                  halo(lambda i: jnp.maximum(i * blocks - 1, 0)),
                  pl.BlockSpec((tm, 2 * KV_WIDTH), lambda i: (i, kv_col)),
                  halo(lambda i: jnp.minimum((i + 1) * blocks, t // BLOCK - 1)),
                  row(BRANCH_WIDTH), row(BRANCH_WIDTH), row(BRANCH_WIDTH), _const_spec((tm, tm)),
                  _layer_spec(layer, (D_MODEL, N_BRANCH * D_MODEL)),
                  _layer_spec(layer, (N_BRANCH, BRANCH_WIDTH, D_MODEL)),
                  _layer_spec(layer, (BRANCH_WIDTH, 2 * BRANCH_WIDTH)),
                  _layer_spec(layer, (1, 2 * BRANCH_WIDTH)),
                  _layer_spec(layer, (D_MODEL, D_MODEL))],
        out_specs=row(D_MODEL),
        out_shape=jax.ShapeDtypeStruct((t, D_MODEL), F32),
        compiler_params=pltpu.CompilerParams(
            dimension_semantics=("parallel",), vmem_limit_bytes=VMEM_LIMIT),
        name="merge",
    )(sink, x2, g, jnp.asarray(_attn_bias()), qkv, qkv, qkv, qkv, h_sum, gr, y_ssm, unperm,
      w_gate, w_branch, w_glu, b_glu, w_out)


def _route(logits):
    lane = lax.broadcasted_iota(jnp.int32, logits.shape, 1).astype(F32)
    neg = -jnp.inf
    first = lambda mask: jnp.min(jnp.where(mask, lane, float(ROUTE_COLS)), axis=-1, keepdims=True)
    gmask = lane < N_GROUPS
    gmax = jnp.max(jnp.where(gmask, logits, neg), axis=-1, keepdims=True)
    gidx = first(gmask & (logits == gmax))
    gsum = jnp.sum(jnp.where(gmask, jnp.exp(logits - gmax), 0.0), axis=-1, keepdims=True)
    g_w = 1.0 / gsum
    e_lo = N_GROUPS + EXPERTS_PER_GROUP * gidx
    emask = (lane >= e_lo) & (lane < e_lo + EXPERTS_PER_GROUP)
    v1 = jnp.max(jnp.where(emask, logits, neg), axis=-1, keepdims=True)
    i1 = first(emask & (logits == v1))
    emask2 = emask & (lane != i1)
    v2 = jnp.max(jnp.where(emask2, logits, neg), axis=-1, keepdims=True)
    i2 = first(emask2 & (logits == v2))
    t = jnp.exp(v2 - v1)
    w1 = 1.0 / (1.0 + t)
    w2 = t * w1
    cw = jnp.where(lane == i1, w1 * g_w, jnp.where(lane == i2, w2 * g_w, 0.0))
    return cw, jnp.where(lane == gidx, 1.0, 0.0)


def _moe_kernel(x_ref, g_ref, wr_ref, br_ref, tril_ref, w13_ref, w2_ref, fg_ref, o_ref,
                xs_s, cws_s, ys_s, *, final_norm):
    x = x_ref[...]
    tm = x.shape[0]
    ns = xs_s.shape[0]
    hn = _rms(x, g_ref[...])
    hb = hn.astype(BF16)
    lo = (hn - hb.astype(F32)).astype(BF16)
    hb_w = _dot(hb, wr_ref[...])
    logits = (hb_w[:, :ROUTE_COLS] + hb_w[:, ROUTE_COLS:] + _dot(lo, wr_ref[:, :ROUTE_COLS])
              + br_ref[...])
    cw, ghot = _route(logits)

    lane = lax.broadcasted_iota(jnp.int32, (1, ROUTE_COLS), 1)
    rank = _dot(tril_ref[...], ghot.astype(BF16))
    total = jnp.sum(ghot, axis=0, keepdims=True)
    counts, offsets = [], []
    off = jnp.int32(0)
    for grp in range(N_GROUPS):
        cnt = jnp.sum(jnp.where(lane == grp, total, 0.0)).astype(jnp.int32)
        counts.append(cnt)
        offsets.append(off)
        off = off + ((cnt + (MOE_ALIGN - 1)) // MOE_ALIGN) * MOE_ALIGN
    off_vec = jnp.zeros((1, ROUTE_COLS), F32)
    for grp in range(1, N_GROUPS):
        off_vec = jnp.where(lane == grp, offsets[grp].astype(F32), off_vec)
    pos_terms = ghot * (rank + off_vec)
    pos_col = jnp.sum(pos_terms, axis=-1, keepdims=True)
    pos_row = _place_nt(jnp.ones((SUBLANES, ROUTE_COLS), BF16), pos_terms)[0:1]
    sort = (lax.broadcasted_iota(jnp.int32, (ns, tm), 0).astype(F32) == pos_row).astype(BF16)
    unsort = (lax.broadcasted_iota(jnp.int32, (tm, ns), 1).astype(F32) == pos_col).astype(BF16)

    xs_s[...] = _dot(sort, hb).astype(BF16)
    c1 = cw.astype(BF16)
    c2 = (cw - c1.astype(F32)).astype(BF16)
    cw_sorted = _dot(sort, jnp.concatenate([c1, c2], axis=1))
    cws_s[...] = cw_sorted[:, :ROUTE_COLS] + cw_sorted[:, ROUTE_COLS:]
    ys_s[...] = jnp.zeros_like(ys_s)

    for grp in range(N_GROUPS):
        def chunk(k, carry, grp=grp):
            r0 = pl.multiple_of(offsets[grp] + k * MOE_CHUNK, MOE_ALIGN)
            h = _dot(xs_s[pl.ds(r0, MOE_CHUNK), :], w13_ref[grp])
            cwb = cws_s[pl.ds(r0, MOE_CHUNK), :]
            cols = []
            for e in range(EXPERTS_PER_GROUP):
                col = N_GROUPS + grp * EXPERTS_PER_GROUP + e
                wcol = jnp.sum(jnp.where(lane == col, cwb, 0.0), axis=-1, keepdims=True)
                cols.append(jnp.broadcast_to(wcol, (MOE_CHUNK, D_EXPERT)))
            h1 = h[:, :GROUP_HIDDEN]
            hg = (h1 * _sigmoid(h1) * h[:, GROUP_HIDDEN:] * jnp.concatenate(cols, axis=1))
            ys_s[pl.ds(r0, MOE_CHUNK), :] = _dot(hg.astype(BF16), w2_ref[grp])
            return carry

        lax.fori_loop(0, (counts[grp] + (MOE_CHUNK - 1)) // MOE_CHUNK, chunk, 0)

    y = x + _dot(unsort, ys_s[...].astype(BF16))
    if final_norm:
        y = _rms(y, fg_ref[...])
    o_ref[...] = y


def _moe(layer, x2, g, w_route, b_route, w13, w2, final_g, tm, final_norm):
    t = x2.shape[0]
    ns = -(-(tm + N_GROUPS * (MOE_ALIGN - 1) + MOE_CHUNK) // LANES) * LANES
    row = pl.BlockSpec((tm, D_MODEL), lambda i: (i, 0))
    once = pl.Buffered(1)
    tril = jnp.asarray(np.tril(np.ones((tm, tm), np.float32), -1), BF16)
    return pl.pallas_call(
        functools.partial(_moe_kernel, final_norm=final_norm),
        grid=(t // tm,),
        in_specs=[row,
                  _layer_spec(layer, (1, D_MODEL)),
                  _layer_spec(layer, (D_MODEL, 2 * ROUTE_COLS)),
                  _layer_spec(layer, (1, ROUTE_COLS)),
                  _const_spec((tm, tm)),
                  _layer_spec(layer, (N_GROUPS, D_MODEL, 2 * GROUP_HIDDEN), pipeline_mode=once),
                  _layer_spec(layer, (N_GROUPS, GROUP_HIDDEN, D_MODEL), pipeline_mode=once),
                  _const_spec((1, D_MODEL))],
        out_specs=row,
        out_shape=jax.ShapeDtypeStruct((t, D_MODEL), F32),
        scratch_shapes=[pltpu.VMEM((ns, D_MODEL), BF16), pltpu.VMEM((ns, ROUTE_COLS), F32),
                        pltpu.VMEM((ns, D_MODEL), F32)],
        compiler_params=pltpu.CompilerParams(
            dimension_semantics=("parallel",), vmem_limit_bytes=VMEM_LIMIT),
        name="moe_final" if final_norm else "moe",
    )(x2, g, w_route, b_route, tril, w13, w2, final_g)


def _prepare(p):
    w_in = p['w_in']
    depth = w_in.shape[0]
    pad = ROUTE_COLS - N_GROUPS - N_EXPERTS
    w_route = jnp.concatenate(
        [p['moe_w_group'], p['moe_w_expert'], jnp.zeros((depth, D_MODEL, pad), F32)], axis=-1)
    wr_hi = w_route.astype(BF16)
    grp = lambda w: w.transpose(0, 1, 3, 2, 4).reshape(depth, N_GROUPS, D_MODEL, GROUP_HIDDEN)
    s5_a1, s5_a2, lamd, lamp, dskip = jax.vmap(_s5_params)(
        p['ssm_a_re'], p['ssm_a_im'], p['ssm_log_step'], p['ssm_b_re'], p['ssm_b_im'],
        p['ssm_c_re'], p['ssm_c_im'], p['ssm_d'])
    return dict(
        norm1_g=p['norm1_g'][:, None],
        w_main=w_in[:, :, :MAIN_COLS].astype(BF16),
        w_gate=(0.5 * w_in[:, :, MAIN_COLS:]).astype(BF16),
        attn_sink=p['attn_sink'],
        conv_w=p['lru_conv_w'],
        conv_b=p['lru_conv_b'][:, None],
        lru_wg=(0.5 * jnp.concatenate([_block_diag(p['lru_w_r']), _block_diag(p['lru_w_i'])],
                                      axis=-1)).astype(BF16),
        lru_bg=0.5 * jnp.concatenate([p['lru_b_r'], p['lru_b_i']], axis=-1)[:, :, None],
        lru_c=(-0.5 * LRU_C) * jax.nn.softplus(-p['lru_lambda'])[:, :, None],
        s5_a1=s5_a1, s5_a2=s5_a2, lamd=lamd, lamp=lamp, dskip=dskip,
        w_branch=(0.5 * p['w_branch']).astype(BF16),
        w_glu=(0.5 * p['ssm_w_glu']).astype(BF16),
        b_glu=0.5 * p['ssm_b_glu'][:, None],
        w_out=p['w_out'].astype(BF16),
        norm2_g=p['norm2_g'][:, None],
        w_route=jnp.concatenate([wr_hi, (w_route - wr_hi.astype(F32)).astype(BF16)], axis=-1),
        b_route=jnp.concatenate(
            [p['moe_b_group'], p['moe_b_expert'], jnp.zeros((depth, pad), F32)], axis=-1)[:, None],
        w13=jnp.concatenate([grp(p['moe_w1']), grp(p['moe_w3'])], axis=-1).astype(BF16),
        w2=p['moe_w2'].reshape(depth, N_GROUPS, GROUP_HIDDEN, D_MODEL).astype(BF16),
    )


def _layer(layer, x2, bsz, seq, w, final_g, final_norm):
    t = bsz * seq
    bw = BRANCH_WIDTH
    assert t % SSM_TILE == 0
    perm = _chunk_perm()
    qkv, xr, gr, u = _inproj(layer, x2, w['norm1_g'], w['w_main'], jnp.asarray(perm, BF16))
    h_sum = _lru(layer, xr.reshape(bsz, seq, bw), w['conv_w'], w['conv_b'], w['lru_wg'],
                 w['lru_bg'], w['lru_c']).reshape(t, bw)
    y_ssm = _s5(layer, u, w['s5_a1'], w['s5_a2'], w['lamd'], w['lamp'], w['dskip'],
                bsz, seq, _pick_tile(bsz, 8))
    x2 = _merge(layer, x2, w['norm1_g'], w['attn_sink'], qkv, seq, h_sum, gr, y_ssm,
                jnp.asarray(perm.T, BF16), w['w_gate'], w['w_branch'], w['w_glu'], w['b_glu'],
                w['w_out'])
    return _moe(layer, x2, w['norm2_g'], w['w_route'], w['b_route'], w['w13'], w['w2'],
                final_g, _pick_tile(t, 512), final_norm)


_PARAM_KEYS = ('norm1_g', 'w_in', 'attn_sink', 'lru_conv_w', 'lru_conv_b', 'lru_w_r', 'lru_b_r',
               'lru_w_i', 'lru_b_i', 'lru_lambda', 'ssm_a_re', 'ssm_a_im', 'ssm_log_step',
               'ssm_b_re', 'ssm_b_im', 'ssm_c_re', 'ssm_c_im', 'ssm_d', 'ssm_w_glu', 'ssm_b_glu',
               'w_branch', 'w_out', 'norm2_g', 'moe_w_group', 'moe_b_group', 'moe_w_expert',
               'moe_b_expert', 'moe_w1', 'moe_w3', 'moe_w2')


def kernel(x, norm1_g, w_in, attn_sink, lru_conv_w, lru_conv_b, lru_w_r, lru_b_r, lru_w_i, lru_b_i,
           lru_lambda, ssm_a_re, ssm_a_im, ssm_log_step, ssm_b_re, ssm_b_im, ssm_c_re, ssm_c_im,
           ssm_d, ssm_w_glu, ssm_b_glu, w_branch, w_out, norm2_g, moe_w_group, moe_b_group,
           moe_w_expert, moe_b_expert, moe_w1, moe_w3, moe_w2, final_norm_g):
    params = dict(zip(_PARAM_KEYS, (
        norm1_g, w_in, attn_sink, lru_conv_w, lru_conv_b, lru_w_r, lru_b_r, lru_w_i, lru_b_i,
        lru_lambda, ssm_a_re, ssm_a_im, ssm_log_step, ssm_b_re, ssm_b_im, ssm_c_re, ssm_c_im,
        ssm_d, ssm_w_glu, ssm_b_glu, w_branch, w_out, norm2_g, moe_w_group, moe_b_group,
        moe_w_expert, moe_b_expert, moe_w1, moe_w3, moe_w2)))
    bsz, seq, _ = x.shape
    depth = norm1_g.shape[0]
    w = _prepare(params)
    x2 = x.reshape(bsz * seq, D_MODEL)
    for layer in range(depth):
        x2 = _layer(layer, x2, bsz, seq, w, final_norm_g[None], final_norm=(layer == depth - 1))
    return x2.reshape(bsz, seq, D_MODEL)
```

```python
import functools

import jax
import jax.numpy as jnp
import numpy as np
from jax import lax
from jax.experimental import pallas as pl
from jax.experimental.pallas import tpu as pltpu

F32 = jnp.float32
BF16 = jnp.bfloat16

D_MODEL = 1024
N_Q_HEADS = 8
N_KV_HEADS = 2
HEAD_DIM = 64
WINDOW = 128
BLOCK = 128
BRANCH_WIDTH = 512
KV_WIDTH = N_KV_HEADS * HEAD_DIM
LRU_C = 8.0
SSM_GROUP = 16
SSM_GROUPS = BRANCH_WIDTH // SSM_GROUP
SSM_STATE = 64
N_BRANCH = 3
N_GROUPS = 4
EXPERTS_PER_GROUP = 4
N_EXPERTS = N_GROUPS * EXPERTS_PER_GROUP
D_EXPERT = 256
GROUP_HIDDEN = EXPERTS_PER_GROUP * D_EXPERT
NORM_EPS = 1e-6
MASK_VALUE = -1e30

LANES = 128
SUBLANES = 8
QKV_WIDTH = BRANCH_WIDTH + 2 * KV_WIDTH
MAIN_COLS = QKV_WIDTH + 3 * BRANCH_WIDTH
HEADS_PER_KV = N_Q_HEADS // N_KV_HEADS
SSM_CHUNK = 16
PAIR_CH = 2 * SSM_GROUP
SSM_PAIRS = SSM_GROUPS // 2
TILE_PAIRS = LANES // PAIR_CH
PAIR_IN = SSM_CHUNK * PAIR_CH
PAIR_STATE = 2 * SSM_STATE
SSM_TILE = 512
N_COMP = 4
SCAN_STEPS = (1, 2, 4)
ROUTE_COLS = LANES
MOE_ALIGN = 16
MOE_CHUNK = 192
CONV_HALO = 16
SQRT_CLAMP = 1e-30
VMEM_LIMIT = 56 * 1024 * 1024


def _rms(x, g):
    ms = jnp.mean(x * x, axis=-1, keepdims=True)
    return x * lax.rsqrt(ms + NORM_EPS) * g


def _sigmoid(x):
    return 0.5 * jnp.tanh(0.5 * x) + 0.5


def _dot(a, b):
    return jnp.dot(a, b, preferred_element_type=F32)


def _dot_nt(a, b):
    return lax.dot_general(a, b, (((1,), (1,)), ((), ())), preferred_element_type=F32)


def _const_spec(shape):
    nd = len(shape)
    return pl.BlockSpec(shape, lambda *_: (0,) * nd)


def _layer_spec(layer, shape, **kw):
    nd = len(shape)
    return pl.BlockSpec((None,) + tuple(shape), lambda *_: (layer,) + (0,) * nd, **kw)


def _row_in_tile(rows):
    return lax.broadcasted_iota(jnp.int32, (rows, 1), 0) & (SUBLANES - 1)


def _pick_tile(n, want):
    t = min(n, want)
    while n % t:
        t //= 2
    return t


def _chunk_perm():
    t = np.arange(SSM_TILE)
    dst = (t % SSM_CHUNK) * (SSM_TILE // SSM_CHUNK) + t // SSM_CHUNK
    perm = np.zeros((SSM_TILE, SSM_TILE), np.float32)
    perm[dst, t] = 1.0
    return perm


def _inproj_kernel(x_ref, g_ref, w_ref, perm_ref, qkv_ref, xr_ref, gr_ref, u_ref):
    hn = _rms(x_ref[...], g_ref[...]).astype(BF16)
    bw = BRANCH_WIDTH
    qkv_ref[...] = _dot(hn, w_ref[:, 0:QKV_WIDTH]).astype(BF16)
    xr_ref[...] = _dot(hn, w_ref[:, QKV_WIDTH:QKV_WIDTH + bw]).astype(BF16)
    gr_ref[...] = _dot(hn, w_ref[:, QKV_WIDTH + bw:QKV_WIDTH + 2 * bw]).astype(BF16)
    u = _dot(hn, w_ref[:, QKV_WIDTH + 2 * bw:MAIN_COLS]).astype(BF16)
    u_ref[...] = _dot(perm_ref[...], u).astype(BF16)


def _inproj(layer, x2, g, w_main, perm):
    t = x2.shape[0]
    tm = SSM_TILE
    row = lambda w: pl.BlockSpec((tm, w), lambda i: (i, 0))
    return pl.pallas_call(
        _inproj_kernel,
        grid=(t // tm,),
        in_specs=[row(D_MODEL), _layer_spec(layer, (1, D_MODEL)),
                  _layer_spec(layer, (D_MODEL, MAIN_COLS)), _const_spec((tm, tm))],
        out_specs=[row(QKV_WIDTH), row(BRANCH_WIDTH), row(BRANCH_WIDTH), row(BRANCH_WIDTH)],
        out_shape=[jax.ShapeDtypeStruct((t, QKV_WIDTH), BF16)]
        + [jax.ShapeDtypeStruct((t, BRANCH_WIDTH), BF16)] * 3,
        compiler_params=pltpu.CompilerParams(
            dimension_semantics=("parallel",), vmem_limit_bytes=VMEM_LIMIT),
        name="inproj",
    )(x2, g, w_main, perm)


def _attn_bias():
    qi = np.arange(BLOCK)[:, None]
    kj = np.arange(3 * BLOCK)[None, :]
    dist = np.abs(qi - kj + BLOCK)
    out = np.empty((N_KV_HEADS, HEADS_PER_KV * BLOCK, 3 * BLOCK), np.float32)
    for head in range(N_Q_HEADS):
        h, g = divmod(head, HEADS_PER_KV)
        slope = 2.0 ** (-8.0 * (head + 1) / N_Q_HEADS)
        out[h, g * BLOCK:(g + 1) * BLOCK] = np.where(dist <= WINDOW, -slope * dist, 1.0)
    return out


def _attn_block(sink_ref, bias_ref, q, kv, n, nb, layer):
    k = kv[:, :KV_WIDTH]
    v = kv[:, KV_WIDTH:]
    lane = lax.broadcasted_iota(jnp.int32, (1, KV_WIDTH), 1)
    lo = lane < HEAD_DIM

    rows = HEADS_PER_KV * BLOCK
    kj = lax.broadcasted_iota(jnp.int32, (1, 3 * BLOCK), 1)
    in_seq = ((kj >= BLOCK) | (n > 0)) & ((kj < 2 * BLOCK) | (n < nb - 1))
    valid = (bias_ref[0] <= 0.0) & in_seq
    head_in_kv = lax.broadcasted_iota(jnp.int32, (rows, 1), 0) // BLOCK

    scale = HEAD_DIM ** -0.5
    outs = [None] * (N_Q_HEADS // 2)
    for h in range(N_KV_HEADS):
        own = lo if h == 0 else jnp.logical_not(lo)
        km = jnp.where(own, k, 0.0)
        vm = jnp.where(own, v, 0.0)
        k2 = ((km + pltpu.roll(km, HEAD_DIM, 1)) * scale).astype(BF16)
        v2 = (vm + pltpu.roll(vm, HEAD_DIM, 1)).astype(BF16)
        parts = []
        sink = jnp.zeros((rows, 1), F32)
        for g in range(HEADS_PER_KV):
            head = h * HEADS_PER_KV + g
            pair = head // 2
            qp = q[:, pair * KV_WIDTH:(pair + 1) * KV_WIDTH]
            keep = lo if head % 2 == 0 else jnp.logical_not(lo)
            parts.append(jnp.where(keep, qp, jnp.zeros_like(qp)))
            sink = jnp.where(head_in_kv == g, sink_ref[layer, head], sink)
        s = _dot_nt(jnp.concatenate(parts, axis=0), k2)
        s = jnp.where(valid, s + bias_ref[h], MASK_VALUE)
        m = jnp.maximum(jnp.max(s, axis=-1, keepdims=True), sink)
        p = jnp.exp(s - m)
        denom = jnp.sum(p, axis=-1, keepdims=True) + jnp.exp(sink - m)
        r = _dot(p.astype(BF16), v2) / denom
        for j in range(HEADS_PER_KV // 2):
            even = r[(2 * j) * BLOCK:(2 * j + 1) * BLOCK]
            odd = r[(2 * j + 1) * BLOCK:(2 * j + 2) * BLOCK]
            outs[h * (HEADS_PER_KV // 2) + j] = jnp.where(lo, even, odd).astype(BF16)
    return outs


def _tile_scan(a, u, reverse):
    rows = a.shape[0]
    rit = _row_in_tile(rows)
    for d in SCAN_STEPS:
        if reverse:
            take = rit < SUBLANES - d
            shift = rows - d
        else:
            take = rit >= d
            shift = d
        a_sh = jnp.where(take, pltpu.roll(a, shift, 0), 1.0)
        u_sh = jnp.where(take, pltpu.roll(u, shift, 0), 0.0)
        u = u + a * u_sh
        a = a * a_sh
    return a, u


def _lru_kernel(x_ref, cw_ref, cb_ref, wg_ref, bg_ref, c_ref, o_ref, xe_s, xc_s, a_s, u_s, hf_s,
                *, seq, rc):
    phase = pl.program_id(1)
    bw = BRANCH_WIDTH
    n_tiles = seq // SUBLANES

    def conv():
        zeros = jnp.zeros((CONV_HALO, bw), F32)
        xe_s[0:CONV_HALO, :] = zeros
        xe_s[CONV_HALO + seq:CONV_HALO + seq + CONV_HALO, :] = zeros
        xe_s[CONV_HALO:CONV_HALO + seq, :] = x_ref[...].astype(F32)
        cw = cw_ref[...]
        for c in range(seq // rc):
            base = CONV_HALO + c * rc
            xc = cb_ref[...]
            for tap in range(4):
                xc = xc + cw[tap:tap + 1] * xe_s[base + tap - 2:base + tap - 2 + rc, :]
            xc_s[c * rc:(c + 1) * rc, :] = xc

    def build(reverse):
        for c in range(seq // rc):
            xc = xc_s[c * rc:(c + 1) * rc, :]
            tz = jnp.tanh(_dot(xc.astype(BF16), wg_ref[...]) + bg_ref[...])
            log_a = c_ref[...] * tz[:, :bw] + c_ref[...]
            a = jnp.exp(log_a)
            t = 1.0 - a * a
            xh = 0.5 * xc
            u = t * lax.rsqrt(jnp.maximum(t, SQRT_CLAMP)) * (tz[:, bw:] * xh + xh)
            a, u = _tile_scan(a, u, reverse)
            a_s[c * rc:(c + 1) * rc, :] = a
            u_s[c * rc:(c + 1) * rc, :] = u

    @pl.when(phase == 0)
    def _():
        conv()
        build(False)

        def step(k, carry):
            r0 = pl.multiple_of(k * SUBLANES, SUBLANES)
            h = u_s[pl.ds(r0, SUBLANES), :] + a_s[pl.ds(r0, SUBLANES), :] * carry
            hf_s[pl.ds(r0, SUBLANES), :] = h
            return jnp.broadcast_to(h[SUBLANES - 1:SUBLANES, :], (SUBLANES, bw))

        lax.fori_loop(0, n_tiles, step, jnp.zeros((SUBLANES, bw), F32), unroll=4)

    @pl.when(phase == 1)
    def _():
        build(True)

        def step(k, carry):
            r0 = pl.multiple_of((n_tiles - 1 - k) * SUBLANES, SUBLANES)
            h = u_s[pl.ds(r0, SUBLANES), :] + a_s[pl.ds(r0, SUBLANES), :] * carry
            o_ref[pl.ds(r0, SUBLANES), :] = h + hf_s[pl.ds(r0, SUBLANES), :]
            return jnp.broadcast_to(h[0:1, :], (SUBLANES, bw))

        lax.fori_loop(0, n_tiles, step, jnp.zeros((SUBLANES, bw), F32), unroll=4)


def _lru(layer, xr3, conv_w, conv_b, wg, bg, sp):
    b, l, w = xr3.shape
    seq_spec = pl.BlockSpec((None, l, w), lambda bi, ph: (bi, 0, 0))
    per_dir = lambda shape: pl.BlockSpec(
        (None, None) + shape, lambda bi, ph: (layer, ph) + (0,) * len(shape))
    return pl.pallas_call(
        functools.partial(_lru_kernel, seq=l, rc=_pick_tile(l, 512)),
        grid=(b, 2),
        in_specs=[seq_spec, _layer_spec(layer, (4, w)), _layer_spec(layer, (1, w)),
                  per_dir((w, 2 * w)), per_dir((1, 2 * w)), per_dir((1, w))],
        out_specs=seq_spec,
        out_shape=jax.ShapeDtypeStruct((b, l, w), F32),
        scratch_shapes=[pltpu.VMEM((l + 2 * CONV_HALO, w), F32)] + [pltpu.VMEM((l, w), F32)] * 4,
        compiler_params=pltpu.CompilerParams(
            dimension_semantics=("parallel", "arbitrary"), vmem_limit_bytes=VMEM_LIMIT),
        name="rg_lru",
    )(xr3, conv_w, conv_b, wg, bg, sp)


def _block_diag(wb):
    nbk, bd = wb.shape[-3], wb.shape[-2]
    eye = jnp.eye(nbk, dtype=wb.dtype)
    out = jnp.einsum('...hij,hk->...hikj', wb, eye)
    return out.reshape(wb.shape[:-3] + (nbk * bd, nbk * bd))


def _split3(a):
    a1 = a.astype(BF16)
    r1 = a - a1.astype(F32)
    a2 = r1.astype(BF16)
    a3 = (r1 - a2.astype(F32)).astype(BF16)
    return a1, a2, a3


def _place(a, sel):
    p1, p2, p3 = _split3(a)
    return _dot(p1, sel) + _dot(p2, sel) + _dot(p3, sel)


def _place_nt(sel, a):
    p1, p2, p3 = _split3(a)
    return _dot_nt(sel, p1) + _dot_nt(sel, p2) + _dot_nt(sel, p3)


def _sel_cols(n_cols, base, idx_of_col):
    lane = lax.broadcasted_iota(jnp.int32, (LANES, n_cols), 0)
    col = lax.broadcasted_iota(jnp.int32, (LANES, n_cols), 1)
    return (lane == base + idx_of_col(col)).astype(BF16)


def _sel_rows(n_rows, base, idx_of_row):
    row = lax.broadcasted_iota(jnp.int32, (n_rows, LANES), 0)
    lane = lax.broadcasted_iota(jnp.int32, (n_rows, LANES), 1)
    return (lane == base + idx_of_row(row)).astype(BF16)


def _s5_build(a1_ref, a2_ref, tk_s, wsin_s, mout_s):
    q = SSM_CHUNK
    st = PAIR_STATE
    tp = TILE_PAIRS
    pw = LANES // 4
    cw = SSM_GROUP
    hp = lax.Precision.HIGHEST
    a1 = a1_ref[...].reshape(tp * st, LANES)
    a2 = a2_ref[...].reshape(tp * st, LANES)
    blk = lambda i: i >> 5
    grp = lambda i: (i >> 4) & 1
    ch = lambda i: i & (cw - 1)

    srow = lax.broadcasted_iota(jnp.int32, (tp * st, 1), 0)
    icol = lax.broadcasted_iota(jnp.int32, (1, PAIR_IN), 1)
    irow = lax.broadcasted_iota(jnp.int32, (PAIR_IN, 1), 0)
    scol = lax.broadcasted_iota(jnp.int32, (1, tp * st), 1)
    same_sc = ((srow >> 6) & 1) == grp(icol)
    same_is = grp(irow) == ((scol >> 6) & 1)
    kg_row = lax.broadcasted_iota(jnp.int32, (PAIR_CH, 1), 0)
    same_ks = grp(kg_row) == ((scol >> 6) & 1)
    erow = (lax.broadcasted_iota(jnp.int32, (PAIR_IN, PAIR_CH), 0) & (PAIR_CH - 1)
            == lax.broadcasted_iota(jnp.int32, (PAIR_IN, PAIR_CH), 1)).astype(BF16)

    c_re = _place(a2, _sel_cols(PAIR_IN, 0, ch))
    c_im = _place(a2, _sel_cols(PAIR_IN, cw, ch))

    def c_times(p_re, p_im):
        return (jnp.where(same_sc, c_re * p_re - c_im * p_im, 0.0),
                jnp.where(same_sc, -(c_re * p_im + c_im * p_re), 0.0))

    tk_fwd = None
    for d in range(2):
        pbase = 2 * d * pw
        bbase = 2 * cw + 2 * d * cw
        power = lambda f: (_place(a1, _sel_cols(PAIR_IN, pbase, f)),
                           _place(a1, _sel_cols(PAIR_IN, pbase + pw, f)))

        m_re, m_im = c_times(*power((lambda i: blk(i) + 1) if d == 0 else (lambda i: q - blk(i))))
        for pp in range(tp):
            mout_s[pp, (2 * d) * st:(2 * d + 1) * st, :] = m_re[pp * st:(pp + 1) * st].astype(BF16)
            mout_s[pp, (2 * d + 1) * st:(2 * d + 2) * st, :] = m_im[pp * st:(pp + 1) * st].astype(BF16)

        spow = (lambda i: q - 1 - blk(i)) if d == 0 else blk
        pt_re = _place_nt(_sel_rows(PAIR_IN, pbase, spow), a1)
        pt_im = _place_nt(_sel_rows(PAIR_IN, pbase + pw, spow), a1)
        bt_re = _place_nt(_sel_rows(PAIR_IN, bbase, ch), a2)
        bt_im = _place_nt(_sel_rows(PAIR_IN, bbase + cw, ch), a2)
        w_re = jnp.where(same_is, pt_re * bt_re - pt_im * bt_im, 0.0)
        w_im = jnp.where(same_is, pt_re * bt_im + pt_im * bt_re, 0.0)
        for pp in range(tp):
            wsin_s[pp, :, (2 * d) * st:(2 * d + 1) * st] = w_re[:, pp * st:(pp + 1) * st].astype(BF16)
            wsin_s[pp, :, (2 * d + 1) * st:(2 * d + 2) * st] = w_im[:, pp * st:(pp + 1) * st].astype(BF16)

        k_re, k_im = c_times(*power(blk if d == 0 else (lambda i: q - 1 - blk(i))))
        b_re = jnp.where(same_ks, _place_nt(_sel_rows(PAIR_CH, bbase, ch), a2), 0.0)
        b_im = jnp.where(same_ks, _place_nt(_sel_rows(PAIR_CH, bbase + cw, ch), a2), 0.0)
        tk_bwd = []
        for pp in range(tp):
            sl = slice(pp * st, (pp + 1) * st)
            lagk = (jnp.dot(b_re[:, sl], k_re[sl], preferred_element_type=F32, precision=hp)
                    + jnp.dot(b_im[:, sl], k_im[sl], preferred_element_type=F32, precision=hp))
            l1, l2, l3 = _split3(lagk)
            rep = _dot(erow, l1) + _dot(erow, l2) + _dot(erow, l3)
            blocks = []
            for s in range(q):
                piece = rep[s * PAIR_CH:(s + 1) * PAIR_CH]
                if d == 0:
                    shifted = piece if s == 0 else pltpu.roll(piece, s * PAIR_CH, 1)
                    blocks.append(jnp.where(blk(icol) >= s, shifted, 0.0))
                else:
                    shift = ((s + 1) * PAIR_CH) % PAIR_IN
                    shifted = piece if shift == 0 else pltpu.roll(piece, shift, 1)
                    blocks.append(jnp.where(blk(icol) <= s, shifted, 0.0))
            tk_bwd.append(blocks)
        if d == 0:
            tk_fwd = tk_bwd
        else:
            for pp in range(tp):
                for s in range(q):
                    tk_s[pp, s * PAIR_CH:(s + 1) * PAIR_CH, :] = (
                        tk_fwd[pp][s] + tk_bwd[pp][s]).astype(BF16)


def _lane_piece_mask(piece):
    lane = lax.broadcasted_iota(jnp.int32, (1, LANES), 1)
    return (lane // PAIR_CH) == piece


def _s5_kernel(u_ref, a1_ref, a2_ref, lamd_ref, lamp_ref, d_ref, y_ref,
               x_s, y_s, g_s, tk_s, wsin_s, mout_s, *, nch, nbat):
    q = SSM_CHUNK
    rows = nbat * nch
    spt = LANES // PAIR_CH

    @pl.when(pl.program_id(1) == 0)
    def _():
        _s5_build(a1_ref, a2_ref, tk_s, wsin_s, mout_s)

    step_rows = lambda s: u_ref[:, s].astype(F32).reshape(rows, LANES)
    masks = [_lane_piece_mask(j) for j in range(spt)]
    for kt in range(q // spt):
        rolled = []
        for j in range(spt):
            xs = step_rows(kt * spt + j)
            rolled.append([xs if sh == 0 else pltpu.roll(xs, sh * PAIR_CH, 1) for sh in range(spt)])
        for pp in range(TILE_PAIRS):
            tile = None
            for j in range(spt):
                src = rolled[j][(j - pp) % spt]
                tile = src if tile is None else jnp.where(masks[j], src, tile)
            x_s[pp, :, kt * LANES:(kt + 1) * LANES] = tile.astype(BF16)

    for pp in range(TILE_PAIRS):
        _s5_pair(x_s.at[pp], y_s.at[pp], g_s, tk_s.at[pp], wsin_s.at[pp], mout_s.at[pp],
                 lamd_ref.at[pp], lamp_ref.at[pp], nch=nch, nbat=nbat)

    for kt in range(q // spt):
        rolled = []
        for pp in range(TILE_PAIRS):
            yt = y_s[pp, :, kt * LANES:(kt + 1) * LANES]
            rolled.append([yt if sh == 0 else pltpu.roll(yt, sh * PAIR_CH, 1) for sh in range(spt)])
        for j in range(spt):
            tile = None
            for pp in range(TILE_PAIRS):
                src = rolled[pp][(pp - j) % spt]
                tile = src if tile is None else jnp.where(masks[pp], src, tile)
            s = kt * spt + j
            out = tile + d_ref[...] * step_rows(s)
            y_ref[:, s] = out.reshape(y_ref.shape[0], y_ref.shape[2], LANES).astype(y_ref.dtype)


def _s5_pair(x_ref, y_ref, g_s, tk_ref, wsin_ref, mout_ref, lamd_ref, lamp_ref, *, nch, nbat):
    rows = nbat * nch
    st = PAIR_STATE
    n_tiles = nch // SUBLANES
    x = x_ref[...]

    sloc = _dot(x, wsin_ref[...])
    chunk = lax.broadcasted_iota(jnp.int32, (rows, 1), 0) & (nch - 1)
    rit = _row_in_tile(rows)
    comps = []
    for comp in range(N_COMP):
        reverse = comp >= 2
        piece = sloc[:, comp * st:(comp + 1) * st]
        if reverse:
            piece = jnp.where(chunk == nch - 1, 0.0, pltpu.roll(piece, rows - 1, 0))
        else:
            piece = jnp.where(chunk == 0, 0.0, pltpu.roll(piece, 1, 0))
        comps.append(piece)

    for di, d in enumerate(SCAN_STEPS):
        for base in (0, 2):
            reverse = base == 2
            lr = lamd_ref[di * N_COMP + base:di * N_COMP + base + 1, :]
            li = lamd_ref[di * N_COMP + base + 1:di * N_COMP + base + 2, :]
            if reverse:
                take = rit < SUBLANES - d
                shift = rows - d
            else:
                take = rit >= d
                shift = d
            gr, gi = comps[base], comps[base + 1]
            sr = jnp.where(take, pltpu.roll(gr, shift, 0), 0.0)
            si = jnp.where(take, pltpu.roll(gi, shift, 0), 0.0)
            comps[base] = gr + lr * sr - li * si
            comps[base + 1] = gi + lr * si + li * sr
    for comp in range(N_COMP):
        g_s[:, comp * st:(comp + 1) * st] = comps[comp]

    pf_r, pf_i = lamp_ref[0:SUBLANES, :], lamp_ref[SUBLANES:2 * SUBLANES, :]
    pb_r, pb_i = lamp_ref[2 * SUBLANES:3 * SUBLANES, :], lamp_ref[3 * SUBLANES:4 * SUBLANES, :]

    def step(k, carry):
        out = []
        for bi in range(nbat):
            cfr, cfi, cbr, cbi = carry[4 * bi:4 * bi + 4]
            rf = pl.multiple_of(bi * nch + k * SUBLANES, SUBLANES)
            rb = pl.multiple_of(bi * nch + (n_tiles - 1 - k) * SUBLANES, SUBLANES)
            gfr = g_s[pl.ds(rf, SUBLANES), 0:st] + pf_r * cfr - pf_i * cfi
            gfi = g_s[pl.ds(rf, SUBLANES), st:2 * st] + pf_r * cfi + pf_i * cfr
            gbr = g_s[pl.ds(rb, SUBLANES), 2 * st:3 * st] + pb_r * cbr - pb_i * cbi
            gbi = g_s[pl.ds(rb, SUBLANES), 3 * st:4 * st] + pb_r * cbi + pb_i * cbr
            g_s[pl.ds(rf, SUBLANES), 0:st] = gfr
            g_s[pl.ds(rf, SUBLANES), st:2 * st] = gfi
            g_s[pl.ds(rb, SUBLANES), 2 * st:3 * st] = gbr
            g_s[pl.ds(rb, SUBLANES), 3 * st:4 * st] = gbi
            last = lambda v: jnp.broadcast_to(v[SUBLANES - 1:SUBLANES, :], (SUBLANES, st))
            first = lambda v: jnp.broadcast_to(v[0:1, :], (SUBLANES, st))
            out += [last(gfr), last(gfi), first(gbr), first(gbi)]
        return tuple(out)

    z = jnp.zeros((SUBLANES, st), F32)
    lax.fori_loop(0, n_tiles, step, (z,) * (4 * nbat))

    y_ref[...] = _dot(x, tk_ref[...]) + _dot(g_s[...].astype(BF16), mout_ref[...])


def _s5(layer, u2, a1, a2, lamd, lamp, dskip, bsz, seq, nbat):
    q = SSM_CHUNK
    nch = seq // q
    assert nch & (nch - 1) == 0 and nch % SUBLANES == 0, "chunks per sequence must be a power of two"
    rows = nbat * nch
    assert (nbat * seq) % SSM_TILE == 0
    nt = nbat * seq // SSM_TILE
    cpt = SSM_TILE // q
    st = PAIR_STATE
    tp = TILE_PAIRS
    tile_spec = lambda shape: pl.BlockSpec(
        (None, None) + shape, lambda o, i: (layer, o) + (0,) * len(shape))
    io_spec = pl.BlockSpec((nt, q, cpt, LANES), lambda o, i: (i, 0, 0, o))
    u4 = u2.reshape(u2.shape[0] // SSM_TILE, q, cpt, BRANCH_WIDTH)
    return pl.pallas_call(
        functools.partial(_s5_kernel, nch=nch, nbat=nbat),
        grid=(BRANCH_WIDTH // LANES, bsz // nbat),
        in_specs=[
            io_spec,
            tile_spec((tp, st, LANES)),
            tile_spec((tp, st, LANES)),
            tile_spec((tp, len(SCAN_STEPS) * N_COMP, st)),
            tile_spec((tp, N_COMP * SUBLANES, st)),
            tile_spec((1, LANES)),
        ],
        out_specs=io_spec,
        out_shape=jax.ShapeDtypeStruct(u4.shape, BF16),
        scratch_shapes=[pltpu.VMEM((tp, rows, PAIR_IN), BF16),
                        pltpu.VMEM((tp, rows, PAIR_IN), F32),
                        pltpu.VMEM((rows, N_COMP * st), F32),
                        pltpu.VMEM((tp, PAIR_IN, PAIR_IN), BF16),
                        pltpu.VMEM((tp, PAIR_IN, N_COMP * st), BF16),
                        pltpu.VMEM((tp, N_COMP * st, PAIR_IN), BF16)],
        compiler_params=pltpu.CompilerParams(
            dimension_semantics=("parallel", "arbitrary"), vmem_limit_bytes=VMEM_LIMIT),
        name="s5_chunked",
    )(u4, a1, a2, lamd, lamp, dskip).reshape(u2.shape)


def _s5_params(a_re, a_im, log_step, b_re, b_im, c_re, c_im, d_skip):
    q = SSM_CHUNK
    g, n, c = SSM_GROUPS, SSM_STATE, SSM_GROUP
    no, og = SSM_PAIRS, 2
    step = jnp.exp(log_step)[..., None]
    dr = a_re * step
    di = a_im * step
    mag = jnp.exp(dr)
    abar_r = mag * jnp.cos(di)
    abar_i = mag * jnp.sin(di)
    den = a_re * a_re + a_im * a_im
    nr = abar_r - 1.0
    sr = (nr * a_re + abar_i * a_im) / den
    si = (abar_i * a_re - nr * a_im) / den
    bb_r = sr[..., None] * b_re - si[..., None] * b_im
    bb_i = sr[..., None] * b_im + si[..., None] * b_re

    def powers(kk):
        kk = jnp.asarray(kk, F32)
        pmag = jnp.exp(dr[..., None] * kk)
        return pmag * jnp.cos(di[..., None] * kk), pmag * jnp.sin(di[..., None] * kk)

    pr, pi = powers(jnp.arange(LANES // 4))
    a1 = jnp.concatenate([pr[0], pi[0], pr[1], pi[1]], axis=-1).reshape(no, PAIR_STATE, LANES)
    by_state = lambda w_: w_.reshape(no, PAIR_STATE, c)
    a2 = jnp.concatenate(
        [by_state(c_re.transpose(0, 2, 1)), by_state(c_im.transpose(0, 2, 1)),
         by_state(bb_r[0]), by_state(bb_i[0]), by_state(bb_r[1]), by_state(bb_i[1]),
         jnp.zeros((no, PAIR_STATE, LANES - 6 * c), F32)], axis=-1)

    def scan_rows(kk):
        sr_, si_ = powers(q * jnp.asarray(kk))
        rows = jnp.stack([sr_[0], si_[0], sr_[1], si_[1]], axis=0)
        return rows.reshape(N_COMP, no, og, n, len(kk))

    lamd = scan_rows(SCAN_STEPS).transpose(1, 4, 0, 2, 3).reshape(
        no, len(SCAN_STEPS) * N_COMP, PAIR_STATE)
    fwd = scan_rows(list(range(1, SUBLANES + 1)))[:2]
    bwd = scan_rows(list(range(SUBLANES, 0, -1)))[2:]
    lamp = jnp.concatenate([fwd, bwd], axis=0).transpose(1, 0, 4, 2, 3).reshape(
        no, N_COMP * SUBLANES, PAIR_STATE)
    dskip = d_skip.reshape(BRANCH_WIDTH // LANES, 1, LANES)
    by_tile = lambda w_: w_.reshape((BRANCH_WIDTH // LANES, TILE_PAIRS) + w_.shape[1:])
    return by_tile(a1), by_tile(a2), by_tile(lamd), by_tile(lamp), dskip


def _merge_kernel(sink_ref, x_ref, g_ref, bias_ref, q_ref, kvp_ref, kvc_ref, kvn_ref, hs_ref, gr_ref,
                  ys_ref, unperm_ref, wgate_ref, wb_ref, wglu_ref, bglu_ref, wout_ref, o_ref,
                  *, nb, layer):
    x = x_ref[...]
    hn = _rms(x, g_ref[...]).astype(BF16)
    bw = BRANCH_WIDTH
    blocks = x.shape[0] // BLOCK
    kv = jnp.concatenate([kvp_ref[...], kvc_ref[...], kvn_ref[...]], axis=0).astype(F32)
    first = pl.program_id(0) * blocks
    o_attn = jnp.concatenate(
        [jnp.concatenate(_attn_block(sink_ref, bias_ref, q_ref[j * BLOCK:(j + 1) * BLOCK, :],
                                     kv[j * BLOCK:(j + 3) * BLOCK], (first + j) % nb, nb, layer),
                         axis=1) for j in range(blocks)], axis=0)
    o_lru = (jax.nn.gelu(gr_ref[...].astype(F32)) * hs_ref[...]).astype(BF16)
    zg = _dot(jax.nn.gelu(ys_ref[...].astype(F32)).astype(BF16), wglu_ref[...]) + bglu_ref[...]
    o_ssm = (zg[:, :bw] * jnp.tanh(zg[:, bw:]) + zg[:, :bw]).astype(BF16)
    o_ssm = _dot(unperm_ref[...], o_ssm).astype(BF16)
    mixed = None
    for kb, ob in enumerate((o_attn, o_lru, o_ssm)):
        tz = jnp.tanh(_dot(hn, wgate_ref[:, kb * D_MODEL:(kb + 1) * D_MODEL]))
        proj = _dot(ob, wb_ref[kb])
        term = proj * tz + proj
        mixed = term if mixed is None else mixed + term
    o_ref[...] = x + _dot(mixed.astype(BF16), wout_ref[...])


def _merge(layer, x2, g, sink, qkv, seq, h_sum, gr, y_ssm, unperm, w_gate, w_branch, w_glu, b_glu,
           w_out):
    t = x2.shape[0]
    tm = SSM_TILE
    assert seq % BLOCK == 0 and tm % BLOCK == 0
    blocks = tm // BLOCK
    kv_col = BRANCH_WIDTH // (2 * KV_WIDTH)
    row = lambda w: pl.BlockSpec((tm, w), lambda i: (i, 0))
    halo = lambda f: pl.BlockSpec((BLOCK, 2 * KV_WIDTH), lambda i: (f(i), kv_col))
    return pl.pallas_call(
        functools.partial(_merge_kernel, nb=seq // BLOCK, layer=layer),
        grid=(t // tm,),
        in_specs=[pl.BlockSpec(memory_space=pltpu.SMEM),
                  row(D_MODEL), _layer_spec(layer, (1, D_MODEL)),
                  _const_spec((N_KV_HEADS, HEADS_PER_KV * BLOCK, 3 * BLOCK)),
                  row(BRANCH_WIDTH),
                  halo(lambda i: jnp.maximum(i * blocks - 1, 0)),
                  pl.BlockSpec((tm, 2 * KV_WIDTH), lambda i: (i, kv_col)),
                  halo(lambda i: jnp.minimum((i + 1) * blocks, t // BLOCK - 1)),
                  row(BRANCH_WIDTH), row(BRANCH_WIDTH), row(BRANCH_WIDTH), _const_spec((tm, tm)),
                  _layer_spec(layer, (D_MODEL, N_BRANCH * D_MODEL)),
                  _layer_spec(layer, (N_BRANCH, BRANCH_WIDTH, D_MODEL)),
                  _layer_spec(layer, (BRANCH_WIDTH, 2 * BRANCH_WIDTH)),
                  _layer_spec(layer, (1, 2 * BRANCH_WIDTH)),
                  _layer_spec(layer, (D_MODEL, D_MODEL))],
        out_specs=row(D_MODEL),
        out_shape=jax.ShapeDtypeStruct((t, D_MODEL), F32),
        compiler_params=pltpu.CompilerParams(
            dimension_semantics=("parallel",), vmem_limit_bytes=VMEM_LIMIT),
        name="merge",
    )(sink, x2, g, jnp.asarray(_attn_bias()), qkv, qkv, qkv, qkv, h_sum, gr, y_ssm, unperm,
      w_gate, w_branch, w_glu, b_glu, w_out)


def _route(logits):
    lane = lax.broadcasted_iota(jnp.int32, logits.shape, 1).astype(F32)
    neg = -jnp.inf
    first = lambda mask: jnp.min(jnp.where(mask, lane, float(ROUTE_COLS)), axis=-1, keepdims=True)
    gmask = lane < N_GROUPS
    gmax = jnp.max(jnp.where(gmask, logits, neg), axis=-1, keepdims=True)
    gidx = first(gmask & (logits == gmax))
    gsum = jnp.sum(jnp.where(gmask, jnp.exp(logits - gmax), 0.0), axis=-1, keepdims=True)
    g_w = 1.0 / gsum
    e_lo = N_GROUPS + EXPERTS_PER_GROUP * gidx
    emask = (lane >= e_lo) & (lane < e_lo + EXPERTS_PER_GROUP)
    v1 = jnp.max(jnp.where(emask, logits, neg), axis=-1, keepdims=True)
    i1 = first(emask & (logits == v1))
    emask2 = emask & (lane != i1)
    v2 = jnp.max(jnp.where(emask2, logits, neg), axis=-1, keepdims=True)
    i2 = first(emask2 & (logits == v2))
    t = jnp.exp(v2 - v1)
    w1 = 1.0 / (1.0 + t)
    w2 = t * w1
    cw = jnp.where(lane == i1, w1 * g_w, jnp.where(lane == i2, w2 * g_w, 0.0))
    return cw, jnp.where(lane == gidx, 1.0, 0.0)


def _moe_kernel(x_ref, g_ref, wr_ref, br_ref, tril_ref, w13_ref, w2_ref, fg_ref, o_ref,
                xs_s, cws_s, ys_s, *, final_norm):
    x = x_ref[...]
    tm = x.shape[0]
    ns = xs_s.shape[0]
    hn = _rms(x, g_ref[...])
    hb = hn.astype(BF16)
    lo = (hn - hb.astype(F32)).astype(BF16)
    hb_w = _dot(hb, wr_ref[...])
    logits = (hb_w[:, :ROUTE_COLS] + hb_w[:, ROUTE_COLS:] + _dot(lo, wr_ref[:, :ROUTE_COLS])
              + br_ref[...])
    cw, ghot = _route(logits)

    lane = lax.broadcasted_iota(jnp.int32, (1, ROUTE_COLS), 1)
    rank = _dot(tril_ref[...], ghot.astype(BF16))
    total = jnp.sum(ghot, axis=0, keepdims=True)
    counts, offsets = [], []
    off = jnp.int32(0)
    for grp in range(N_GROUPS):
        cnt = jnp.sum(jnp.where(lane == grp, total, 0.0)).astype(jnp.int32)
        counts.append(cnt)
        offsets.append(off)
        off = off + ((cnt + (MOE_ALIGN - 1)) // MOE_ALIGN) * MOE_ALIGN
    off_vec = jnp.zeros((1, ROUTE_COLS), F32)
    for grp in range(1, N_GROUPS):
        off_vec = jnp.where(lane == grp, offsets[grp].astype(F32), off_vec)
    pos_terms = ghot * (rank + off_vec)
    pos_col = jnp.sum(pos_terms, axis=-1, keepdims=True)
    pos_row = _place_nt(jnp.ones((SUBLANES, ROUTE_COLS), BF16), pos_terms)[0:1]
    sort = (lax.broadcasted_iota(jnp.int32, (ns, tm), 0).astype(F32) == pos_row).astype(BF16)
    unsort = (lax.broadcasted_iota(jnp.int32, (tm, ns), 1).astype(F32) == pos_col).astype(BF16)

    xs_s[...] = _dot(sort, hb).astype(BF16)
    c1 = cw.astype(BF16)
    c2 = (cw - c1.astype(F32)).astype(BF16)
    cw_sorted = _dot(sort, jnp.concatenate([c1, c2], axis=1))
    cws_s[...] = cw_sorted[:, :ROUTE_COLS] + cw_sorted[:, ROUTE_COLS:]
    ys_s[...] = jnp.zeros_like(ys_s)

    for grp in range(N_GROUPS):
        def chunk(k, carry, grp=grp):
            r0 = pl.multiple_of(offsets[grp] + k * MOE_CHUNK, MOE_ALIGN)
            h = _dot(xs_s[pl.ds(r0, MOE_CHUNK), :], w13_ref[grp])
            cwb = cws_s[pl.ds(r0, MOE_CHUNK), :]
            cols = []
            for e in range(EXPERTS_PER_GROUP):
                col = N_GROUPS + grp * EXPERTS_PER_GROUP + e
                wcol = jnp.sum(jnp.where(lane == col, cwb, 0.0), axis=-1, keepdims=True)
                cols.append(jnp.broadcast_to(wcol, (MOE_CHUNK, D_EXPERT)))
            h1 = h[:, :GROUP_HIDDEN]
            hg = (h1 * _sigmoid(h1) * h[:, GROUP_HIDDEN:] * jnp.concatenate(cols, axis=1))
            ys_s[pl.ds(r0, MOE_CHUNK), :] = _dot(hg.astype(BF16), w2_ref[grp])
            return carry

        lax.fori_loop(0, (counts[grp] + (MOE_CHUNK - 1)) // MOE_CHUNK, chunk, 0)

    y = x + _dot(unsort, ys_s[...].astype(BF16))
    if final_norm:
        y = _rms(y, fg_ref[...])
    o_ref[...] = y


def _moe(layer, x2, g, w_route, b_route, w13, w2, final_g, tm, final_norm):
    t = x2.shape[0]
    ns = -(-(tm + N_GROUPS * (MOE_ALIGN - 1) + MOE_CHUNK) // LANES) * LANES
    row = pl.BlockSpec((tm, D_MODEL), lambda i: (i, 0))
    once = pl.Buffered(1)
    tril = jnp.asarray(np.tril(np.ones((tm, tm), np.float32), -1), BF16)
    return pl.pallas_call(
        functools.partial(_moe_kernel, final_norm=final_norm),
        grid=(t // tm,),
        in_specs=[row,
                  _layer_spec(layer, (1, D_MODEL)),
                  _layer_spec(layer, (D_MODEL, 2 * ROUTE_COLS)),
                  _layer_spec(layer, (1, ROUTE_COLS)),
                  _const_spec((tm, tm)),
                  _layer_spec(layer, (N_GROUPS, D_MODEL, 2 * GROUP_HIDDEN), pipeline_mode=once),
                  _layer_spec(layer, (N_GROUPS, GROUP_HIDDEN, D_MODEL), pipeline_mode=once),
                  _const_spec((1, D_MODEL))],
        out_specs=row,
        out_shape=jax.ShapeDtypeStruct((t, D_MODEL), F32),
        scratch_shapes=[pltpu.VMEM((ns, D_MODEL), BF16), pltpu.VMEM((ns, ROUTE_COLS), F32),
                        pltpu.VMEM((ns, D_MODEL), F32)],
        compiler_params=pltpu.CompilerParams(
            dimension_semantics=("parallel",), vmem_limit_bytes=VMEM_LIMIT),
        name="moe_final" if final_norm else "moe",
    )(x2, g, w_route, b_route, tril, w13, w2, final_g)


def _prepare(p):
    w_in = p['w_in']
    depth = w_in.shape[0]
    pad = ROUTE_COLS - N_GROUPS - N_EXPERTS
    w_route = jnp.concatenate(
        [p['moe_w_group'], p['moe_w_expert'], jnp.zeros((depth, D_MODEL, pad), F32)], axis=-1)
    wr_hi = w_route.astype(BF16)
    grp = lambda w: w.transpose(0, 1, 3, 2, 4).reshape(depth, N_GROUPS, D_MODEL, GROUP_HIDDEN)
    s5_a1, s5_a2, lamd, lamp, dskip = jax.vmap(_s5_params)(
        p['ssm_a_re'], p['ssm_a_im'], p['ssm_log_step'], p['ssm_b_re'], p['ssm_b_im'],
        p['ssm_c_re'], p['ssm_c_im'], p['ssm_d'])
    return dict(
        norm1_g=p['norm1_g'][:, None],
        w_main=w_in[:, :, :MAIN_COLS].astype(BF16),
        w_gate=(0.5 * w_in[:, :, MAIN_COLS:]).astype(BF16),
        attn_sink=p['attn_sink'],
        conv_w=p['lru_conv_w'],
        conv_b=p['lru_conv_b'][:, None],
        lru_wg=(0.5 * jnp.concatenate([_block_diag(p['lru_w_r']), _block_diag(p['lru_w_i'])],
                                      axis=-1)).astype(BF16),
        lru_bg=0.5 * jnp.concatenate([p['lru_b_r'], p['lru_b_i']], axis=-1)[:, :, None],
        lru_c=(-0.5 * LRU_C) * jax.nn.softplus(-p['lru_lambda'])[:, :, None],
        s5_a1=s5_a1, s5_a2=s5_a2, lamd=lamd, lamp=lamp, dskip=dskip,
        w_branch=(0.5 * p['w_branch']).astype(BF16),
        w_glu=(0.5 * p['ssm_w_glu']).astype(BF16),
        b_glu=0.5 * p['ssm_b_glu'][:, None],
        w_out=p['w_out'].astype(BF16),
        norm2_g=p['norm2_g'][:, None],
        w_route=jnp.concatenate([wr_hi, (w_route - wr_hi.astype(F32)).astype(BF16)], axis=-1),
        b_route=jnp.concatenate(
            [p['moe_b_group'], p['moe_b_expert'], jnp.zeros((depth, pad), F32)], axis=-1)[:, None],
        w13=jnp.concatenate([grp(p['moe_w1']), grp(p['moe_w3'])], axis=-1).astype(BF16),
        w2=p['moe_w2'].reshape(depth, N_GROUPS, GROUP_HIDDEN, D_MODEL).astype(BF16),
    )


def _layer(layer, x2, bsz, seq, w, final_g, final_norm):
    t = bsz * seq
    bw = BRANCH_WIDTH
    assert t % SSM_TILE == 0
    perm = _chunk_perm()
    qkv, xr, gr, u = _inproj(layer, x2, w['norm1_g'], w['w_main'], jnp.asarray(perm, BF16))
    h_sum = _lru(layer, xr.reshape(bsz, seq, bw), w['conv_w'], w['conv_b'], w['lru_wg'],
                 w['lru_bg'], w['lru_c']).reshape(t, bw)
    y_ssm = _s5(layer, u, w['s5_a1'], w['s5_a2'], w['lamd'], w['lamp'], w['dskip'],
                bsz, seq, _pick_tile(bsz, 2))
    x2 = _merge(layer, x2, w['norm1_g'], w['attn_sink'], qkv, seq, h_sum, gr, y_ssm,
                jnp.asarray(perm.T, BF16), w['w_gate'], w['w_branch'], w['w_glu'], w['b_glu'],
                w['w_out'])
    return _moe(layer, x2, w['norm2_g'], w['w_route'], w['b_route'], w['w13'], w['w2'],
                final_g, _pick_tile(t, 512), final_norm)


_PARAM_KEYS = ('norm1_g', 'w_in', 'attn_sink', 'lru_conv_w', 'lru_conv_b', 'lru_w_r', 'lru_b_r',
               'lru_w_i', 'lru_b_i', 'lru_lambda', 'ssm_a_re', 'ssm_a_im', 'ssm_log_step',
               'ssm_b_re', 'ssm_b_im', 'ssm_c_re', 'ssm_c_im', 'ssm_d', 'ssm_w_glu', 'ssm_b_glu',
               'w_branch', 'w_out', 'norm2_g', 'moe_w_group', 'moe_b_group', 'moe_w_expert',
               'moe_b_expert', 'moe_w1', 'moe_w3', 'moe_w2')


def kernel(x, norm1_g, w_in, attn_sink, lru_conv_w, lru_conv_b, lru_w_r, lru_b_r, lru_w_i, lru_b_i,
           lru_lambda, ssm_a_re, ssm_a_im, ssm_log_step, ssm_b_re, ssm_b_im, ssm_c_re, ssm_c_im,
           ssm_d, ssm_w_glu, ssm_b_glu, w_branch, w_out, norm2_g, moe_w_group, moe_b_group,
           moe_w_expert, moe_b_expert, moe_w1, moe_w3, moe_w2, final_norm_g):
    params = dict(zip(_PARAM_KEYS, (
        norm1_g, w_in, attn_sink, lru_conv_w, lru_conv_b, lru_w_r, lru_b_r, lru_w_i, lru_b_i,
        lru_lambda, ssm_a_re, ssm_a_im, ssm_log_step, ssm_b_re, ssm_b_im, ssm_c_re, ssm_c_im,
        ssm_d, ssm_w_glu, ssm_b_glu, w_branch, w_out, norm2_g, moe_w_group, moe_b_group,
        moe_w_expert, moe_b_expert, moe_w1, moe_w3, moe_w2)))
    bsz, seq, _ = x.shape
    depth = norm1_g.shape[0]
    w = _prepare(params)
    x2 = x.reshape(bsz * seq, D_MODEL)
    for layer in range(depth):
        x2 = _layer(layer, x2, bsz, seq, w, final_norm_g[None], final_norm=(layer == depth - 1))
    return x2.reshape(bsz, seq, D_MODEL)
```
